```python
import jax
import jax.numpy as jnp
from jax import lax
import numpy as np

D_MODEL = 2048
BATCH = 4
SEQ = 4096
DEPTH = 4

GRID_W = 64
Q_BLOCK = 128
NEG_INF = -1e30
LN_EPS = 1e-5

A_HEADS = 8
A_KV_HEADS = 2
A_HEAD_DIM = 128
ROPE_THETA = 10000.0
QK_EPS = 1e-6

B_HEAD_DIM = 64
B_WIDTH = 1024
B_HEADS = B_WIDTH // B_HEAD_DIM
DECAY_LORA = 64
ICLR_LORA = 64
GATE_LORA = 128
GN_EPS = 64e-5

C_PATTERNS = ((128, 1), (512, 4), (2048, 16))
C_GROUPS = 3
C_HEADS_PER_GROUP = 4
C_HEADS = C_GROUPS * C_HEADS_PER_GROUP
C_HEAD_DIM = 128
REL_BUCKETS = 32
REL_MAX_DISTANCE = 1024

N_GROUPS = 8
EXPERTS_PER_GROUP = 8
N_EXPERTS = N_GROUPS * EXPERTS_PER_GROUP
TOP_K = 2
EXPERT_FF = 384
MOE_BLOCK = 128

A_Q = A_HEADS * A_HEAD_DIM
A_KV = A_KV_HEADS * A_HEAD_DIM
A_COLS = A_Q + 2 * A_KV
B_COLS = 3 * B_WIDTH + 2 * DECAY_LORA + 2 * ICLR_LORA + GATE_LORA
C_QKV = C_HEADS * C_HEAD_DIM
C_COLS = 3 * C_QKV
N_BRANCHES = 3
GATE_COLS = N_BRANCHES * D_MODEL
IN_COLS = A_COLS + B_COLS + C_COLS + GATE_COLS
A_OUT = A_Q
B_OUT = B_WIDTH
C_OUT = C_HEADS_PER_GROUP * C_HEAD_DIM
BRANCH_ROWS = A_OUT + B_OUT + C_OUT

kernel_name = 'hybrid_gated_bidir_encoder'


def _split(t, sizes):
    return jnp.split(t, np.cumsum(sizes)[:-1].tolist(), axis=-1)


def layer_norm(x, g, b):
    xf = x.astype(jnp.float32)
    mu = jnp.mean(xf, -1, keepdims=True)
    var = jnp.mean(jnp.square(xf - mu), -1, keepdims=True)
    return ((xf - mu) * lax.rsqrt(var + LN_EPS) * g + b).astype(x.dtype)


def rms_heads(t, gain):
    t = t.astype(jnp.float32)
    return t * lax.rsqrt(jnp.mean(t * t, -1, keepdims=True) + QK_EPS) * gain.astype(jnp.float32)


def rope_1d(t, pos):
    n = t.shape[-1]
    inv = ROPE_THETA ** (-jnp.arange(0, n, 2, dtype=jnp.float32) / n)
    ang = pos.astype(jnp.float32)[:, None] * inv[None, :]
    cos, sin = jnp.cos(ang)[:, None, :], jnp.sin(ang)[:, None, :]
    t1, t2 = t[..., : n // 2], t[..., n // 2:]
    return jnp.concatenate([t1 * cos - t2 * sin, t1 * sin + t2 * cos], -1)


def gqa_axial_attention(q, k, v, q_gain, k_gain):
    bsz, seq = q.shape[:2]
    rows = seq // GRID_W
    row = jnp.repeat(jnp.arange(rows), GRID_W)
    col = jnp.arange(seq) % GRID_W
    half = A_HEAD_DIM // 2

    def axial(t):
        return jnp.concatenate([rope_1d(t[..., :half], row), rope_1d(t[..., half:], col)], -1)

    q = axial(rms_heads(q, q_gain)) * A_HEAD_DIM ** -0.5
    k = axial(rms_heads(k, k_gain))
    v = v.astype(jnp.float32)
    grp = A_HEADS // A_KV_HEADS
    nblk = seq // Q_BLOCK
    qb = jnp.moveaxis(q.reshape(bsz, nblk, Q_BLOCK, A_KV_HEADS, grp, A_HEAD_DIM), 1, 0)

    def block(qblk):
        p = jax.nn.softmax(jnp.einsum('bqkgd,bskd->bkgqs', qblk, k), axis=-1)
        return jnp.einsum('bkgqs,bskd->bqkgd', p, v)

    o = lax.map(block, qb)
    return jnp.moveaxis(o, 0, 1).reshape(bsz, seq, A_Q)


def rwkv7_bidirectional(y, mu_prev, mu_next, w0, w2, a0, a2, g2, k_k, k_a, r_k, gn_g, gn_b):
    f32 = jnp.float32
    y = y.astype(f32)
    bsz, seq = y.shape[:2]
    prev = jnp.pad(y, ((0, 0), (1, 0), (0, 0)))[:, :-1]
    nxt = jnp.pad(y, ((0, 0), (0, 1), (0, 0)))[:, 1:]
    y = y + mu_prev * (prev - y) + mu_next * (nxt - y)
    r, k, v, hw, ha, hg = _split(y, [B_WIDTH] * 3 + [2 * DECAY_LORA, 2 * ICLR_LORA, GATE_LORA])
    hw = hw.reshape(bsz, seq, 2, DECAY_LORA)
    ha = ha.reshape(bsz, seq, 2, ICLR_LORA)
    log_w = -jax.nn.softplus(-(w0 + jnp.einsum('bsdr,drc->bsdc', jnp.tanh(hw), w2))) - 0.5
    decay = jnp.exp(-jnp.exp(log_w))
    a = jax.nn.sigmoid(a0 + jnp.einsum('bsdr,drc->bsdc', ha, a2))
    g = jax.nn.sigmoid(hg) @ g2

    def heads(t):
        return t.reshape(t.shape[:-1] + (B_HEADS, B_HEAD_DIM))

    kk = heads(k * k_k)
    kk = kk * lax.rsqrt(jnp.sum(kk * kk, -1, keepdims=True) + 1e-12)
    k_dir = heads(k[:, :, None] * (1.0 + (a - 1.0) * k_a))
    kk_a = kk[:, :, None] * heads(a)
    r_h, v_h = heads(r), heads(v)

    def per_dir(t):
        t = jnp.stack([t[:, :, 0], jnp.flip(t[:, :, 1], axis=1)], axis=0)
        return jnp.moveaxis(t, 2, 0)

    def shared(t):
        return per_dir(jnp.stack([t, t], axis=2))

    def step(state, inp):
        r_t, w_t, k_t, v_t, kk_t, b_t = inp
        sa = jnp.einsum('dbhvk,dbhk->dbhv', state, kk_t)
        state = (state * w_t[..., None, :] - sa[..., :, None] * b_t[..., None, :]
                 + v_t[..., :, None] * k_t[..., None, :])
        return state, jnp.einsum('dbhvk,dbhk->dbhv', state, r_t)

    state0 = jnp.zeros((2, bsz, B_HEADS, B_HEAD_DIM, B_HEAD_DIM), f32)
    _, out = lax.scan(step, state0, (shared(r_h), per_dir(heads(decay)), per_dir(k_dir),
                                     shared(v_h), shared(kk), per_dir(kk_a)))
    out = jnp.moveaxis(out, 0, 2)
    o = out[0] + jnp.flip(out[1], axis=1)
    mu = jnp.mean(o, -1, keepdims=True)
    var = jnp.mean(jnp.square(o - mu), -1, keepdims=True)
    o = ((o - mu) * lax.rsqrt(var + GN_EPS)).reshape(bsz, seq, B_WIDTH) * gn_g + gn_b
    bonus = jnp.einsum('bshn,bsdhn->bsh', r_h * heads(r_k), k_dir)[..., None] * v_h
    return (o + bonus.reshape(bsz, seq, B_WIDTH)) * g


def t5_bucket(rel):
    nb = REL_BUCKETS // 2
    max_exact = nb // 2
    n = np.abs(rel)
    large = max_exact + (np.log(np.maximum(n, 1) / max_exact) / np.log(REL_MAX_DISTANCE / max_exact)
                         * (nb - max_exact)).astype(np.int32)
    large = np.minimum(large, nb - 1)
    return (rel > 0).astype(np.int32) * nb + np.where(n < max_exact, n, large)


def dilated_group(q, k, v, bias, delta, dilation, half):
    bsz, seq, h, dh = q.shape
    n = seq // dilation
    nb = -(-n // half)
    npad = nb * half

    def strided(t, lo, hi):
        t = t.reshape(bsz, n, dilation, h, dh)
        return jnp.pad(t, ((0, 0), (lo, hi), (0, 0), (0, 0), (0, 0)))

    qs = strided(q, 0, npad - n).reshape(bsz, nb, half, dilation, h, dh)

    def band(t):
        tp = strided(t, half, npad - n + half).reshape(bsz, nb + 2, half, dilation, h, dh)
        return jnp.concatenate([tp[:, :-2], tp[:, 1:-1], tp[:, 2:]], axis=2)

    kb, vb = band(k), band(v)
    key_m = jnp.arange(nb)[:, None] * half - half + jnp.arange(3 * half)[None, :]
    mask = (np.abs(delta) <= half)[None] & ((key_m >= 0) & (key_m < n))[:, None, :]
    s = jnp.einsum('bnqrhd,bnkrhd->bnrhqk', qs, kb) * dh ** -0.5 + bias
    s = jnp.where(mask[None, :, None, None], s, NEG_INF)
    lse = jax.nn.logsumexp(s, axis=-1)
    p = jnp.exp(s - lse[..., None])
    o = jnp.einsum('bnrhqk,bnkrhd->bnqrhd', p, vb)
    o = o.reshape(bsz, npad, dilation, h, dh)[:, :n].reshape(bsz, seq, h, dh)
    lse = jnp.moveaxis(lse, -1, 2).reshape(bsz, npad, dilation, h)[:, :n].reshape(bsz, seq, h)
    return o, lse


def dilated_attention(q, k, v, rel_bias):
    outs, lses = [], []
    for gi, (window, dilation) in enumerate(C_PATTERNS):
        half = window // (2 * dilation)
        delta = np.arange(3 * half)[None, :] - half - np.arange(half)[:, None]
        hs = slice(gi * C_HEADS_PER_GROUP, (gi + 1) * C_HEADS_PER_GROUP)
        bias = jnp.moveaxis(rel_bias[t5_bucket(delta * dilation)][..., hs], -1, 0).astype(jnp.float32)
        o, lse = dilated_group(q[:, :, hs], k[:, :, hs], v[:, :, hs], bias, delta, dilation, half)
        outs.append(o)
        lses.append(lse)
    wts = jax.nn.softmax(jnp.stack(lses), axis=0)[..., None]
    o = jnp.sum(jnp.stack(outs) * wts, axis=0)
    return o.reshape(o.shape[0], o.shape[1], C_OUT)


def hierarchical_moe(x, w_rg, b_rg, w_re, b_re, w_gate, w_up, w_down):
    f32 = jnp.float32
    bsz, seq, d = x.shape
    xf = x.reshape(-1, d)
    n_tok = xf.shape[0]
    xr = xf.astype(f32)
    grp_prob = jax.nn.softmax(xr @ w_rg.astype(f32) + b_rg.astype(f32), axis=-1)
    grp_w, grp = lax.top_k(grp_prob, 1)
    e_logits = (xr @ w_re.astype(f32) + b_re.astype(f32)).reshape(n_tok, N_GROUPS, EXPERTS_PER_GROUP)
    in_grp = jnp.take_along_axis(e_logits, grp[:, :, None], axis=1)[:, 0]
    top_l, top_i = lax.top_k(in_grp, TOP_K)
    gate = jax.nn.softmax(top_l, axis=-1) * grp_w
    expert = grp * EXPERTS_PER_GROUP + top_i
    flat_e = expert.reshape(-1)
    n_asg = flat_e.shape[0]
    order = jnp.argsort(flat_e)
    e_sorted = flat_e[order]
    tok_sorted = (order // TOP_K).astype(jnp.int32)
    gate_sorted = gate.reshape(-1)[order]
    counts = jnp.bincount(flat_e, length=N_EXPERTS)
    padded = (counts + MOE_BLOCK - 1) // MOE_BLOCK * MOE_BLOCK
    pad_end = jnp.cumsum(padded)
    pad_start = pad_end - padded
    start = jnp.cumsum(counts) - counts
    dest = pad_start[e_sorted] + jnp.arange(n_asg) - start[e_sorted]
    n_rows = (n_asg + MOE_BLOCK - 1) // MOE_BLOCK * MOE_BLOCK + N_EXPERTS * MOE_BLOCK
    n_blk = n_rows // MOE_BLOCK
    row_tok = jnp.full((n_rows,), n_tok, jnp.int32).at[dest].set(tok_sorted)
    blk_expert = jnp.minimum(jnp.searchsorted(pad_end, jnp.arange(n_blk) * MOE_BLOCK, side='right'),
                             N_EXPERTS - 1)
    x_pad = jnp.concatenate([xf, jnp.zeros((1, d), xf.dtype)], axis=0)

    def expert_block(args):
        toks, e = args
        xb = x_pad[toks]
        hid = jax.nn.silu(xb @ w_gate[e]) * (xb @ w_up[e])
        return hid @ w_down[e]

    y_rows = lax.map(expert_block, (row_tok.reshape(n_blk, MOE_BLOCK), blk_expert)).reshape(n_rows, d)
    y = jnp.zeros((n_tok, d), f32).at[tok_sorted].add(y_rows[dest].astype(f32) * gate_sorted[:, None])
    return y.reshape(bsz, seq, d).astype(x.dtype)


def setup_inputs(seed: int = 0) -> dict:
    key = jax.random.key(seed)
    ks = iter(jax.random.split(key, 40))
    f32 = jnp.float32
    L, D = DEPTH, D_MODEL
    beta = (8.0 * DEPTH) ** -0.25

    def nrm(shape, scale):
        return jax.random.normal(next(ks), shape, f32) * scale

    def uni(shape, lo, hi):
        return jax.random.uniform(next(ks), shape, f32, lo, hi)

    return {
        'x': nrm((BATCH, SEQ, D), 1.0),
        'w_in': nrm((L, D, IN_COLS), D ** -0.5),
        'w_branch': nrm((L, BRANCH_ROWS, D), A_OUT ** -0.5),
        'w_out': nrm((L, D, D), beta * D ** -0.5),
        'mu_prev': uni((L, B_COLS), 0.0, 0.5),
        'mu_next': uni((L, B_COLS), 0.0, 0.5),
        'rwkv_w0': uni((L, 2, B_WIDTH), -5.0, 0.0),
        'rwkv_w2': nrm((L, 2, DECAY_LORA, B_WIDTH), 0.5 * DECAY_LORA ** -0.5),
        'rwkv_a0': nrm((L, 2, B_WIDTH), 0.5),
        'rwkv_a2': nrm((L, 2, ICLR_LORA, B_WIDTH), 0.5 * ICLR_LORA ** -0.5),
        'rwkv_g2': nrm((L, GATE_LORA, B_WIDTH), GATE_LORA ** -0.5),
        'rwkv_k_k': 0.85 + nrm((L, B_WIDTH), 0.02),
        'rwkv_k_a': 1.0 + nrm((L, B_WIDTH), 0.02),
        'rwkv_r_k': nrm((L, B_WIDTH), 0.1),
        'rwkv_gn_g': 1.0 + nrm((L, B_WIDTH), 0.02),
        'rwkv_gn_b': nrm((L, B_WIDTH), 0.01),
        'q_norm': 1.0 + nrm((L, A_HEAD_DIM), 0.02),
        'k_norm': 1.0 + nrm((L, A_HEAD_DIM), 0.02),
        'rel_bias': nrm((REL_BUCKETS, C_HEADS), 0.5),
        'ln1_g': 1.0 + nrm((L, D), 0.02),
        'ln1_b': nrm((L, D), 0.01),
        'router_group_w': nrm((L, D, N_GROUPS), D ** -0.5),
        'router_group_b': nrm((L, N_GROUPS), 0.01),
        'router_expert_w': nrm((L, D, N_EXPERTS), D ** -0.5),
        'router_expert_b': nrm((L, N_EXPERTS), 0.01),
        'w_gate': nrm((L, N_EXPERTS, D, EXPERT_FF), D ** -0.5),
        'w_up': nrm((L, N_EXPERTS, D, EXPERT_FF), D ** -0.5),
        'w_down': nrm((L, N_EXPERTS, EXPERT_FF, D), beta * EXPERT_FF ** -0.5),
        'ln2_g': 1.0 + nrm((L, D), 0.02),
        'ln2_b': nrm((L, D), 0.01),
    }


def reference(x, w_in, w_branch, w_out, mu_prev, mu_next, rwkv_w0, rwkv_w2, rwkv_a0, rwkv_a2,
              rwkv_g2, rwkv_k_k, rwkv_k_a, rwkv_r_k, rwkv_gn_g, rwkv_gn_b, q_norm, k_norm, rel_bias,
              ln1_g, ln1_b, router_group_w, router_group_b, router_expert_w, router_expert_b,
              w_gate, w_up, w_down, ln2_g, ln2_b):
    f32 = jnp.float32
    alpha = (2.0 * DEPTH) ** 0.25
    bsz, seq, _ = x.shape
    for l in range(DEPTH):
        proj = jnp.einsum('bsd,dc->bsc', x, w_in[l])
        p_a, p_b, p_c, p_g = _split(proj, [A_COLS, B_COLS, C_COLS, GATE_COLS])
        qa, ka, va = _split(p_a, [A_Q, A_KV, A_KV])
        y_a = gqa_axial_attention(qa.reshape(bsz, seq, A_HEADS, A_HEAD_DIM),
                                  ka.reshape(bsz, seq, A_KV_HEADS, A_HEAD_DIM),
                                  va.reshape(bsz, seq, A_KV_HEADS, A_HEAD_DIM), q_norm[l], k_norm[l])
        y_b = rwkv7_bidirectional(p_b, mu_prev[l], mu_next[l], rwkv_w0[l], rwkv_w2[l], rwkv_a0[l],
                                  rwkv_a2[l], rwkv_g2[l], rwkv_k_k[l], rwkv_k_a[l], rwkv_r_k[l],
                                  rwkv_gn_g[l], rwkv_gn_b[l])
        qc, kc, vc = (t.reshape(bsz, seq, C_HEADS, C_HEAD_DIM).astype(f32) for t in _split(p_c, [C_QKV] * 3))
        y_c = dilated_attention(qc, kc, vc, rel_bias)
        gates = jax.nn.sigmoid(p_g.astype(f32)).reshape(bsz, seq, N_BRANCHES, D_MODEL)
        wb_a, wb_b, wb_c = jnp.split(w_branch[l], [A_OUT, A_OUT + B_OUT], axis=0)
        merged = (gates[:, :, 0] * (y_a.astype(x.dtype) @ wb_a)
                  + gates[:, :, 1] * (y_b.astype(x.dtype) @ wb_b)
                  + gates[:, :, 2] * (y_c.astype(x.dtype) @ wb_c))
        x = layer_norm(alpha * x + merged.astype(x.dtype) @ w_out[l], ln1_g[l], ln1_b[l])
        moe = hierarchical_moe(x, router_group_w[l], router_group_b[l], router_expert_w[l],
                               router_expert_b[l], w_gate[l], w_up[l], w_down[l])
        x = layer_norm(alpha * x + moe, ln2_g[l], ln2_b[l])
    return x
```

```python
import functools

import numpy as np
import jax
import jax.numpy as jnp
from jax import lax
from jax.experimental import pallas as pl
from jax.experimental.pallas import tpu as pltpu

F32 = jnp.float32
BF16 = jnp.bfloat16
HIGHEST = lax.Precision.HIGHEST

D_MODEL = 2048
DEPTH = 4
GRID_W = 64
NEG_INF = -1e30
LN_EPS = 1e-5

A_HEADS = 8
A_KV_HEADS = 2
A_GROUP = A_HEADS // A_KV_HEADS
HEAD_DIM = 128
ROPE_THETA = 10000.0
QK_EPS = 1e-6

B_HEAD_DIM = 64
B_WIDTH = 1024
B_HEADS = B_WIDTH // B_HEAD_DIM
LORA_PAIR = 128
GATE_LORA = 128
GN_EPS = 64e-5

C_PATTERNS = ((128, 1), (512, 4), (2048, 16))
C_GROUPS = 3
C_HEADS_PER_GROUP = 4
C_HEADS = C_GROUPS * C_HEADS_PER_GROUP
C_HALF = 64
REL_BUCKETS = 32
REL_MAX_DISTANCE = 1024

N_GROUPS = 8
EXPERTS_PER_GROUP = 8
N_EXPERTS = N_GROUPS * EXPERTS_PER_GROUP
TOP_K = 2
EXPERT_FF = 384

A_Q = A_HEADS * HEAD_DIM
A_KV = A_KV_HEADS * HEAD_DIM
A_COLS = A_Q + 2 * A_KV
B_COLS = 3 * B_WIDTH + 2 * LORA_PAIR + GATE_LORA
C_QKV = C_HEADS * HEAD_DIM
C_COLS = 3 * C_QKV
N_BRANCHES = 3
GATE_COLS = N_BRANCHES * D_MODEL
C_OUT = C_HEADS_PER_GROUP * HEAD_DIM

LANES = 128
SUBLANES = 8
MIB = 1024 * 1024

MM_ROWS = 1024
ATTN_A_Q_ROWS = 256
ATTN_C_Q = 128
ATTN_C_K = 2 * ATTN_C_Q
ATTN_C_TOKENS = ATTN_C_Q * C_PATTERNS[-1][1]
RWKV_PREP_ROWS = 256
RWKV_CHUNK = 64
RWKV_SCAN_ROWS = 128
MERGE_ROWS = 512
MERGE_COLS = 512
OUT_ROWS = 256
MOE_ROWS = 256
ROUTER_COLS = 128


def _params(n_grid, vmem_mib):
    return pltpu.CompilerParams(dimension_semantics=("arbitrary",) * n_grid,
                                vmem_limit_bytes=vmem_mib * MIB)


def _bdot(a, b):
    return jnp.dot(a.astype(BF16), b.astype(BF16), preferred_element_type=F32)


def _bdot_nt(a, b):
    return lax.dot_general(a.astype(BF16), b.astype(BF16), (((1,), (1,)), ((), ())),
                           preferred_element_type=F32)


def _bdot_tn(a, b):
    return lax.dot_general(a.astype(BF16), b.astype(BF16), (((0,), (0,)), ((), ())),
                           preferred_element_type=F32)


def _sigmoid(x):
    return 1.0 / (1.0 + jnp.exp(-x))


def _layer_norm(z, g, b):
    mu = jnp.mean(z, axis=-1, keepdims=True)
    zc = z - mu
    var = jnp.mean(zc * zc, axis=-1, keepdims=True)
    return zc * lax.rsqrt(var + LN_EPS) * g + b


def _mm_kernel(x_ref, w_ref, o_ref):
    o_ref[...] = jnp.dot(x_ref[...], w_ref[...], preferred_element_type=F32).astype(o_ref.dtype)


def _matmul(x, w, tn, out_dtype=F32):
    m, k = x.shape
    n = w.shape[1]
    tm = min(MM_ROWS, m)
    return pl.pallas_call(
        _mm_kernel,
        grid=(m // tm, n // tn),
        in_specs=[pl.BlockSpec((tm, k), lambda i, j: (i, 0)),
                  pl.BlockSpec((k, tn), lambda i, j: (0, j))],
        out_specs=pl.BlockSpec((tm, tn), lambda i, j: (i, j)),
        out_shape=jax.ShapeDtypeStruct((m, n), out_dtype),
        compiler_params=_params(2, 48),
        name="dense_proj",
    )(x, w)


def _rope_tables(seq):
    n = HEAD_DIM // 2
    inv = ROPE_THETA ** (-jnp.arange(0, n, 2, dtype=F32) / n)
    pos = jnp.arange(seq)
    ang_r = (pos // GRID_W).astype(F32)[:, None] * inv[None, :]
    ang_c = (pos % GRID_W).astype(F32)[:, None] * inv[None, :]
    cos = jnp.concatenate([jnp.cos(ang_r)] * 2 + [jnp.cos(ang_c)] * 2, axis=-1)
    sin = jnp.concatenate([-jnp.sin(ang_r), jnp.sin(ang_r), -jnp.sin(ang_c), jnp.sin(ang_c)], axis=-1)
    return cos, sin


def _norm_rope(t, gain, cos, sin):
    t = t * lax.rsqrt(jnp.mean(t * t, axis=-1, keepdims=True) + QK_EPS) * gain
    lane = lax.broadcasted_iota(jnp.int32, t.shape, 1)
    quarter = HEAD_DIM // 4
    partner = jnp.where(lane % (2 * quarter) < quarter,
                        pltpu.roll(t, HEAD_DIM - quarter, axis=1), pltpu.roll(t, quarter, axis=1))
    return t * cos + partner * sin


def _attn_a_kernel(q_ref, k_ref, v_ref, cq_ref, sq_ref, ck_ref, sk_ref, qg_ref, kg_ref, o_ref,
                   kp_ref, vp_ref):
    @pl.when((pl.program_id(2) == 0) & (pl.program_id(3) == 0))
    def _():
        kp_ref[...] = _norm_rope(k_ref[...], kg_ref[...], ck_ref[...], sk_ref[...]).astype(BF16)
        vp_ref[...] = v_ref[...].astype(BF16)

    q = _norm_rope(q_ref[...], qg_ref[...], cq_ref[...], sq_ref[...]) * HEAD_DIM ** -0.5
    s = _bdot_nt(q, kp_ref[...])
    m = jnp.max(s, axis=-1, keepdims=True)
    p = jnp.exp(s - m)
    l = jnp.sum(p, axis=-1, keepdims=True)
    o = jnp.dot(p.astype(BF16), vp_ref[...], preferred_element_type=F32)
    o_ref[...] = (o / l).astype(o_ref.dtype)


def _attention_a(p_a, q_gain, k_gain, bsz, seq):
    tq = min(ATTN_A_Q_ROWS, seq)
    nq = seq // tq
    cos, sin = _rope_tables(seq)
    k_col = A_HEADS
    v_col = A_HEADS + A_KV_HEADS
    q_tab = pl.BlockSpec((tq, HEAD_DIM), lambda b, kv, g, i: (i, 0))
    k_tab = pl.BlockSpec((seq, HEAD_DIM), lambda b, kv, g, i: (0, 0))
    gain = pl.BlockSpec((1, HEAD_DIM), lambda b, kv, g, i: (0, 0))
    return pl.pallas_call(
        _attn_a_kernel,
        grid=(bsz, A_KV_HEADS, A_GROUP, nq),
        in_specs=[
            pl.BlockSpec((tq, HEAD_DIM), lambda b, kv, g, i: (b * nq + i, kv * A_GROUP + g)),
            pl.BlockSpec((seq, HEAD_DIM), lambda b, kv, g, i: (b, k_col + kv)),
            pl.BlockSpec((seq, HEAD_DIM), lambda b, kv, g, i: (b, v_col + kv)),
            q_tab, q_tab, k_tab, k_tab, gain, gain,
        ],
        out_specs=pl.BlockSpec((tq, HEAD_DIM), lambda b, kv, g, i: (b * nq + i, kv * A_GROUP + g)),
        out_shape=jax.ShapeDtypeStruct((bsz * seq, A_Q), BF16),
        scratch_shapes=[pltpu.VMEM((seq, HEAD_DIM), BF16), pltpu.VMEM((seq, HEAD_DIM), BF16)],
        compiler_params=_params(4, 48),
        name="axial_gqa",
    )(p_a, p_a, p_a, cos, sin, cos, sin, q_gain.reshape(1, HEAD_DIM), k_gain.reshape(1, HEAD_DIM))


def _t5_bucket(rel):
    nb = REL_BUCKETS // 2
    max_exact = nb // 2
    n = np.abs(rel)
    large = max_exact + (np.log(np.maximum(n, 1) / max_exact) / np.log(REL_MAX_DISTANCE / max_exact)
                         * (nb - max_exact)).astype(np.int32)
    large = np.minimum(large, nb - 1)
    return (rel > 0).astype(np.int32) * nb + np.where(n < max_exact, n, large)


_C_WINDOW_OFFSETS = (0, -C_HALF, ATTN_C_Q - ATTN_C_K)


def _dilated_bias(rel_bias):
    per_group = []
    for gi, (_, dilation) in enumerate(C_PATTERNS):
        hs = slice(gi * C_HEADS_PER_GROUP, (gi + 1) * C_HEADS_PER_GROUP)
        variants = []
        for off in _C_WINDOW_OFFSETS:
            delta = off + np.arange(ATTN_C_K)[None, :] - np.arange(ATTN_C_Q)[:, None]
            vals = jnp.moveaxis(rel_bias[_t5_bucket(delta * dilation)][..., hs], -1, 0).astype(F32)
            variants.append(jnp.where((np.abs(delta) <= C_HALF)[None], vals, NEG_INF))
        per_group.append(jnp.stack(variants, axis=1))
    return jnp.stack(per_group, axis=0)


def _strided(start, size, stride):
    return pl.ds(start, size) if stride == 1 else pl.ds(start, size, stride=stride)


def _attn_c_kernel(q0, q1, q2, k0, k1, k2, v0, v1, v2, bias_ref, o_ref,
                   og0, og1, og2, ls0, ls1, ls2, *, seq):
    tile = pl.program_id(2)
    scale = HEAD_DIM ** -0.5
    groups = ((q0, k0, v0, og0, ls0), (q1, k1, v1, og1, ls1), (q2, k2, v2, og2, ls2))
    for gi, (q_ref, k_ref, v_ref, og_ref, ls_ref) in enumerate(groups):
        dil = C_PATTERNS[gi][1]
        n = seq // dil
        per_tile = ATTN_C_TOKENS // dil
        for res in range(dil):
            for blk in range(per_tile // ATTN_C_Q):
                m_local = blk * ATTN_C_Q
                m0 = tile * per_tile + m_local
                k_start = jnp.clip(m0 + _C_WINDOW_OFFSETS[1], 0, n - ATTN_C_K)
                variant = jnp.where(m0 == 0, 0, jnp.where(m0 == n - ATTN_C_Q, 2, 1))
                q_rows = _strided(m_local * dil + res, ATTN_C_Q, dil)
                k_rows = _strided(k_start * dil + res, ATTN_C_K, dil)
                s = _bdot_nt(q_ref[q_rows, :], k_ref[k_rows, :]) * scale + bias_ref[gi, variant]
                m = jnp.max(s, axis=-1, keepdims=True)
                p = jnp.exp(s - m)
                l = jnp.sum(p, axis=-1, keepdims=True)
                o = _bdot(p, v_ref[k_rows, :])
                og_ref[q_rows, :] = o / l
                ls_ref[q_rows, :] = jnp.broadcast_to(m + jnp.log(l), (ATTN_C_Q, HEAD_DIM))
    top = jnp.maximum(jnp.maximum(ls0[...], ls1[...]), ls2[...])
    e0, e1, e2 = jnp.exp(ls0[...] - top), jnp.exp(ls1[...] - top), jnp.exp(ls2[...] - top)
    o_ref[...] = ((e0 * og0[...] + e1 * og1[...] + e2 * og2[...]) / (e0 + e1 + e2)).astype(o_ref.dtype)


def _attention_c(p_c, rel_bias, bsz, seq):
    assert seq % ATTN_C_TOKENS == 0 and seq // C_PATTERNS[-1][1] >= ATTN_C_K
    nt = seq // ATTN_C_TOKENS
    bias = _dilated_bias(rel_bias)

    def head_col(part, gi):
        return lambda b, j, t: part * C_HEADS + gi * C_HEADS_PER_GROUP + j

    def q_spec(gi):
        col = head_col(0, gi)
        return pl.BlockSpec((ATTN_C_TOKENS, HEAD_DIM), lambda b, j, t: (b * nt + t, col(b, j, t)))

    def kv_spec(part, gi):
        col = head_col(part, gi)
        return pl.BlockSpec((seq, HEAD_DIM), lambda b, j, t: (b, col(b, j, t)))

    tile_buf = pltpu.VMEM((ATTN_C_TOKENS, HEAD_DIM), F32)
    return pl.pallas_call(
        functools.partial(_attn_c_kernel, seq=seq),
        grid=(bsz, C_HEADS_PER_GROUP, nt),
        in_specs=[q_spec(0), q_spec(1), q_spec(2),
                  kv_spec(1, 0), kv_spec(1, 1), kv_spec(1, 2),
                  kv_spec(2, 0), kv_spec(2, 1), kv_spec(2, 2),
                  pl.BlockSpec((C_GROUPS, None, len(_C_WINDOW_OFFSETS), ATTN_C_Q, ATTN_C_K),
                               lambda b, j, t: (0, j, 0, 0, 0))],
        out_specs=pl.BlockSpec((ATTN_C_TOKENS, HEAD_DIM), lambda b, j, t: (b * nt + t, j)),
        out_shape=jax.ShapeDtypeStruct((bsz * seq, C_OUT), BF16),
        scratch_shapes=[tile_buf] * 6,
        compiler_params=_params(3, 56),
        name="dilated_attn",
    )(*([p_c] * 9), bias)


def _head_ones():
    r = lax.broadcasted_iota(jnp.int32, (LANES, LANES), 0) // B_HEAD_DIM
    c = lax.broadcasted_iota(jnp.int32, (LANES, LANES), 1) // B_HEAD_DIM
    return (r == c).astype(F32)


def _head_sum(x, ones):
    parts = [jnp.dot(x[:, i:i + LANES], ones, precision=HIGHEST, preferred_element_type=F32)
             for i in range(0, x.shape[1], LANES)]
    return jnp.concatenate(parts, axis=1)


def _rwkv_prep_kernel(y_ref, yp_ref, yn_ref, mup_ref, mun_ref, w0_ref, w2_ref, a0_ref, a2_ref, g2_ref,
                      kk_ref, ka_ref, rk_ref,
                      r_o, v_o, kk_o, lw0_o, lw1_o, kd0_o, kd1_o, b0_o, b1_o, g_o, bonus_o,
                      *, blocks_per_seq):
    i = pl.program_id(0)
    ts = y_ref.shape[0]
    at_start = (i % blocks_per_seq) == 0
    at_end = (i % blocks_per_seq) == blocks_per_seq - 1
    rows = lax.broadcasted_iota(jnp.int32, (ts, 1), 0)

    def mixed(c0, c1):
        y = y_ref[:, c0:c1]
        before = jnp.where(at_start, 0.0, yp_ref[SUBLANES - 1:SUBLANES, c0:c1])
        after = jnp.where(at_end, 0.0, yn_ref[0:1, c0:c1])
        prev = jnp.where(rows == 0, before, pltpu.roll(y, 1, axis=0))
        nxt = jnp.where(rows == ts - 1, after, pltpu.roll(y, ts - 1, axis=0))
        return y + mup_ref[:, c0:c1] * (prev - y) + mun_ref[:, c0:c1] * (nxt - y)

    w = B_WIDTH
    r = mixed(0, w)
    k = mixed(w, 2 * w)
    v = mixed(2 * w, 3 * w)
    hw = mixed(3 * w, 3 * w + LORA_PAIR)
    ha = mixed(3 * w + LORA_PAIR, 3 * w + 2 * LORA_PAIR)
    hg = mixed(3 * w + 2 * LORA_PAIR, 3 * w + 2 * LORA_PAIR + GATE_LORA)
    ones = _head_ones()

    r_o[...] = r
    v_o[...] = v
    g_o[...] = _bdot(_sigmoid(hg), g2_ref[...])
    kk = k * kk_ref[...]
    kk = kk * lax.rsqrt(_head_sum(kk * kk, ones) + 1e-12)
    kk_o[...] = kk
    tanh_hw = jnp.tanh(hw)
    rrk = r * rk_ref[...]
    bonus = jnp.zeros_like(r)
    for d, (lw_o, kd_o, b_o) in enumerate(((lw0_o, kd0_o, b0_o), (lw1_o, kd1_o, b1_o))):
        z = -(w0_ref[d:d + 1, :] + _bdot(tanh_hw, w2_ref[d]))
        softplus = jnp.maximum(z, 0.0) + jnp.log(1.0 + jnp.exp(-jnp.abs(z)))
        lw_o[...] = -jnp.exp(-softplus - 0.5)
        a = _sigmoid(a0_ref[d:d + 1, :] + _bdot(ha, a2_ref[d]))
        kd = k * (1.0 + (a - 1.0) * ka_ref[...])
        kd_o[...] = kd
        b_o[...] = kk * a
        bonus = bonus + rrk * kd
    bonus_o[...] = _head_sum(bonus, ones) * v


def _rwkv_prep(p_b, mu_prev, mu_next, w0, w2, a0, a2, g2, k_k, k_a, r_k, seq):
    n_tok = p_b.shape[0]
    ts = min(RWKV_PREP_ROWS, seq)
    blocks_per_seq = seq // ts
    halo = ts // SUBLANES
    n_halo = n_tok // SUBLANES
    zeros = jnp.zeros_like(w2[0])
    w2p = jnp.stack([jnp.concatenate([w2[0], zeros], 0), jnp.concatenate([zeros, w2[1]], 0)]).astype(BF16)
    a2p = jnp.stack([jnp.concatenate([a2[0], zeros], 0), jnp.concatenate([zeros, a2[1]], 0)]).astype(BF16)
    full = lambda shape: pl.BlockSpec(shape, lambda i: (0,) * len(shape))
    row_blk = pl.BlockSpec((ts, B_WIDTH), lambda i: (i, 0))
    out = jax.ShapeDtypeStruct((n_tok, B_WIDTH), F32)
    return pl.pallas_call(
        functools.partial(_rwkv_prep_kernel, blocks_per_seq=blocks_per_seq),
        grid=(n_tok // ts,),
        in_specs=[
            pl.BlockSpec((ts, B_COLS), lambda i: (i, 0)),
            pl.BlockSpec((SUBLANES, B_COLS), lambda i: (jnp.maximum(i * halo - 1, 0), 0)),
            pl.BlockSpec((SUBLANES, B_COLS), lambda i: (jnp.minimum((i + 1) * halo, n_halo - 1), 0)),
            full((1, B_COLS)), full((1, B_COLS)),
            full((2, B_WIDTH)), full((2, LORA_PAIR, B_WIDTH)),
            full((2, B_WIDTH)), full((2, LORA_PAIR, B_WIDTH)),
            full((GATE_LORA, B_WIDTH)),
            full((1, B_WIDTH)), full((1, B_WIDTH)), full((1, B_WIDTH)),
        ],
        out_specs=[row_blk] * 11,
        out_shape=[out] * 11,
        compiler_params=_params(1, 56),
        name="rwkv_prep",
    )(p_b, p_b, p_b, mu_prev.reshape(1, B_COLS), mu_next.reshape(1, B_COLS), w0, w2p, a0, a2p,
      g2.astype(BF16), k_k.reshape(1, B_WIDTH), k_a.reshape(1, B_WIDTH), r_k.reshape(1, B_WIDTH))


def _rwkv_chunk(rt, kb, kt, bt, kt_end, bt_end, v, decay_end, state, strict, incl, eye):
    c = rt.shape[0]
    cross = _bdot_nt(jnp.concatenate([kb, rt], axis=0), jnp.concatenate([kt, bt], axis=0))
    a_k = jnp.where(strict, cross[:c, :c], 0.0)
    a_b = jnp.where(strict, cross[:c, c:], 0.0)
    a_rk = jnp.where(incl, cross[c:, :c], 0.0)
    a_rb = jnp.where(incl, cross[c:, c:], 0.0)
    inv = eye - a_b
    power = _bdot(a_b, a_b)
    span = 2
    while span < c:
        if 2 * span < c:
            both = _bdot(jnp.concatenate([inv, power], axis=0), power)
            inv = inv + both[:c]
            power = both[c:]
        else:
            inv = inv + _bdot(inv, power)
        span *= 2
    sol = _bdot(inv, jnp.concatenate([kb, _bdot(a_k, v)], axis=1))
    p, q = sol[:, :B_HEAD_DIM], sol[:, B_HEAD_DIM:]
    from_state = _bdot_nt(jnp.concatenate([p, rt], axis=0), state)
    u = from_state[:c] + q
    out = from_state[c:] + _bdot(jnp.concatenate([a_rk, -a_rb], axis=1), jnp.concatenate([v, u], axis=0))
    new_state = state * decay_end + _bdot_tn(jnp.concatenate([v, u], axis=0),
                                             jnp.concatenate([kt_end, -bt_end], axis=0))
    return out, new_state


def _rwkv_scan_kernel(r0, v0, kk0, lw0, kd0, b0, r1, v1, kk1, lw1, kd1, b1, o0_ref, o1_ref, s_ref, *, chunk):
    @pl.when(pl.program_id(2) == 0)
    def _():
        s_ref[...] = jnp.zeros_like(s_ref)

    n_chunks = r0.shape[0] // chunk
    heads = LANES // B_HEAD_DIM
    row = lax.broadcasted_iota(jnp.int32, (chunk, chunk), 0)
    col = lax.broadcasted_iota(jnp.int32, (chunk, chunk), 1)
    eye = (row == col).astype(F32)
    dirs = ((r0, v0, kk0, lw0, kd0, b0, o0_ref), (r1, v1, kk1, lw1, kd1, b1, o1_ref))
    for d, (r_ref, v_ref, kk_ref, lw_ref, kd_ref, b_ref, o_ref) in enumerate(dirs):
        backward = d == 1
        strict = (col > row) if backward else (col < row)
        incl = (col >= row) if backward else (col <= row)
        tri = incl.astype(F32)
        states = [s_ref[d * heads + h] for h in range(heads)]
        order = range(n_chunks - 1, -1, -1) if backward else range(n_chunks)
        for ci in order:
            rows = slice(ci * chunk, (ci + 1) * chunk)
            lw = lw_ref[rows, :]
            cum = jnp.dot(tri, lw, precision=HIGHEST, preferred_element_type=F32)
            end = 0 if backward else chunk - 1
            cum_end = cum[end:end + 1, :]
            grow, shrink, to_end = jnp.exp(cum), jnp.exp(-cum), jnp.exp(cum_end - cum)
            rt = r_ref[rows, :] * grow
            kb = kk_ref[rows, :] * jnp.exp(cum - lw)
            k = kd_ref[rows, :]
            b = b_ref[rows, :]
            kt, bt = k * shrink, b * shrink
            kt_end, bt_end = k * to_end, b * to_end
            decay_end = jnp.exp(cum_end)
            v = v_ref[rows, :]
            outs = []
            for h in range(heads):
                hs = slice(h * B_HEAD_DIM, (h + 1) * B_HEAD_DIM)
                o, states[h] = _rwkv_chunk(rt[:, hs], kb[:, hs], kt[:, hs], bt[:, hs], kt_end[:, hs],
                                           bt_end[:, hs], v[:, hs], decay_end[:, hs], states[h],
                                           strict, incl, eye)
                outs.append(o)
            o_ref[rows, :] = jnp.concatenate(outs, axis=1)
        for h in range(heads):
            s_ref[d * heads + h] = states[h]


def _rwkv_scan(r, v, kk, lw0, lw1, kd0, kd1, b0, b1, bsz, seq):
    rb = min(RWKV_SCAN_ROWS, seq)
    chunk = min(RWKV_CHUNK, rb)
    nb = seq // rb
    fwd = pl.BlockSpec((rb, LANES), lambda b, hp, c: (b * nb + c, hp))
    bwd = pl.BlockSpec((rb, LANES), lambda b, hp, c: (b * nb + nb - 1 - c, hp))
    out = jax.ShapeDtypeStruct(r.shape, F32)
    heads = LANES // B_HEAD_DIM
    return pl.pallas_call(
        functools.partial(_rwkv_scan_kernel, chunk=chunk),
        grid=(bsz, B_WIDTH // LANES, nb),
        in_specs=[fwd] * 6 + [bwd] * 6,
        out_specs=[fwd, bwd],
        out_shape=[out, out],
        scratch_shapes=[pltpu.VMEM((2 * heads, B_HEAD_DIM, B_HEAD_DIM), F32)],
        compiler_params=_params(3, 32),
        name="rwkv_scan",
    )(r, v, kk, lw0, kd0, b0, r, v, kk, lw1, kd1, b1)


def _rwkv_post_kernel(o0_ref, o1_ref, bonus_ref, g_ref, gn_g_ref, gn_b_ref, y_ref):
    ones = _head_ones()
    o = o0_ref[...] + o1_ref[...]
    mu = _head_sum(o, ones) * (1.0 / B_HEAD_DIM)
    oc = o - mu
    var = _head_sum(oc * oc, ones) * (1.0 / B_HEAD_DIM)
    normed = oc * lax.rsqrt(var + GN_EPS) * gn_g_ref[...] + gn_b_ref[...]
    y_ref[...] = ((normed + bonus_ref[...]) * g_ref[...]).astype(y_ref.dtype)


def _rwkv_post(o0, o1, bonus, g, gn_g, gn_b):
    n_tok = o0.shape[0]
    ts = min(RWKV_PREP_ROWS, n_tok)
    row_blk = pl.BlockSpec((ts, B_WIDTH), lambda i: (i, 0))
    vec = pl.BlockSpec((1, B_WIDTH), lambda i: (0, 0))
    return pl.pallas_call(
        _rwkv_post_kernel,
        grid=(n_tok // ts,),
        in_specs=[row_blk] * 4 + [vec, vec],
        out_specs=row_blk,
        out_shape=jax.ShapeDtypeStruct((n_tok, B_WIDTH), BF16),
        compiler_params=_params(1, 32),
        name="rwkv_post",
    )(o0, o1, bonus, g, gn_g.reshape(1, B_WIDTH), gn_b.reshape(1, B_WIDTH))


def _rwkv(p_b, mu_prev, mu_next, w0, w2, a0, a2, g2, k_k, k_a, r_k, gn_g, gn_b, bsz, seq):
    r, v, kk, lw0, lw1, kd0, kd1, b0, b1, g, bonus = _rwkv_prep(
        p_b, mu_prev, mu_next, w0, w2, a0, a2, g2, k_k, k_a, r_k, seq)
    o0, o1 = _rwkv_scan(r, v, kk, lw0, lw1, kd0, kd1, b0, b1, bsz, seq)
    return _rwkv_post(o0, o1, bonus, g, gn_g, gn_b)


def _merge_kernel(ya_ref, yb_ref, yc_ref, ga_ref, gb_ref, gc_ref, wa_ref, wb_ref, wc_ref, o_ref):
    def branch(y_ref, w_ref, gate_ref):
        return _sigmoid(gate_ref[...]) * jnp.dot(y_ref[...], w_ref[...], preferred_element_type=F32)

    merged = branch(ya_ref, wa_ref, ga_ref) + branch(yb_ref, wb_ref, gb_ref) + branch(yc_ref, wc_ref, gc_ref)
    o_ref[...] = merged.astype(o_ref.dtype)


def _merge(y_a, y_b, y_c, p_g, wb_a, wb_b, wb_c):
    n_tok = y_a.shape[0]
    tm = min(MERGE_ROWS, n_tok)
    tn = MERGE_COLS
    nj = D_MODEL // tn

    def rows(width):
        return pl.BlockSpec((tm, width), lambda i, j: (i, 0))

    def gate(branch):
        return pl.BlockSpec((tm, tn), lambda i, j: (i, branch * nj + j))

    def weight(depth):
        return pl.BlockSpec((depth, tn), lambda i, j: (0, j))

    return pl.pallas_call(
        _merge_kernel,
        grid=(n_tok // tm, nj),
        in_specs=[rows(A_Q), rows(B_WIDTH), rows(C_OUT), gate(0), gate(1), gate(2),
                  weight(A_Q), weight(B_WIDTH), weight(C_OUT)],
        out_specs=pl.BlockSpec((tm, tn), lambda i, j: (i, j)),
        out_shape=jax.ShapeDtypeStruct((n_tok, D_MODEL), BF16),
        compiler_params=_params(2, 32),
        name="gated_merge",
    )(y_a, y_b, y_c, p_g, p_g, p_g, wb_a, wb_b, wb_c)


def _out_proj_kernel(m_ref, x_ref, w_ref, g_ref, b_ref, wr_ref, br_ref, x_o, xb_o, logit_o, *, alpha):
    z = alpha * x_ref[...] + jnp.dot(m_ref[...], w_ref[...], preferred_element_type=F32)
    x1 = _layer_norm(z, g_ref[...], b_ref[...])
    x_o[...] = x1
    xb_o[...] = x1.astype(BF16)
    logit_o[...] = jnp.dot(x1, wr_ref[...], precision=HIGHEST, preferred_element_type=F32) + br_ref[...]


def _out_proj(merged, x, w_out, ln_g, ln_b, w_router, b_router, alpha):
    n_tok = x.shape[0]
    tm = min(OUT_ROWS, n_tok)
    rows = lambda width: pl.BlockSpec((tm, width), lambda i: (i, 0))
    full = lambda shape: pl.BlockSpec(shape, lambda i: (0, 0))
    return pl.pallas_call(
        functools.partial(_out_proj_kernel, alpha=alpha),
        grid=(n_tok // tm,),
        in_specs=[rows(D_MODEL), rows(D_MODEL), full((D_MODEL, D_MODEL)), full((1, D_MODEL)),
                  full((1, D_MODEL)), full((D_MODEL, ROUTER_COLS)), full((1, ROUTER_COLS))],
        out_specs=[rows(D_MODEL), rows(D_MODEL), rows(ROUTER_COLS)],
        out_shape=[jax.ShapeDtypeStruct((n_tok, D_MODEL), F32),
                   jax.ShapeDtypeStruct((n_tok, D_MODEL), BF16),
                   jax.ShapeDtypeStruct((n_tok, ROUTER_COLS), F32)],
        compiler_params=_params(1, 48),
        name="out_proj_ln_router",
    )(merged, x, w_out, ln_g.reshape(1, D_MODEL), ln_b.reshape(1, D_MODEL), w_router, b_router)


def _route(logits):
    n_tok = logits.shape[0]
    grp_prob = jax.nn.softmax(logits[:, :N_GROUPS], axis=-1)
    grp_w, grp = lax.top_k(grp_prob, 1)
    e_logits = logits[:, N_GROUPS:N_GROUPS + N_EXPERTS].reshape(n_tok, N_GROUPS, EXPERTS_PER_GROUP)
    in_grp = jnp.take_along_axis(e_logits, grp[:, :, None], axis=1)[:, 0]
    top_l, top_i = lax.top_k(in_grp, TOP_K)
    gate = jax.nn.softmax(top_l, axis=-1) * grp_w
    flat_e = (grp * EXPERTS_PER_GROUP + top_i).reshape(-1)
    n_asg = flat_e.shape[0]
    order = jnp.argsort(flat_e)
    e_sorted = flat_e[order]
    counts = jnp.bincount(flat_e, length=N_EXPERTS)
    padded = (counts + MOE_ROWS - 1) // MOE_ROWS * MOE_ROWS
    pad_end = jnp.cumsum(padded)
    start = jnp.cumsum(counts) - counts
    dest_sorted = (pad_end - padded)[e_sorted] + jnp.arange(n_asg) - start[e_sorted]
    n_rows = (n_asg + MOE_ROWS - 1) // MOE_ROWS * MOE_ROWS + N_EXPERTS * MOE_ROWS
    n_blk = n_rows // MOE_ROWS
    row_tok = jnp.zeros((n_rows,), jnp.int32).at[dest_sorted].set((order // TOP_K).astype(jnp.int32))
    dest = jnp.zeros((n_asg,), jnp.int32).at[order].set(dest_sorted.astype(jnp.int32))
    blk_expert = jnp.minimum(jnp.searchsorted(pad_end, jnp.arange(n_blk) * MOE_ROWS, side='right'),
                             N_EXPERTS - 1).astype(jnp.int32)
    return gate, dest.reshape(n_tok, TOP_K), row_tok, blk_expert


def _expert_kernel(blk_expert_ref, x_ref, wg_ref, wu_ref, wd_ref, o_ref):
    x = x_ref[...]
    gate = jnp.dot(x, wg_ref[...].astype(BF16), preferred_element_type=F32)
    up = jnp.dot(x, wu_ref[...].astype(BF16), preferred_element_type=F32)
    hid = gate * _sigmoid(gate) * up
    o_ref[...] = jnp.dot(hid.astype(BF16), wd_ref[...].astype(BF16), preferred_element_type=F32)


def _experts(x_rows, blk_expert, w_gate, w_up, w_down):
    n_rows = x_rows.shape[0]
    grid_spec = pltpu.PrefetchScalarGridSpec(
        num_scalar_prefetch=1,
        grid=(n_rows // MOE_ROWS,),
        in_specs=[pl.BlockSpec((MOE_ROWS, D_MODEL), lambda i, e: (i, 0)),
                  pl.BlockSpec((None, D_MODEL, EXPERT_FF), lambda i, e: (e[i], 0, 0)),
                  pl.BlockSpec((None, D_MODEL, EXPERT_FF), lambda i, e: (e[i], 0, 0)),
                  pl.BlockSpec((None, EXPERT_FF, D_MODEL), lambda i, e: (e[i], 0, 0))],
        out_specs=pl.BlockSpec((MOE_ROWS, D_MODEL), lambda i, e: (i, 0)),
    )
    return pl.pallas_call(
        _expert_kernel,
        grid_spec=grid_spec,
        out_shape=jax.ShapeDtypeStruct((n_rows, D_MODEL), F32),
        compiler_params=_params(1, 56),
        name="expert_ffn",
    )(blk_expert, x_rows, w_gate, w_up, w_down)


def _combine_kernel(x_ref, ya_ref, yb_ref, gate_ref, g_ref, b_ref, x_o, xb_o, *, alpha):
    gate = gate_ref[...]
    moe = gate[:, 0:1] * ya_ref[...] + gate[:, 1:2] * yb_ref[...]
    x2 = _layer_norm(alpha * x_ref[...] + moe, g_ref[...], b_ref[...])
    x_o[...] = x2
    xb_o[...] = x2.astype(BF16)


def _combine(x1, y_first, y_second, gate, ln_g, ln_b, alpha):
    n_tok = x1.shape[0]
    tm = min(OUT_ROWS, n_tok)
    rows = lambda width: pl.BlockSpec((tm, width), lambda i: (i, 0))
    vec = pl.BlockSpec((1, D_MODEL), lambda i: (0, 0))
    gate_lanes = jnp.pad(gate, ((0, 0), (0, LANES - TOP_K)))
    return pl.pallas_call(
        functools.partial(_combine_kernel, alpha=alpha),
        grid=(n_tok // tm,),
        in_specs=[rows(D_MODEL), rows(D_MODEL), rows(D_MODEL), rows(LANES), vec, vec],
        out_specs=[rows(D_MODEL), rows(D_MODEL)],
        out_shape=[jax.ShapeDtypeStruct((n_tok, D_MODEL), F32),
                   jax.ShapeDtypeStruct((n_tok, D_MODEL), BF16)],
        compiler_params=_params(1, 32),
        name="moe_combine_ln",
    )(x1, y_first, y_second, gate_lanes, ln_g.reshape(1, D_MODEL), ln_b.reshape(1, D_MODEL))


def _moe(x1, x1_bf16, logits, w_gate, w_up, w_down, ln_g, ln_b, alpha):
    gate, dest, row_tok, blk_expert = _route(logits)
    y_rows = _experts(x1_bf16[row_tok], blk_expert, w_gate, w_up, w_down)
    return _combine(x1, y_rows[dest[:, 0]], y_rows[dest[:, 1]], gate, ln_g, ln_b, alpha)


def kernel(x, w_in, w_branch, w_out, mu_prev, mu_next, rwkv_w0, rwkv_w2, rwkv_a0, rwkv_a2, rwkv_g2, rwkv_k_k, rwkv_k_a, rwkv_r_k, rwkv_gn_g, rwkv_gn_b, q_norm, k_norm, rel_bias, ln1_g, ln1_b, router_group_w, router_group_b, router_expert_w, router_expert_b, w_gate, w_up, w_down, ln2_g, ln2_b):
    bsz, seq, _ = x.shape
    depth = w_in.shape[0]
    alpha = (2.0 * depth) ** 0.25
    xf = x.reshape(bsz * seq, D_MODEL)
    xb = xf.astype(BF16)
    col_b = A_COLS
    col_c = A_COLS + B_COLS
    col_g = A_COLS + B_COLS + C_COLS
    for l in range(depth):
        w_l = w_in[l]
        p_a = _matmul(xb, w_l[:, :col_b].astype(BF16), 768)
        p_b = _matmul(xb, w_l[:, col_b:col_c].astype(BF16), 1152)
        p_c = _matmul(xb, w_l[:, col_c:col_g].astype(BF16), 1152)
        p_g = _matmul(xb, w_l[:, col_g:].astype(BF16), 1024)
        y_a = _attention_a(p_a, q_norm[l], k_norm[l], bsz, seq)
        y_b = _rwkv(p_b, mu_prev[l], mu_next[l], rwkv_w0[l], rwkv_w2[l], rwkv_a0[l], rwkv_a2[l],
                    rwkv_g2[l], rwkv_k_k[l], rwkv_k_a[l], rwkv_r_k[l], rwkv_gn_g[l], rwkv_gn_b[l],
                    bsz, seq)
        y_c = _attention_c(p_c, rel_bias, bsz, seq)
        wb = w_branch[l].astype(BF16)
        merged = _merge(y_a, y_b, y_c, p_g, wb[:A_Q], wb[A_Q:A_Q + B_WIDTH], wb[A_Q + B_WIDTH:])
        w_router = jnp.pad(jnp.concatenate([router_group_w[l], router_expert_w[l]], axis=1),
                           ((0, 0), (0, ROUTER_COLS - N_GROUPS - N_EXPERTS)))
        b_router = jnp.pad(jnp.concatenate([router_group_b[l], router_expert_b[l]]),
                           (0, ROUTER_COLS - N_GROUPS - N_EXPERTS)).reshape(1, ROUTER_COLS)
        x1, x1b, logits = _out_proj(merged, xf, w_out[l].astype(BF16), ln1_g[l], ln1_b[l],
                                    w_router, b_router, alpha)
        xf, xb = _moe(x1, x1b, logits, w_gate[l], w_up[l], w_down[l], ln2_g[l], ln2_b[l], alpha)
    return xf.reshape(bsz, seq, D_MODEL)
```

```python
import functools

import numpy as np
import jax
import jax.numpy as jnp
from jax import lax
from jax.experimental import pallas as pl
from jax.experimental.pallas import tpu as pltpu

F32 = jnp.float32
BF16 = jnp.bfloat16
HIGHEST = lax.Precision.HIGHEST

D_MODEL = 2048
DEPTH = 4
GRID_W = 64
NEG_INF = -1e30
LN_EPS = 1e-5

A_HEADS = 8
A_KV_HEADS = 2
A_GROUP = A_HEADS // A_KV_HEADS
HEAD_DIM = 128
ROPE_THETA = 10000.0
QK_EPS = 1e-6

B_HEAD_DIM = 64
B_WIDTH = 1024
B_HEADS = B_WIDTH // B_HEAD_DIM
LORA_PAIR = 128
GATE_LORA = 128
GN_EPS = 64e-5

C_PATTERNS = ((128, 1), (512, 4), (2048, 16))
C_GROUPS = 3
C_HEADS_PER_GROUP = 4
C_HEADS = C_GROUPS * C_HEADS_PER_GROUP
C_HALF = 64
REL_BUCKETS = 32
REL_MAX_DISTANCE = 1024

N_GROUPS = 8
EXPERTS_PER_GROUP = 8
N_EXPERTS = N_GROUPS * EXPERTS_PER_GROUP
TOP_K = 2
EXPERT_FF = 384

A_Q = A_HEADS * HEAD_DIM
A_KV = A_KV_HEADS * HEAD_DIM
A_COLS = A_Q + 2 * A_KV
B_COLS = 3 * B_WIDTH + 2 * LORA_PAIR + GATE_LORA
C_QKV = C_HEADS * HEAD_DIM
C_COLS = 3 * C_QKV
N_BRANCHES = 3
GATE_COLS = N_BRANCHES * D_MODEL
C_OUT = C_HEADS_PER_GROUP * HEAD_DIM

LANES = 128
SUBLANES = 8
MIB = 1024 * 1024

MM_ROWS = 1024
ATTN_A_Q_ROWS = 256
ATTN_C_Q = 128
ATTN_C_K = 2 * ATTN_C_Q
ATTN_C_TOKENS = ATTN_C_Q * C_PATTERNS[-1][1]
RWKV_PREP_ROWS = 256
RWKV_CHUNK = 64
RWKV_SCAN_ROWS = 256
RWKV_SCAN_LANES = 256
MERGE_ROWS = 512
MERGE_COLS = 512
OUT_ROWS = 256
MOE_ROWS = 256
ROUTER_COLS = 128


def _params(n_grid, vmem_mib):
    return pltpu.CompilerParams(dimension_semantics=("arbitrary",) * n_grid,
                                vmem_limit_bytes=vmem_mib * MIB)


def _bdot(a, b):
    return jnp.dot(a.astype(BF16), b.astype(BF16), preferred_element_type=F32)


def _bdot_nt(a, b):
    return lax.dot_general(a.astype(BF16), b.astype(BF16), (((1,), (1,)), ((), ())),
                           preferred_element_type=F32)


def _bdot_tn(a, b):
    return lax.dot_general(a.astype(BF16), b.astype(BF16), (((0,), (0,)), ((), ())),
                           preferred_element_type=F32)


def _sigmoid(x):
    return 1.0 / (1.0 + jnp.exp(-x))


def _layer_norm(z, g, b):
    mu = jnp.mean(z, axis=-1, keepdims=True)
    zc = z - mu
    var = jnp.mean(zc * zc, axis=-1, keepdims=True)
    return zc * lax.rsqrt(var + LN_EPS) * g + b


def _mm_kernel(x_ref, w_ref, o_ref):
    o_ref[...] = jnp.dot(x_ref[...], w_ref[...], preferred_element_type=F32).astype(o_ref.dtype)


def _matmul(x, w, tn, out_dtype=F32):
    m, k = x.shape
    n = w.shape[1]
    tm = min(MM_ROWS, m)
    return pl.pallas_call(
        _mm_kernel,
        grid=(m // tm, n // tn),
        in_specs=[pl.BlockSpec((tm, k), lambda i, j: (i, 0)),
                  pl.BlockSpec((k, tn), lambda i, j: (0, j))],
        out_specs=pl.BlockSpec((tm, tn), lambda i, j: (i, j)),
        out_shape=jax.ShapeDtypeStruct((m, n), out_dtype),
        compiler_params=_params(2, 48),
        name="dense_proj",
    )(x, w)


def _rope_tables(seq):
    n = HEAD_DIM // 2
    inv = ROPE_THETA ** (-jnp.arange(0, n, 2, dtype=F32) / n)
    pos = jnp.arange(seq)
    ang_r = (pos // GRID_W).astype(F32)[:, None] * inv[None, :]
    ang_c = (pos % GRID_W).astype(F32)[:, None] * inv[None, :]
    cos = jnp.concatenate([jnp.cos(ang_r)] * 2 + [jnp.cos(ang_c)] * 2, axis=-1)
    sin = jnp.concatenate([-jnp.sin(ang_r), jnp.sin(ang_r), -jnp.sin(ang_c), jnp.sin(ang_c)], axis=-1)
    return cos, sin


def _norm_rope(t, gain, cos, sin):
    t = t * lax.rsqrt(jnp.mean(t * t, axis=-1, keepdims=True) + QK_EPS) * gain
    lane = lax.broadcasted_iota(jnp.int32, t.shape, 1)
    quarter = HEAD_DIM // 4
    partner = jnp.where(lane % (2 * quarter) < quarter,
                        pltpu.roll(t, HEAD_DIM - quarter, axis=1), pltpu.roll(t, quarter, axis=1))
    return t * cos + partner * sin


def _attn_a_kernel(q_ref, k_ref, v_ref, cq_ref, sq_ref, ck_ref, sk_ref, qg_ref, kg_ref, o_ref,
                   kp_ref, vp_ref):
    @pl.when((pl.program_id(2) == 0) & (pl.program_id(3) == 0))
    def _():
        kp_ref[...] = _norm_rope(k_ref[...], kg_ref[...], ck_ref[...], sk_ref[...]).astype(BF16)
        vp_ref[...] = v_ref[...].astype(BF16)

    q = _norm_rope(q_ref[...], qg_ref[...], cq_ref[...], sq_ref[...]) * HEAD_DIM ** -0.5
    s = _bdot_nt(q, kp_ref[...])
    m = jnp.max(s, axis=-1, keepdims=True)
    p = jnp.exp(s - m)
    l = jnp.sum(p, axis=-1, keepdims=True)
    o = jnp.dot(p.astype(BF16), vp_ref[...], preferred_element_type=F32)
    o_ref[...] = (o / l).astype(o_ref.dtype)


def _attention_a(p_a, q_gain, k_gain, bsz, seq):
    tq = min(ATTN_A_Q_ROWS, seq)
    nq = seq // tq
    cos, sin = _rope_tables(seq)
    k_col = A_HEADS
    v_col = A_HEADS + A_KV_HEADS
    q_tab = pl.BlockSpec((tq, HEAD_DIM), lambda b, kv, g, i: (i, 0))
    k_tab = pl.BlockSpec((seq, HEAD_DIM), lambda b, kv, g, i: (0, 0))
    gain = pl.BlockSpec((1, HEAD_DIM), lambda b, kv, g, i: (0, 0))
    return pl.pallas_call(
        _attn_a_kernel,
        grid=(bsz, A_KV_HEADS, A_GROUP, nq),
        in_specs=[
            pl.BlockSpec((tq, HEAD_DIM), lambda b, kv, g, i: (b * nq + i, kv * A_GROUP + g)),
            pl.BlockSpec((seq, HEAD_DIM), lambda b, kv, g, i: (b, k_col + kv)),
            pl.BlockSpec((seq, HEAD_DIM), lambda b, kv, g, i: (b, v_col + kv)),
            q_tab, q_tab, k_tab, k_tab, gain, gain,
        ],
        out_specs=pl.BlockSpec((tq, HEAD_DIM), lambda b, kv, g, i: (b * nq + i, kv * A_GROUP + g)),
        out_shape=jax.ShapeDtypeStruct((bsz * seq, A_Q), BF16),
        scratch_shapes=[pltpu.VMEM((seq, HEAD_DIM), BF16), pltpu.VMEM((seq, HEAD_DIM), BF16)],
        compiler_params=_params(4, 48),
        name="axial_gqa",
    )(p_a, p_a, p_a, cos, sin, cos, sin, q_gain.reshape(1, HEAD_DIM), k_gain.reshape(1, HEAD_DIM))


def _t5_bucket(rel):
    nb = REL_BUCKETS // 2
    max_exact = nb // 2
    n = np.abs(rel)
    large = max_exact + (np.log(np.maximum(n, 1) / max_exact) / np.log(REL_MAX_DISTANCE / max_exact)
                         * (nb - max_exact)).astype(np.int32)
    large = np.minimum(large, nb - 1)
    return (rel > 0).astype(np.int32) * nb + np.where(n < max_exact, n, large)


_C_WINDOW_OFFSETS = (0, -C_HALF, ATTN_C_Q - ATTN_C_K)


def _dilated_bias(rel_bias):
    per_group = []
    for gi, (_, dilation) in enumerate(C_PATTERNS):
        hs = slice(gi * C_HEADS_PER_GROUP, (gi + 1) * C_HEADS_PER_GROUP)
        variants = []
        for off in _C_WINDOW_OFFSETS:
            delta = off + np.arange(ATTN_C_K)[None, :] - np.arange(ATTN_C_Q)[:, None]
            vals = jnp.moveaxis(rel_bias[_t5_bucket(delta * dilation)][..., hs], -1, 0).astype(F32)
            variants.append(jnp.where((np.abs(delta) <= C_HALF)[None], vals, NEG_INF))
        per_group.append(jnp.stack(variants, axis=1))
    return jnp.stack(per_group, axis=0)


def _strided(start, size, stride):
    return pl.ds(start, size) if stride == 1 else pl.ds(start, size, stride=stride)


def _attn_c_kernel(q0, q1, q2, k0, k1, k2, v0, v1, v2, bias_ref, o_ref,
                   og0, og1, og2, ls0, ls1, ls2, *, seq):
    tile = pl.program_id(2)
    scale = HEAD_DIM ** -0.5
    groups = ((q0, k0, v0, og0, ls0), (q1, k1, v1, og1, ls1), (q2, k2, v2, og2, ls2))
    for gi, (q_ref, k_ref, v_ref, og_ref, ls_ref) in enumerate(groups):
        dil = C_PATTERNS[gi][1]
        n = seq // dil
        per_tile = ATTN_C_TOKENS // dil
        for res in range(dil):
            for blk in range(per_tile // ATTN_C_Q):
                m_local = blk * ATTN_C_Q
                m0 = tile * per_tile + m_local
                k_start = jnp.clip(m0 + _C_WINDOW_OFFSETS[1], 0, n - ATTN_C_K)
                variant = jnp.where(m0 == 0, 0, jnp.where(m0 == n - ATTN_C_Q, 2, 1))
                q_rows = _strided(m_local * dil + res, ATTN_C_Q, dil)
                k_rows = _strided(k_start * dil + res, ATTN_C_K, dil)
                s = _bdot_nt(q_ref[q_rows, :], k_ref[k_rows, :]) * scale + bias_ref[gi, variant]
                m = jnp.max(s, axis=-1, keepdims=True)
                p = jnp.exp(s - m)
                l = jnp.sum(p, axis=-1, keepdims=True)
                o = _bdot(p, v_ref[k_rows, :])
                og_ref[q_rows, :] = o / l
                ls_ref[q_rows, :] = jnp.broadcast_to(m + jnp.log(l), (ATTN_C_Q, HEAD_DIM))
    top = jnp.maximum(jnp.maximum(ls0[...], ls1[...]), ls2[...])
    e0, e1, e2 = jnp.exp(ls0[...] - top), jnp.exp(ls1[...] - top), jnp.exp(ls2[...] - top)
    o_ref[...] = ((e0 * og0[...] + e1 * og1[...] + e2 * og2[...]) / (e0 + e1 + e2)).astype(o_ref.dtype)


def _attention_c(p_c, rel_bias, bsz, seq):
    assert seq % ATTN_C_TOKENS == 0 and seq // C_PATTERNS[-1][1] >= ATTN_C_K
    nt = seq // ATTN_C_TOKENS
    bias = _dilated_bias(rel_bias)

    def head_col(part, gi):
        return lambda b, j, t: part * C_HEADS + gi * C_HEADS_PER_GROUP + j

    def q_spec(gi):
        col = head_col(0, gi)
        return pl.BlockSpec((ATTN_C_TOKENS, HEAD_DIM), lambda b, j, t: (b * nt + t, col(b, j, t)))

    def kv_spec(part, gi):
        col = head_col(part, gi)
        return pl.BlockSpec((seq, HEAD_DIM), lambda b, j, t: (b, col(b, j, t)))

    tile_buf = pltpu.VMEM((ATTN_C_TOKENS, HEAD_DIM), F32)
    return pl.pallas_call(
        functools.partial(_attn_c_kernel, seq=seq),
        grid=(bsz, C_HEADS_PER_GROUP, nt),
        in_specs=[q_spec(0), q_spec(1), q_spec(2),
                  kv_spec(1, 0), kv_spec(1, 1), kv_spec(1, 2),
                  kv_spec(2, 0), kv_spec(2, 1), kv_spec(2, 2),
                  pl.BlockSpec((C_GROUPS, None, len(_C_WINDOW_OFFSETS), ATTN_C_Q, ATTN_C_K),
                               lambda b, j, t: (0, j, 0, 0, 0))],
        out_specs=pl.BlockSpec((ATTN_C_TOKENS, HEAD_DIM), lambda b, j, t: (b * nt + t, j)),
        out_shape=jax.ShapeDtypeStruct((bsz * seq, C_OUT), BF16),
        scratch_shapes=[tile_buf] * 6,
        compiler_params=_params(3, 56),
        name="dilated_attn",
    )(*([p_c] * 9), bias)


def _head_ones():
    r = lax.broadcasted_iota(jnp.int32, (LANES, LANES), 0) // B_HEAD_DIM
    c = lax.broadcasted_iota(jnp.int32, (LANES, LANES), 1) // B_HEAD_DIM
    return (r == c).astype(F32)


def _head_sum(x, ones):
    parts = [jnp.dot(x[:, i:i + LANES], ones, precision=HIGHEST, preferred_element_type=F32)
             for i in range(0, x.shape[1], LANES)]
    return jnp.concatenate(parts, axis=1)


def _rwkv_prep_kernel(y_ref, yp_ref, yn_ref, mup_ref, mun_ref, w0_ref, w2_ref, a0_ref, a2_ref, g2_ref,
                      kk_ref, ka_ref, rk_ref,
                      r_o, v_o, kk_o, lw0_o, lw1_o, kd0_o, kd1_o, b0_o, b1_o, g_o, bonus_o,
                      *, blocks_per_seq):
    i = pl.program_id(0)
    ts = y_ref.shape[0]
    at_start = (i % blocks_per_seq) == 0
    at_end = (i % blocks_per_seq) == blocks_per_seq - 1
    rows = lax.broadcasted_iota(jnp.int32, (ts, 1), 0)

    def mixed(c0, c1):
        y = y_ref[:, c0:c1]
        before = jnp.where(at_start, 0.0, yp_ref[SUBLANES - 1:SUBLANES, c0:c1])
        after = jnp.where(at_end, 0.0, yn_ref[0:1, c0:c1])
        prev = jnp.where(rows == 0, before, pltpu.roll(y, 1, axis=0))
        nxt = jnp.where(rows == ts - 1, after, pltpu.roll(y, ts - 1, axis=0))
        return y + mup_ref[:, c0:c1] * (prev - y) + mun_ref[:, c0:c1] * (nxt - y)

    w = B_WIDTH
    r = mixed(0, w)
    k = mixed(w, 2 * w)
    v = mixed(2 * w, 3 * w)
    hw = mixed(3 * w, 3 * w + LORA_PAIR)
    ha = mixed(3 * w + LORA_PAIR, 3 * w + 2 * LORA_PAIR)
    hg = mixed(3 * w + 2 * LORA_PAIR, 3 * w + 2 * LORA_PAIR + GATE_LORA)
    ones = _head_ones()

    r_o[...] = r
    v_o[...] = v
    g_o[...] = _bdot(_sigmoid(hg), g2_ref[...])
    kk = k * kk_ref[...]
    kk = kk * lax.rsqrt(_head_sum(kk * kk, ones) + 1e-12)
    kk_o[...] = kk
    tanh_hw = jnp.tanh(hw)
    rrk = r * rk_ref[...]
    bonus = jnp.zeros_like(r)
    for d, (lw_o, kd_o, b_o) in enumerate(((lw0_o, kd0_o, b0_o), (lw1_o, kd1_o, b1_o))):
        z = -(w0_ref[d:d + 1, :] + _bdot(tanh_hw, w2_ref[d]))
        softplus = jnp.maximum(z, 0.0) + jnp.log(1.0 + jnp.exp(-jnp.abs(z)))
        lw_o[...] = -jnp.exp(-softplus - 0.5)
        a = _sigmoid(a0_ref[d:d + 1, :] + _bdot(ha, a2_ref[d]))
        kd = k * (1.0 + (a - 1.0) * ka_ref[...])
        kd_o[...] = kd
        b_o[...] = kk * a
        bonus = bonus + rrk * kd
    bonus_o[...] = _head_sum(bonus, ones) * v


def _rwkv_prep(p_b, mu_prev, mu_next, w0, w2, a0, a2, g2, k_k, k_a, r_k, seq):
    n_tok = p_b.shape[0]
    ts = min(RWKV_PREP_ROWS, seq)
    blocks_per_seq = seq // ts
    halo = ts // SUBLANES
    n_halo = n_tok // SUBLANES
    zeros = jnp.zeros_like(w2[0])
    w2p = jnp.stack([jnp.concatenate([w2[0], zeros], 0), jnp.concatenate([zeros, w2[1]], 0)]).astype(BF16)
    a2p = jnp.stack([jnp.concatenate([a2[0], zeros], 0), jnp.concatenate([zeros, a2[1]], 0)]).astype(BF16)
    full = lambda shape: pl.BlockSpec(shape, lambda i: (0,) * len(shape))
    row_blk = pl.BlockSpec((ts, B_WIDTH), lambda i: (i, 0))
    out = jax.ShapeDtypeStruct((n_tok, B_WIDTH), F32)
    return pl.pallas_call(
        functools.partial(_rwkv_prep_kernel, blocks_per_seq=blocks_per_seq),
        grid=(n_tok // ts,),
        in_specs=[
            pl.BlockSpec((ts, B_COLS), lambda i: (i, 0)),
            pl.BlockSpec((SUBLANES, B_COLS), lambda i: (jnp.maximum(i * halo - 1, 0), 0)),
            pl.BlockSpec((SUBLANES, B_COLS), lambda i: (jnp.minimum((i + 1) * halo, n_halo - 1), 0)),
            full((1, B_COLS)), full((1, B_COLS)),
            full((2, B_WIDTH)), full((2, LORA_PAIR, B_WIDTH)),
            full((2, B_WIDTH)), full((2, LORA_PAIR, B_WIDTH)),
            full((GATE_LORA, B_WIDTH)),
            full((1, B_WIDTH)), full((1, B_WIDTH)), full((1, B_WIDTH)),
        ],
        out_specs=[row_blk] * 11,
        out_shape=[out] * 11,
        compiler_params=_params(1, 56),
        name="rwkv_prep",
    )(p_b, p_b, p_b, mu_prev.reshape(1, B_COLS), mu_next.reshape(1, B_COLS), w0, w2p, a0, a2p,
      g2.astype(BF16), k_k.reshape(1, B_WIDTH), k_a.reshape(1, B_WIDTH), r_k.reshape(1, B_WIDTH))


def _split3(x):
    hi = x.astype(BF16)
    rest = x - hi.astype(F32)
    mid = rest.astype(BF16)
    return hi, mid, (rest - mid.astype(F32)).astype(BF16)


def _rwkv_scan_kernel(r0, v0, kk0, lw0, kd0, b0, r1, v1, kk1, lw1, kd1, b1, o0_ref, o1_ref, s_ref, *, chunk):
    @pl.when(pl.program_id(2) == 0)
    def _():
        s_ref[...] = jnp.zeros_like(s_ref)

    t, w = r0.shape
    heads = w // B_HEAD_DIM
    n_chunks = t // chunk
    row = lax.broadcasted_iota(jnp.int32, (t, t), 0)
    col = lax.broadcasted_iota(jnp.int32, (t, t), 1)
    same = (row // chunk) == (col // chunk)
    eye = (row == col).astype(F32)
    head_blk = (lax.broadcasted_iota(jnp.int32, (w, w), 0) // B_HEAD_DIM
                == lax.broadcasted_iota(jnp.int32, (w, w), 1) // B_HEAD_DIM)
    chunk_of_row = lax.broadcasted_iota(jnp.int32, (t, 1), 0) // chunk

    def by_chunk(x):
        return jnp.concatenate([jnp.where(chunk_of_row == c, x, 0.0) for c in range(n_chunks)], axis=1)

    def head(x, h):
        return x[:, h * B_HEAD_DIM:(h + 1) * B_HEAD_DIM]

    dirs = ((r0, v0, kk0, lw0, kd0, b0), (r1, v1, kk1, lw1, kd1, b1))
    out_refs = (o0_ref, o1_ref)
    strict, incl, pre = [], [], []
    for d, (r_ref, v_ref, kk_ref, lw_ref, kd_ref, b_ref) in enumerate(dirs):
        earlier = (col > row) if d == 1 else (col < row)
        strict.append(same & earlier)
        incl.append(same & (earlier | (col == row)))
        lw = lw_ref[...]
        sums = jnp.dot(jnp.concatenate([incl[d], same], axis=0).astype(BF16),
                       jnp.concatenate(_split3(lw), axis=1), preferred_element_type=F32)
        sums = sums[:, :w] + sums[:, w:2 * w] + sums[:, 2 * w:]
        cum, tot = sums[:t], sums[t:]
        grow, shrink, to_end = jnp.exp(cum), jnp.exp(-cum), jnp.exp(tot - cum)
        k, b = kd_ref[...], b_ref[...]
        pre.append(dict(v=v_ref[...], rt=r_ref[...] * grow, kb=kk_ref[...] * jnp.exp(cum - lw),
                        kt=k * shrink, bt=b * shrink, kt_end=k * to_end, bt_end=b * to_end,
                        decay_end=jnp.exp(tot)))

    units = [(d, h) for d in range(2) for h in range(heads)]
    part = lambda name, u: head(pre[u[0]][name], u[1])
    cross = {u: _bdot_nt(jnp.concatenate([part("kb", u), part("rt", u)], axis=0),
                         jnp.concatenate([part("kt", u), part("bt", u)], axis=0)) for u in units}
    a_k = {u: jnp.where(strict[u[0]], cross[u][:t, :t], 0.0) for u in units}
    a_b = {u: jnp.where(strict[u[0]], cross[u][:t, t:], 0.0) for u in units}
    a_rk = {u: jnp.where(incl[u[0]], cross[u][t:, :t], 0.0) for u in units}
    a_rb = {u: jnp.where(incl[u[0]], cross[u][t:, t:], 0.0) for u in units}
    inv = {u: eye - a_b[u] for u in units}
    power = {u: _bdot(a_b[u], a_b[u]) for u in units}
    akv = {u: _bdot(a_k[u], part("v", u)) for u in units}
    span = 2
    while span < chunk:
        if 2 * span < chunk:
            both = {u: _bdot(jnp.concatenate([inv[u], power[u]], axis=0), power[u]) for u in units}
            inv = {u: inv[u] + both[u][:t] for u in units}
            power = {u: both[u][t:] for u in units}
        else:
            inv = {u: inv[u] + _bdot(inv[u], power[u]) for u in units}
        span *= 2
    sol = {u: _bdot(inv[u], jnp.concatenate([part("kb", u), akv[u]], axis=1)) for u in units}
    free = {}
    for u in units:
        v = part("v", u)
        p, q = sol[u][:, :B_HEAD_DIM], sol[u][:, B_HEAD_DIM:]
        rhs = jnp.concatenate([jnp.concatenate([v, jnp.zeros_like(v)], axis=1),
                               jnp.concatenate([-q, -p], axis=1)], axis=0)
        free[u] = _bdot(jnp.concatenate([a_rk[u], a_rb[u]], axis=1), rhs)

    def all_heads(table, d, lo):
        return jnp.concatenate([table[(d, h)][:, lo:lo + B_HEAD_DIM] for h in range(heads)], axis=1)

    read, out_free, trans, inject = [], [], [], []
    for d in range(2):
        p_all, q_all = all_heads(sol, d, 0), all_heads(sol, d, B_HEAD_DIM)
        out_free.append(all_heads(free, d, 0))
        read.append(pre[d]["rt"] + all_heads(free, d, B_HEAD_DIM))
        bt_end_c = by_chunk(pre[d]["bt_end"])
        trans.append(_bdot_tn(p_all, bt_end_c))
        inject.append(_bdot_tn(jnp.concatenate([pre[d]["v"], q_all], axis=0),
                               jnp.concatenate([by_chunk(pre[d]["kt_end"]), -bt_end_c], axis=0)))

    state = [s_ref[0], s_ref[1]]
    for i in range(n_chunks):
        for d in range(2):
            c = n_chunks - 1 - i if d == 1 else i
            rows = slice(c * chunk, (c + 1) * chunk)
            cols = slice(c * w, (c + 1) * w)
            blocks = jnp.where(head_blk, jnp.concatenate([state[d]] * heads, axis=0), 0.0)
            out_refs[d][rows, :] = _bdot_nt(read[d][rows], blocks) + out_free[d][rows]
            inj = jnp.where(head_blk, inject[d][:, cols], 0.0)
            inj = sum(inj[h * B_HEAD_DIM:(h + 1) * B_HEAD_DIM] for h in range(heads))
            state[d] = (state[d] * pre[d]["decay_end"][c * chunk:c * chunk + 1, :]
                        - _bdot(state[d], jnp.where(head_blk, trans[d][:, cols], 0.0)) + inj)
    s_ref[0] = state[0]
    s_ref[1] = state[1]


def _rwkv_scan(r, v, kk, lw0, lw1, kd0, kd1, b0, b1, bsz, seq):
    tile = min(RWKV_SCAN_ROWS, seq)
    chunk = min(RWKV_CHUNK, tile)
    nb = seq // tile
    fwd = pl.BlockSpec((tile, RWKV_SCAN_LANES), lambda b, hg, c: (b * nb + c, hg))
    bwd = pl.BlockSpec((tile, RWKV_SCAN_LANES), lambda b, hg, c: (b * nb + nb - 1 - c, hg))
    out = jax.ShapeDtypeStruct(r.shape, F32)
    return pl.pallas_call(
        functools.partial(_rwkv_scan_kernel, chunk=chunk),
        grid=(bsz, B_WIDTH // RWKV_SCAN_LANES, nb),
        in_specs=[fwd] * 6 + [bwd] * 6,
        out_specs=[fwd, bwd],
        out_shape=[out, out],
        scratch_shapes=[pltpu.VMEM((2, B_HEAD_DIM, RWKV_SCAN_LANES), F32)],
        compiler_params=_params(3, 48),
        name="rwkv_scan",
    )(r, v, kk, lw0, kd0, b0, r, v, kk, lw1, kd1, b1)


def _rwkv_post_kernel(o0_ref, o1_ref, bonus_ref, g_ref, gn_g_ref, gn_b_ref, y_ref):
    ones = _head_ones()
    o = o0_ref[...] + o1_ref[...]
    mu = _head_sum(o, ones) * (1.0 / B_HEAD_DIM)
    oc = o - mu
    var = _head_sum(oc * oc, ones) * (1.0 / B_HEAD_DIM)
    normed = oc * lax.rsqrt(var + GN_EPS) * gn_g_ref[...] + gn_b_ref[...]
    y_ref[...] = ((normed + bonus_ref[...]) * g_ref[...]).astype(y_ref.dtype)


def _rwkv_post(o0, o1, bonus, g, gn_g, gn_b):
    n_tok = o0.shape[0]
    ts = min(RWKV_PREP_ROWS, n_tok)
    row_blk = pl.BlockSpec((ts, B_WIDTH), lambda i: (i, 0))
    vec = pl.BlockSpec((1, B_WIDTH), lambda i: (0, 0))
    return pl.pallas_call(
        _rwkv_post_kernel,
        grid=(n_tok // ts,),
        in_specs=[row_blk] * 4 + [vec, vec],
        out_specs=row_blk,
        out_shape=jax.ShapeDtypeStruct((n_tok, B_WIDTH), BF16),
        compiler_params=_params(1, 32),
        name="rwkv_post",
    )(o0, o1, bonus, g, gn_g.reshape(1, B_WIDTH), gn_b.reshape(1, B_WIDTH))


def _rwkv(p_b, mu_prev, mu_next, w0, w2, a0, a2, g2, k_k, k_a, r_k, gn_g, gn_b, bsz, seq):
    r, v, kk, lw0, lw1, kd0, kd1, b0, b1, g, bonus = _rwkv_prep(
        p_b, mu_prev, mu_next, w0, w2, a0, a2, g2, k_k, k_a, r_k, seq)
    o0, o1 = _rwkv_scan(r, v, kk, lw0, lw1, kd0, kd1, b0, b1, bsz, seq)
    return _rwkv_post(o0, o1, bonus, g, gn_g, gn_b)


def _merge_kernel(ya_ref, yb_ref, yc_ref, ga_ref, gb_ref, gc_ref, wa_ref, wb_ref, wc_ref, o_ref):
    def branch(y_ref, w_ref, gate_ref):
        return _sigmoid(gate_ref[...]) * jnp.dot(y_ref[...], w_ref[...], preferred_element_type=F32)

    merged = branch(ya_ref, wa_ref, ga_ref) + branch(yb_ref, wb_ref, gb_ref) + branch(yc_ref, wc_ref, gc_ref)
    o_ref[...] = merged.astype(o_ref.dtype)


def _merge(y_a, y_b, y_c, p_g, wb_a, wb_b, wb_c):
    n_tok = y_a.shape[0]
    tm = min(MERGE_ROWS, n_tok)
    tn = MERGE_COLS
    nj = D_MODEL // tn

    def rows(width):
        return pl.BlockSpec((tm, width), lambda i, j: (i, 0))

    def gate(branch):
        return pl.BlockSpec((tm, tn), lambda i, j: (i, branch * nj + j))

    def weight(depth):
        return pl.BlockSpec((depth, tn), lambda i, j: (0, j))

    return pl.pallas_call(
        _merge_kernel,
        grid=(n_tok // tm, nj),
        in_specs=[rows(A_Q), rows(B_WIDTH), rows(C_OUT), gate(0), gate(1), gate(2),
                  weight(A_Q), weight(B_WIDTH), weight(C_OUT)],
        out_specs=pl.BlockSpec((tm, tn), lambda i, j: (i, j)),
        out_shape=jax.ShapeDtypeStruct((n_tok, D_MODEL), BF16),
        compiler_params=_params(2, 32),
        name="gated_merge",
    )(y_a, y_b, y_c, p_g, p_g, p_g, wb_a, wb_b, wb_c)


def _first_lane_of(mask, lane):
    return jnp.min(jnp.where(mask, lane, float(ROUTER_COLS)), axis=-1, keepdims=True)


def _route_rows(logits):
    lane = lax.broadcasted_iota(jnp.int32, logits.shape, 1).astype(F32)
    grp_logit = jnp.where(lane < N_GROUPS, logits, -jnp.inf)
    e = jnp.exp(grp_logit - jnp.max(grp_logit, axis=-1, keepdims=True))
    prob = e / jnp.sum(e, axis=-1, keepdims=True)
    grp_w = jnp.max(prob, axis=-1, keepdims=True)
    grp = _first_lane_of(prob == grp_w, lane)
    first = N_GROUPS + grp * EXPERTS_PER_GROUP
    cand = jnp.where((lane >= first) & (lane < first + EXPERTS_PER_GROUP), logits, -jnp.inf)
    top1 = jnp.max(cand, axis=-1, keepdims=True)
    lane1 = _first_lane_of(cand == top1, lane)
    cand = jnp.where(lane == lane1, -jnp.inf, cand)
    top2 = jnp.max(cand, axis=-1, keepdims=True)
    lane2 = _first_lane_of(cand == top2, lane)
    e2 = jnp.exp(top2 - top1)
    gate1 = grp_w / (1.0 + e2)
    gate2 = grp_w * e2 / (1.0 + e2)
    return jnp.where(lane == 0, gate1,
                     jnp.where(lane == 1, gate2,
                               jnp.where(lane == 2, lane1 - N_GROUPS,
                                         jnp.where(lane == 3, lane2 - N_GROUPS, 0.0))))


def _out_proj_kernel(m_ref, x_ref, w_ref, g_ref, b_ref, wr_ref, br_ref, x_o, xb_o, route_o, *, alpha):
    z = alpha * x_ref[...] + jnp.dot(m_ref[...], w_ref[...], preferred_element_type=F32)
    x1 = _layer_norm(z, g_ref[...], b_ref[...])
    x_o[...] = x1
    xb_o[...] = x1.astype(BF16)
    logits = jnp.dot(x1, wr_ref[...], precision=HIGHEST, preferred_element_type=F32) + br_ref[...]
    route_o[...] = _route_rows(logits)


def _out_proj(merged, x, w_out, ln_g, ln_b, w_router, b_router, alpha):
    n_tok = x.shape[0]
    tm = min(OUT_ROWS, n_tok)
    rows = lambda width: pl.BlockSpec((tm, width), lambda i: (i, 0))
    full = lambda shape: pl.BlockSpec(shape, lambda i: (0, 0))
    return pl.pallas_call(
        functools.partial(_out_proj_kernel, alpha=alpha),
        grid=(n_tok // tm,),
        in_specs=[rows(D_MODEL), rows(D_MODEL), full((D_MODEL, D_MODEL)), full((1, D_MODEL)),
                  full((1, D_MODEL)), full((D_MODEL, ROUTER_COLS)), full((1, ROUTER_COLS))],
        out_specs=[rows(D_MODEL), rows(D_MODEL), rows(ROUTER_COLS)],
        out_shape=[jax.ShapeDtypeStruct((n_tok, D_MODEL), F32),
                   jax.ShapeDtypeStruct((n_tok, D_MODEL), BF16),
                   jax.ShapeDtypeStruct((n_tok, ROUTER_COLS), F32)],
        compiler_params=_params(1, 48),
        name="out_proj_ln_router",
    )(merged, x, w_out, ln_g.reshape(1, D_MODEL), ln_b.reshape(1, D_MODEL), w_router, b_router)


def _dispatch(expert):
    n_tok = expert.shape[0]
    flat_e = expert.reshape(-1)
    n_asg = flat_e.shape[0]
    order = jnp.argsort(flat_e)
    rank = jnp.argsort(order)
    counts = jnp.bincount(flat_e, length=N_EXPERTS)
    padded = (counts + MOE_ROWS - 1) // MOE_ROWS * MOE_ROWS
    pad_end = jnp.cumsum(padded)
    pad_start = pad_end - padded
    start = jnp.cumsum(counts) - counts
    dest = (pad_start - start)[flat_e] + rank
    n_rows = (n_asg + MOE_ROWS - 1) // MOE_ROWS * MOE_ROWS + N_EXPERTS * MOE_ROWS
    n_blk = n_rows // MOE_ROWS
    blk_expert = jnp.minimum(jnp.searchsorted(pad_end, jnp.arange(n_blk) * MOE_ROWS, side='right'),
                             N_EXPERTS - 1).astype(jnp.int32)
    row_expert = jnp.repeat(blk_expert, MOE_ROWS)
    offset = jnp.arange(n_rows) - pad_start[row_expert]
    source = order[jnp.clip(start[row_expert] + offset, 0, n_asg - 1)] // TOP_K
    row_tok = jnp.where(offset < counts[row_expert], source, 0).astype(jnp.int32)
    return dest.reshape(n_tok, TOP_K).astype(jnp.int32), row_tok, blk_expert


def _expert_kernel(blk_expert_ref, x_ref, wg_ref, wu_ref, wd_ref, o_ref):
    x = x_ref[...]
    gate = jnp.dot(x, wg_ref[...].astype(BF16), preferred_element_type=F32)
    up = jnp.dot(x, wu_ref[...].astype(BF16), preferred_element_type=F32)
    hid = gate * _sigmoid(gate) * up
    o_ref[...] = jnp.dot(hid.astype(BF16), wd_ref[...].astype(BF16), preferred_element_type=F32)


def _experts(x_rows, blk_expert, w_gate, w_up, w_down, layer):
    n_rows = x_rows.shape[0]
    grid_spec = pltpu.PrefetchScalarGridSpec(
        num_scalar_prefetch=1,
        grid=(n_rows // MOE_ROWS,),
        in_specs=[pl.BlockSpec((MOE_ROWS, D_MODEL), lambda i, e: (i, 0)),
                  pl.BlockSpec((None, None, D_MODEL, EXPERT_FF), lambda i, e: (layer, e[i], 0, 0)),
                  pl.BlockSpec((None, None, D_MODEL, EXPERT_FF), lambda i, e: (layer, e[i], 0, 0)),
                  pl.BlockSpec((None, None, EXPERT_FF, D_MODEL), lambda i, e: (layer, e[i], 0, 0))],
        out_specs=pl.BlockSpec((MOE_ROWS, D_MODEL), lambda i, e: (i, 0)),
    )
    return pl.pallas_call(
        _expert_kernel,
        grid_spec=grid_spec,
        out_shape=jax.ShapeDtypeStruct((n_rows, D_MODEL), F32),
        compiler_params=_params(1, 56),
        name="expert_ffn",
    )(blk_expert, x_rows, w_gate, w_up, w_down)


def _combine_kernel(x_ref, ya_ref, yb_ref, gate_ref, g_ref, b_ref, x_o, xb_o, *, alpha):
    gate = gate_ref[...]
    moe = gate[:, 0:1] * ya_ref[...] + gate[:, 1:2] * yb_ref[...]
    x2 = _layer_norm(alpha * x_ref[...] + moe, g_ref[...], b_ref[...])
    x_o[...] = x2
    xb_o[...] = x2.astype(BF16)


def _combine(x1, y_first, y_second, route, ln_g, ln_b, alpha):
    n_tok = x1.shape[0]
    tm = min(OUT_ROWS, n_tok)
    rows = lambda width: pl.BlockSpec((tm, width), lambda i: (i, 0))
    vec = pl.BlockSpec((1, D_MODEL), lambda i: (0, 0))
    return pl.pallas_call(
        functools.partial(_combine_kernel, alpha=alpha),
        grid=(n_tok // tm,),
        in_specs=[rows(D_MODEL), rows(D_MODEL), rows(D_MODEL), rows(LANES), vec, vec],
        out_specs=[rows(D_MODEL), rows(D_MODEL)],
        out_shape=[jax.ShapeDtypeStruct((n_tok, D_MODEL), F32),
                   jax.ShapeDtypeStruct((n_tok, D_MODEL), BF16)],
        compiler_params=_params(1, 32),
        name="moe_combine_ln",
    )(x1, y_first, y_second, route, ln_g.reshape(1, D_MODEL), ln_b.reshape(1, D_MODEL))


def _moe(x1, x1_bf16, route, w_gate, w_up, w_down, layer, ln_g, ln_b, alpha):
    dest, row_tok, blk_expert = _dispatch(route[:, TOP_K:2 * TOP_K].astype(jnp.int32))
    y_rows = _experts(x1_bf16[row_tok], blk_expert, w_gate, w_up, w_down, layer)
    return _combine(x1, y_rows[dest[:, 0]], y_rows[dest[:, 1]], route, ln_g, ln_b, alpha)


def kernel(x, w_in, w_branch, w_out, mu_prev, mu_next, rwkv_w0, rwkv_w2, rwkv_a0, rwkv_a2, rwkv_g2, rwkv_k_k, rwkv_k_a, rwkv_r_k, rwkv_gn_g, rwkv_gn_b, q_norm, k_norm, rel_bias, ln1_g, ln1_b, router_group_w, router_group_b, router_expert_w, router_expert_b, w_gate, w_up, w_down, ln2_g, ln2_b):
    bsz, seq, _ = x.shape
    depth = w_in.shape[0]
    alpha = (2.0 * depth) ** 0.25
    xf = x.reshape(bsz * seq, D_MODEL)
    xb = xf.astype(BF16)
    col_b = A_COLS
    col_c = A_COLS + B_COLS
    col_g = A_COLS + B_COLS + C_COLS
    for l in range(depth):
        w_l = w_in[l]
        p_a = _matmul(xb, w_l[:, :col_b].astype(BF16), 768)
        p_b = _matmul(xb, w_l[:, col_b:col_c].astype(BF16), 1152)
        p_c = _matmul(xb, w_l[:, col_c:col_g].astype(BF16), 1152)
        p_g = _matmul(xb, w_l[:, col_g:].astype(BF16), 1024)
        y_a = _attention_a(p_a, q_norm[l], k_norm[l], bsz, seq)
        y_b = _rwkv(p_b, mu_prev[l], mu_next[l], rwkv_w0[l], rwkv_w2[l], rwkv_a0[l], rwkv_a2[l],
                    rwkv_g2[l], rwkv_k_k[l], rwkv_k_a[l], rwkv_r_k[l], rwkv_gn_g[l], rwkv_gn_b[l],
                    bsz, seq)
        y_c = _attention_c(p_c, rel_bias, bsz, seq)
        wb = w_branch[l].astype(BF16)
        merged = _merge(y_a, y_b, y_c, p_g, wb[:A_Q], wb[A_Q:A_Q + B_WIDTH], wb[A_Q + B_WIDTH:])
        w_router = jnp.pad(jnp.concatenate([router_group_w[l], router_expert_w[l]], axis=1),
                           ((0, 0), (0, ROUTER_COLS - N_GROUPS - N_EXPERTS)))
        b_router = jnp.pad(jnp.concatenate([router_group_b[l], router_expert_b[l]]),
                           (0, ROUTER_COLS - N_GROUPS - N_EXPERTS)).reshape(1, ROUTER_COLS)
        x1, x1b, route = _out_proj(merged, xf, w_out[l].astype(BF16), ln1_g[l], ln1_b[l],
                                   w_router, b_router, alpha)
        xf, xb = _moe(x1, x1b, route, w_gate, w_up, w_down, l, ln2_g[l], ln2_b[l], alpha)
    return xf.reshape(bsz, seq, D_MODEL)
```

```python
import functools

import numpy as np
import jax
import jax.numpy as jnp
from jax import lax
from jax.experimental import pallas as pl
from jax.experimental.pallas import tpu as pltpu

F32 = jnp.float32
BF16 = jnp.bfloat16
HIGHEST = lax.Precision.HIGHEST

D_MODEL = 2048
DEPTH = 4
GRID_W = 64
NEG_INF = -1e30
LN_EPS = 1e-5

A_HEADS = 8
A_KV_HEADS = 2
A_GROUP = A_HEADS // A_KV_HEADS
HEAD_DIM = 128
ROPE_THETA = 10000.0
QK_EPS = 1e-6

B_HEAD_DIM = 64
B_WIDTH = 1024
B_HEADS = B_WIDTH // B_HEAD_DIM
LORA_PAIR = 128
GATE_LORA = 128
GN_EPS = 64e-5

C_PATTERNS = ((128, 1), (512, 4), (2048, 16))
C_GROUPS = 3
C_HEADS_PER_GROUP = 4
C_HEADS = C_GROUPS * C_HEADS_PER_GROUP
C_HALF = 64
REL_BUCKETS = 32
REL_MAX_DISTANCE = 1024

N_GROUPS = 8
EXPERTS_PER_GROUP = 8
N_EXPERTS = N_GROUPS * EXPERTS_PER_GROUP
TOP_K = 2
EXPERT_FF = 384

A_Q = A_HEADS * HEAD_DIM
A_KV = A_KV_HEADS * HEAD_DIM
A_COLS = A_Q + 2 * A_KV
B_COLS = 3 * B_WIDTH + 2 * LORA_PAIR + GATE_LORA
C_QKV = C_HEADS * HEAD_DIM
C_COLS = 3 * C_QKV
N_BRANCHES = 3
GATE_COLS = N_BRANCHES * D_MODEL
C_OUT = C_HEADS_PER_GROUP * HEAD_DIM

LANES = 128
SUBLANES = 8
MIB = 1024 * 1024

MM_ROWS = 1024
ATTN_A_Q_ROWS = 256
ATTN_A_KV_CHUNK = 1024
ATTN_C_Q = 128
ATTN_C_K = 2 * ATTN_C_Q
ATTN_C_TOKENS = ATTN_C_Q * C_PATTERNS[-1][1]
RWKV_PREP_ROWS = 256
RWKV_CHUNK = 64
RWKV_SCAN_ROWS = 256
RWKV_SCAN_LANES = 256
MERGE_ROWS = 512
MERGE_COLS = 512
OUT_ROWS = 256
MOE_ROWS = 256
ROUTER_COLS = 128


def _params(n_grid, vmem_mib):
    return pltpu.CompilerParams(dimension_semantics=("arbitrary",) * n_grid,
                                vmem_limit_bytes=vmem_mib * MIB)


def _bdot(a, b):
    return jnp.dot(a.astype(BF16), b.astype(BF16), preferred_element_type=F32)


def _bdot_nt(a, b):
    return lax.dot_general(a.astype(BF16), b.astype(BF16), (((1,), (1,)), ((), ())),
                           preferred_element_type=F32)


def _bdot_tn(a, b):
    return lax.dot_general(a.astype(BF16), b.astype(BF16), (((0,), (0,)), ((), ())),
                           preferred_element_type=F32)


def _sigmoid(x):
    return 1.0 / (1.0 + jnp.exp(-x))


def _layer_norm(z, g, b):
    mu = jnp.mean(z, axis=-1, keepdims=True)
    zc = z - mu
    var = jnp.mean(zc * zc, axis=-1, keepdims=True)
    return zc * lax.rsqrt(var + LN_EPS) * g + b


def _mm_kernel(x_ref, w_ref, o_ref):
    o_ref[...] = jnp.dot(x_ref[...], w_ref[...], preferred_element_type=F32).astype(o_ref.dtype)


def _matmul(x, w, tn, out_dtype=F32):
    m, k = x.shape
    n = w.shape[1]
    tm = min(MM_ROWS, m)
    return pl.pallas_call(
        _mm_kernel,
        grid=(m // tm, n // tn),
        in_specs=[pl.BlockSpec((tm, k), lambda i, j: (i, 0)),
                  pl.BlockSpec((k, tn), lambda i, j: (0, j))],
        out_specs=pl.BlockSpec((tm, tn), lambda i, j: (i, j)),
        out_shape=jax.ShapeDtypeStruct((m, n), out_dtype),
        compiler_params=_params(2, 48),
        name="dense_proj",
    )(x, w)


def _rope_tables(seq):
    n = HEAD_DIM // 2
    inv = ROPE_THETA ** (-jnp.arange(0, n, 2, dtype=F32) / n)
    pos = jnp.arange(seq)
    ang_r = (pos // GRID_W).astype(F32)[:, None] * inv[None, :]
    ang_c = (pos % GRID_W).astype(F32)[:, None] * inv[None, :]
    cos = jnp.concatenate([jnp.cos(ang_r)] * 2 + [jnp.cos(ang_c)] * 2, axis=-1)
    sin = jnp.concatenate([-jnp.sin(ang_r), jnp.sin(ang_r), -jnp.sin(ang_c), jnp.sin(ang_c)], axis=-1)
    return cos, sin


def _norm_rope(t, gain, cos, sin):
    t = t * lax.rsqrt(jnp.mean(t * t, axis=-1, keepdims=True) + QK_EPS) * gain
    lane = lax.broadcasted_iota(jnp.int32, t.shape, 1)
    quarter = HEAD_DIM // 4
    partner = jnp.where(lane % (2 * quarter) < quarter,
                        pltpu.roll(t, HEAD_DIM - quarter, axis=1), pltpu.roll(t, quarter, axis=1))
    return t * cos + partner * sin


def _attn_a_kernel(q_ref, k_ref, v_ref, cq_ref, sq_ref, ck_ref, sk_ref, qg_ref, kg_ref, o_ref,
                   kp_ref, vp_ref):
    @pl.when((pl.program_id(2) == 0) & (pl.program_id(3) == 0))
    def _():
        kp_ref[...] = _norm_rope(k_ref[...], kg_ref[...], ck_ref[...], sk_ref[...]).astype(BF16)
        vp_ref[:, :HEAD_DIM] = v_ref[...].astype(BF16)
        vp_ref[:, HEAD_DIM:] = jnp.ones((v_ref.shape[0], HEAD_DIM), BF16)

    q = (_norm_rope(q_ref[...], qg_ref[...], cq_ref[...], sq_ref[...]) * HEAD_DIM ** -0.5).astype(BF16)
    seq = kp_ref.shape[0]
    tk = min(ATTN_A_KV_CHUNK, seq)

    def scores(c):
        return lax.dot_general(q, kp_ref[c * tk:(c + 1) * tk, :], (((1,), (1,)), ((), ())),
                               preferred_element_type=F32)

    m = acc = None
    s_next = scores(0)
    for c in range(seq // tk):
        s = s_next
        if (c + 1) * tk < seq:
            s_next = scores(c + 1)
        m_c = jnp.max(s, axis=-1, keepdims=True)
        m_new = m_c if m is None else jnp.maximum(m, m_c)
        p = jnp.exp(s - m_new)
        pv = jnp.dot(p.astype(BF16), vp_ref[c * tk:(c + 1) * tk, :], preferred_element_type=F32)
        acc = pv if m is None else acc * jnp.exp(m - m_new) + pv
        m = m_new
    o_ref[...] = (acc[:, :HEAD_DIM] / acc[:, HEAD_DIM:]).astype(o_ref.dtype)


def _attention_a(p_a, q_gain, k_gain, bsz, seq):
    tq = min(ATTN_A_Q_ROWS, seq)
    nq = seq // tq
    cos, sin = _rope_tables(seq)
    k_col = A_HEADS
    v_col = A_HEADS + A_KV_HEADS
    q_tab = pl.BlockSpec((tq, HEAD_DIM), lambda b, kv, g, i: (i, 0))
    k_tab = pl.BlockSpec((seq, HEAD_DIM), lambda b, kv, g, i: (0, 0))
    gain = pl.BlockSpec((1, HEAD_DIM), lambda b, kv, g, i: (0, 0))
    return pl.pallas_call(
        _attn_a_kernel,
        grid=(bsz, A_KV_HEADS, A_GROUP, nq),
        in_specs=[
            pl.BlockSpec((tq, HEAD_DIM), lambda b, kv, g, i: (b * nq + i, kv * A_GROUP + g)),
            pl.BlockSpec((seq, HEAD_DIM), lambda b, kv, g, i: (b, k_col + kv)),
            pl.BlockSpec((seq, HEAD_DIM), lambda b, kv, g, i: (b, v_col + kv)),
            q_tab, q_tab, k_tab, k_tab, gain, gain,
        ],
        out_specs=pl.BlockSpec((tq, HEAD_DIM), lambda b, kv, g, i: (b * nq + i, kv * A_GROUP + g)),
        out_shape=jax.ShapeDtypeStruct((bsz * seq, A_Q), BF16),
        scratch_shapes=[pltpu.VMEM((seq, HEAD_DIM), BF16), pltpu.VMEM((seq, 2 * HEAD_DIM), BF16)],
        compiler_params=_params(4, 48),
        name="axial_gqa",
    )(p_a, p_a, p_a, cos, sin, cos, sin, q_gain.reshape(1, HEAD_DIM), k_gain.reshape(1, HEAD_DIM))


def _t5_bucket(rel):
    nb = REL_BUCKETS // 2
    max_exact = nb // 2
    n = np.abs(rel)
    large = max_exact + (np.log(np.maximum(n, 1) / max_exact) / np.log(REL_MAX_DISTANCE / max_exact)
                         * (nb - max_exact)).astype(np.int32)
    large = np.minimum(large, nb - 1)
    return (rel > 0).astype(np.int32) * nb + np.where(n < max_exact, n, large)


_C_WINDOW_OFFSETS = (0, -C_HALF, ATTN_C_Q - ATTN_C_K)


def _dilated_bias(rel_bias):
    per_group = []
    for gi, (_, dilation) in enumerate(C_PATTERNS):
        hs = slice(gi * C_HEADS_PER_GROUP, (gi + 1) * C_HEADS_PER_GROUP)
        variants = []
        for off in _C_WINDOW_OFFSETS:
            delta = off + np.arange(ATTN_C_K)[None, :] - np.arange(ATTN_C_Q)[:, None]
            vals = jnp.moveaxis(rel_bias[_t5_bucket(delta * dilation)][..., hs], -1, 0).astype(F32)
            variants.append(jnp.where((np.abs(delta) <= C_HALF)[None], vals, NEG_INF))
        per_group.append(jnp.stack(variants, axis=1))
    return jnp.stack(per_group, axis=0)


def _strided(start, size, stride):
    return pl.ds(start, size) if stride == 1 else pl.ds(start, size, stride=stride)


def _attn_c_kernel(q0, q1, q2, k0, k1, k2, v0, v1, v2, bias_ref, o_ref,
                   og0, og1, og2, ls0, ls1, ls2, *, seq):
    tile = pl.program_id(2)
    scale = HEAD_DIM ** -0.5
    groups = ((q0, k0, v0, og0, ls0), (q1, k1, v1, og1, ls1), (q2, k2, v2, og2, ls2))
    for gi, (q_ref, k_ref, v_ref, og_ref, ls_ref) in enumerate(groups):
        dil = C_PATTERNS[gi][1]
        n = seq // dil
        per_tile = ATTN_C_TOKENS // dil
        for res in range(dil):
            for blk in range(per_tile // ATTN_C_Q):
                m_local = blk * ATTN_C_Q
                m0 = tile * per_tile + m_local
                k_start = jnp.clip(m0 + _C_WINDOW_OFFSETS[1], 0, n - ATTN_C_K)
                variant = jnp.where(m0 == 0, 0, jnp.where(m0 == n - ATTN_C_Q, 2, 1))
                q_rows = _strided(m_local * dil + res, ATTN_C_Q, dil)
                k_rows = _strided(k_start * dil + res, ATTN_C_K, dil)
                s = _bdot_nt(q_ref[q_rows, :], k_ref[k_rows, :]) * scale + bias_ref[gi, variant]
                m = jnp.max(s, axis=-1, keepdims=True)
                p = jnp.exp(s - m)
                l = jnp.sum(p, axis=-1, keepdims=True)
                o = _bdot(p, v_ref[k_rows, :])
                og_ref[q_rows, :] = o / l
                ls_ref[q_rows, :] = jnp.broadcast_to(m + jnp.log(l), (ATTN_C_Q, HEAD_DIM))
    top = jnp.maximum(jnp.maximum(ls0[...], ls1[...]), ls2[...])
    e0, e1, e2 = jnp.exp(ls0[...] - top), jnp.exp(ls1[...] - top), jnp.exp(ls2[...] - top)
    o_ref[...] = ((e0 * og0[...] + e1 * og1[...] + e2 * og2[...]) / (e0 + e1 + e2)).astype(o_ref.dtype)


def _attention_c(p_c, rel_bias, bsz, seq):
    assert seq % ATTN_C_TOKENS == 0 and seq // C_PATTERNS[-1][1] >= ATTN_C_K
    nt = seq // ATTN_C_TOKENS
    bias = _dilated_bias(rel_bias)

    def head_col(part, gi):
        return lambda b, j, t: part * C_HEADS + gi * C_HEADS_PER_GROUP + j

    def q_spec(gi):
        col = head_col(0, gi)
        return pl.BlockSpec((ATTN_C_TOKENS, HEAD_DIM), lambda b, j, t: (b * nt + t, col(b, j, t)))

    def kv_spec(part, gi):
        col = head_col(part, gi)
        return pl.BlockSpec((seq, HEAD_DIM), lambda b, j, t: (b, col(b, j, t)))

    tile_buf = pltpu.VMEM((ATTN_C_TOKENS, HEAD_DIM), F32)
    return pl.pallas_call(
        functools.partial(_attn_c_kernel, seq=seq),
        grid=(bsz, C_HEADS_PER_GROUP, nt),
        in_specs=[q_spec(0), q_spec(1), q_spec(2),
                  kv_spec(1, 0), kv_spec(1, 1), kv_spec(1, 2),
                  kv_spec(2, 0), kv_spec(2, 1), kv_spec(2, 2),
                  pl.BlockSpec((C_GROUPS, None, len(_C_WINDOW_OFFSETS), ATTN_C_Q, ATTN_C_K),
                               lambda b, j, t: (0, j, 0, 0, 0))],
        out_specs=pl.BlockSpec((ATTN_C_TOKENS, HEAD_DIM), lambda b, j, t: (b * nt + t, j)),
        out_shape=jax.ShapeDtypeStruct((bsz * seq, C_OUT), BF16),
        scratch_shapes=[tile_buf] * 6,
        compiler_params=_params(3, 56),
        name="dilated_attn",
    )(*([p_c] * 9), bias)


def _head_ones():
    r = lax.broadcasted_iota(jnp.int32, (LANES, LANES), 0) // B_HEAD_DIM
    c = lax.broadcasted_iota(jnp.int32, (LANES, LANES), 1) // B_HEAD_DIM
    return (r == c).astype(F32)


def _head_sum(x, ones):
    parts = [jnp.dot(x[:, i:i + LANES], ones, precision=HIGHEST, preferred_element_type=F32)
             for i in range(0, x.shape[1], LANES)]
    return jnp.concatenate(parts, axis=1)


def _rwkv_prep_kernel(y_ref, yp_ref, yn_ref, mup_ref, mun_ref, w0_ref, w2_ref, a0_ref, a2_ref, g2_ref,
                      kk_ref, ka_ref, rk_ref,
                      r_o, v_o, kk_o, lw0_o, lw1_o, kd0_o, kd1_o, b0_o, b1_o, g_o, bonus_o,
                      *, blocks_per_seq):
    i = pl.program_id(0)
    ts = y_ref.shape[0]
    at_start = (i % blocks_per_seq) == 0
    at_end = (i % blocks_per_seq) == blocks_per_seq - 1
    rows = lax.broadcasted_iota(jnp.int32, (ts, 1), 0)

    def mixed(c0, c1):
        y = y_ref[:, c0:c1]
        before = jnp.where(at_start, 0.0, yp_ref[SUBLANES - 1:SUBLANES, c0:c1])
        after = jnp.where(at_end, 0.0, yn_ref[0:1, c0:c1])
        prev = jnp.where(rows == 0, before, pltpu.roll(y, 1, axis=0))
        nxt = jnp.where(rows == ts - 1, after, pltpu.roll(y, ts - 1, axis=0))
        return y + mup_ref[:, c0:c1] * (prev - y) + mun_ref[:, c0:c1] * (nxt - y)

    w = B_WIDTH
    r = mixed(0, w)
    k = mixed(w, 2 * w)
    v = mixed(2 * w, 3 * w)
    hw = mixed(3 * w, 3 * w + LORA_PAIR)
    ha = mixed(3 * w + LORA_PAIR, 3 * w + 2 * LORA_PAIR)
    hg = mixed(3 * w + 2 * LORA_PAIR, 3 * w + 2 * LORA_PAIR + GATE_LORA)
    ones = _head_ones()

    r_o[...] = r
    v_o[...] = v
    g_o[...] = _bdot(_sigmoid(hg), g2_ref[...])
    kk = k * kk_ref[...]
    kk = kk * lax.rsqrt(_head_sum(kk * kk, ones) + 1e-12)
    kk_o[...] = kk
    tanh_hw = jnp.tanh(hw)
    rrk = r * rk_ref[...]
    bonus = jnp.zeros_like(r)
    for d, (lw_o, kd_o, b_o) in enumerate(((lw0_o, kd0_o, b0_o), (lw1_o, kd1_o, b1_o))):
        z = -(w0_ref[d:d + 1, :] + _bdot(tanh_hw, w2_ref[d]))
        softplus = jnp.maximum(z, 0.0) + jnp.log(1.0 + jnp.exp(-jnp.abs(z)))
        lw_o[...] = -jnp.exp(-softplus - 0.5)
        a = _sigmoid(a0_ref[d:d + 1, :] + _bdot(ha, a2_ref[d]))
        kd = k * (1.0 + (a - 1.0) * ka_ref[...])
        kd_o[...] = kd
        b_o[...] = kk * a
        bonus = bonus + rrk * kd
    bonus_o[...] = _head_sum(bonus, ones) * v


def _rwkv_prep(p_b, mu_prev, mu_next, w0, w2, a0, a2, g2, k_k, k_a, r_k, seq):
    n_tok = p_b.shape[0]
    ts = min(RWKV_PREP_ROWS, seq)
    blocks_per_seq = seq // ts
    halo = ts // SUBLANES
    n_halo = n_tok // SUBLANES
    zeros = jnp.zeros_like(w2[0])
    w2p = jnp.stack([jnp.concatenate([w2[0], zeros], 0), jnp.concatenate([zeros, w2[1]], 0)]).astype(BF16)
    a2p = jnp.stack([jnp.concatenate([a2[0], zeros], 0), jnp.concatenate([zeros, a2[1]], 0)]).astype(BF16)
    full = lambda shape: pl.BlockSpec(shape, lambda i: (0,) * len(shape))
    row_blk = pl.BlockSpec((ts, B_WIDTH), lambda i: (i, 0))
    out = jax.ShapeDtypeStruct((n_tok, B_WIDTH), F32)
    return pl.pallas_call(
        functools.partial(_rwkv_prep_kernel, blocks_per_seq=blocks_per_seq),
        grid=(n_tok // ts,),
        in_specs=[
            pl.BlockSpec((ts, B_COLS), lambda i: (i, 0)),
            pl.BlockSpec((SUBLANES, B_COLS), lambda i: (jnp.maximum(i * halo - 1, 0), 0)),
            pl.BlockSpec((SUBLANES, B_COLS), lambda i: (jnp.minimum((i + 1) * halo, n_halo - 1), 0)),
            full((1, B_COLS)), full((1, B_COLS)),
            full((2, B_WIDTH)), full((2, LORA_PAIR, B_WIDTH)),
            full((2, B_WIDTH)), full((2, LORA_PAIR, B_WIDTH)),
            full((GATE_LORA, B_WIDTH)),
            full((1, B_WIDTH)), full((1, B_WIDTH)), full((1, B_WIDTH)),
        ],
        out_specs=[row_blk] * 11,
        out_shape=[out] * 11,
        compiler_params=_params(1, 56),
        name="rwkv_prep",
    )(p_b, p_b, p_b, mu_prev.reshape(1, B_COLS), mu_next.reshape(1, B_COLS), w0, w2p, a0, a2p,
      g2.astype(BF16), k_k.reshape(1, B_WIDTH), k_a.reshape(1, B_WIDTH), r_k.reshape(1, B_WIDTH))


def _split3(x):
    hi = x.astype(BF16)
    rest = x - hi.astype(F32)
    mid = rest.astype(BF16)
    return hi, mid, (rest - mid.astype(F32)).astype(BF16)


def _rwkv_scan_kernel(r0, v0, kk0, lw0, kd0, b0, r1, v1, kk1, lw1, kd1, b1, o0_ref, o1_ref, s_ref, *, chunk):
    @pl.when(pl.program_id(2) == 0)
    def _():
        s_ref[...] = jnp.zeros_like(s_ref)

    t, w = r0.shape
    heads = w // B_HEAD_DIM
    n_chunks = t // chunk
    row = lax.broadcasted_iota(jnp.int32, (t, t), 0)
    col = lax.broadcasted_iota(jnp.int32, (t, t), 1)
    same = (row // chunk) == (col // chunk)
    eye = (row == col).astype(F32)
    head_blk = (lax.broadcasted_iota(jnp.int32, (w, w), 0) // B_HEAD_DIM
                == lax.broadcasted_iota(jnp.int32, (w, w), 1) // B_HEAD_DIM)
    chunk_of_row = lax.broadcasted_iota(jnp.int32, (t, 1), 0) // chunk

    def by_chunk(x):
        return jnp.concatenate([jnp.where(chunk_of_row == c, x, 0.0) for c in range(n_chunks)], axis=1)

    def head(x, h):
        return x[:, h * B_HEAD_DIM:(h + 1) * B_HEAD_DIM]

    dirs = ((r0, v0, kk0, lw0, kd0, b0), (r1, v1, kk1, lw1, kd1, b1))
    out_refs = (o0_ref, o1_ref)
    strict, incl, pre = [], [], []
    for d, (r_ref, v_ref, kk_ref, lw_ref, kd_ref, b_ref) in enumerate(dirs):
        earlier = (col > row) if d == 1 else (col < row)
        strict.append(same & earlier)
        incl.append(same & (earlier | (col == row)))
        lw = lw_ref[...]
        sums = jnp.dot(jnp.concatenate([incl[d], same], axis=0).astype(BF16),
                       jnp.concatenate(_split3(lw), axis=1), preferred_element_type=F32)
        sums = sums[:, :w] + sums[:, w:2 * w] + sums[:, 2 * w:]
        cum, tot = sums[:t], sums[t:]
        grow, shrink, to_end = jnp.exp(cum), jnp.exp(-cum), jnp.exp(tot - cum)
        k, b = kd_ref[...], b_ref[...]
        pre.append(dict(v=v_ref[...], rt=r_ref[...] * grow, kb=kk_ref[...] * jnp.exp(cum - lw),
                        kt=k * shrink, bt=b * shrink, kt_end=k * to_end, bt_end=b * to_end,
                        decay_end=jnp.exp(tot)))

    units = [(d, h) for d in range(2) for h in range(heads)]
    part = lambda name, u: head(pre[u[0]][name], u[1])
    cross = {u: _bdot_nt(jnp.concatenate([part("kb", u), part("rt", u)], axis=0),
                         jnp.concatenate([part("kt", u), part("bt", u)], axis=0)) for u in units}
    a_k = {u: jnp.where(strict[u[0]], cross[u][:t, :t], 0.0) for u in units}
    a_b = {u: jnp.where(strict[u[0]], cross[u][:t, t:], 0.0) for u in units}
    a_rk = {u: jnp.where(incl[u[0]], cross[u][t:, :t], 0.0) for u in units}
    a_rb = {u: jnp.where(incl[u[0]], cross[u][t:, t:], 0.0) for u in units}
    inv = {u: eye - a_b[u] for u in units}
    power = {u: _bdot(a_b[u], a_b[u]) for u in units}
    akv = {u: _bdot(a_k[u], part("v", u)) for u in units}
    span = 2
    while span < chunk:
        if 2 * span < chunk:
            both = {u: _bdot(jnp.concatenate([inv[u], power[u]], axis=0), power[u]) for u in units}
            inv = {u: inv[u] + both[u][:t] for u in units}
            power = {u: both[u][t:] for u in units}
        else:
            inv = {u: inv[u] + _bdot(inv[u], power[u]) for u in units}
        span *= 2
    sol = {u: _bdot(inv[u], jnp.concatenate([part("kb", u), akv[u]], axis=1)) for u in units}
    free = {}
    for u in units:
        v = part("v", u)
        p, q = sol[u][:, :B_HEAD_DIM], sol[u][:, B_HEAD_DIM:]
        rhs = jnp.concatenate([jnp.concatenate([v, jnp.zeros_like(v)], axis=1),
                               jnp.concatenate([-q, -p], axis=1)], axis=0)
        free[u] = _bdot(jnp.concatenate([a_rk[u], a_rb[u]], axis=1), rhs)

    def all_heads(table, d, lo):
        return jnp.concatenate([table[(d, h)][:, lo:lo + B_HEAD_DIM] for h in range(heads)], axis=1)

    read, out_free, trans, inject = [], [], [], []
    for d in range(2):
        p_all, q_all = all_heads(sol, d, 0), all_heads(sol, d, B_HEAD_DIM)
        out_free.append(all_heads(free, d, 0))
        read.append(pre[d]["rt"] + all_heads(free, d, B_HEAD_DIM))
        bt_end_c = by_chunk(pre[d]["bt_end"])
        trans.append(_bdot_tn(p_all, bt_end_c))
        inject.append(_bdot_tn(jnp.concatenate([pre[d]["v"], q_all], axis=0),
                               jnp.concatenate([by_chunk(pre[d]["kt_end"]), -bt_end_c], axis=0)))

    state = [s_ref[0], s_ref[1]]
    for i in range(n_chunks):
        for d in range(2):
            c = n_chunks - 1 - i if d == 1 else i
            rows = slice(c * chunk, (c + 1) * chunk)
            cols = slice(c * w, (c + 1) * w)
            blocks = jnp.where(head_blk, jnp.concatenate([state[d]] * heads, axis=0), 0.0)
            out_refs[d][rows, :] = _bdot_nt(read[d][rows], blocks) + out_free[d][rows]
            inj = jnp.where(head_blk, inject[d][:, cols], 0.0)
            inj = sum(inj[h * B_HEAD_DIM:(h + 1) * B_HEAD_DIM] for h in range(heads))
            state[d] = (state[d] * pre[d]["decay_end"][c * chunk:c * chunk + 1, :]
                        - _bdot(state[d], jnp.where(head_blk, trans[d][:, cols], 0.0)) + inj)
    s_ref[0] = state[0]
    s_ref[1] = state[1]


def _rwkv_scan(r, v, kk, lw0, lw1, kd0, kd1, b0, b1, bsz, seq):
    tile = min(RWKV_SCAN_ROWS, seq)
    chunk = min(RWKV_CHUNK, tile)
    nb = seq // tile
    fwd = pl.BlockSpec((tile, RWKV_SCAN_LANES), lambda b, hg, c: (b * nb + c, hg))
    bwd = pl.BlockSpec((tile, RWKV_SCAN_LANES), lambda b, hg, c: (b * nb + nb - 1 - c, hg))
    out = jax.ShapeDtypeStruct(r.shape, F32)
    return pl.pallas_call(
        functools.partial(_rwkv_scan_kernel, chunk=chunk),
        grid=(bsz, B_WIDTH // RWKV_SCAN_LANES, nb),
        in_specs=[fwd] * 6 + [bwd] * 6,
        out_specs=[fwd, bwd],
        out_shape=[out, out],
        scratch_shapes=[pltpu.VMEM((2, B_HEAD_DIM, RWKV_SCAN_LANES), F32)],
        compiler_params=_params(3, 48),
        name="rwkv_scan",
    )(r, v, kk, lw0, kd0, b0, r, v, kk, lw1, kd1, b1)


def _rwkv_post_kernel(o0_ref, o1_ref, bonus_ref, g_ref, gn_g_ref, gn_b_ref, y_ref):
    ones = _head_ones()
    o = o0_ref[...] + o1_ref[...]
    mu = _head_sum(o, ones) * (1.0 / B_HEAD_DIM)
    oc = o - mu
    var = _head_sum(oc * oc, ones) * (1.0 / B_HEAD_DIM)
    normed = oc * lax.rsqrt(var + GN_EPS) * gn_g_ref[...] + gn_b_ref[...]
    y_ref[...] = ((normed + bonus_ref[...]) * g_ref[...]).astype(y_ref.dtype)


def _rwkv_post(o0, o1, bonus, g, gn_g, gn_b):
    n_tok = o0.shape[0]
    ts = min(RWKV_PREP_ROWS, n_tok)
    row_blk = pl.BlockSpec((ts, B_WIDTH), lambda i: (i, 0))
    vec = pl.BlockSpec((1, B_WIDTH), lambda i: (0, 0))
    return pl.pallas_call(
        _rwkv_post_kernel,
        grid=(n_tok // ts,),
        in_specs=[row_blk] * 4 + [vec, vec],
        out_specs=row_blk,
        out_shape=jax.ShapeDtypeStruct((n_tok, B_WIDTH), BF16),
        compiler_params=_params(1, 32),
        name="rwkv_post",
    )(o0, o1, bonus, g, gn_g.reshape(1, B_WIDTH), gn_b.reshape(1, B_WIDTH))


def _rwkv(p_b, mu_prev, mu_next, w0, w2, a0, a2, g2, k_k, k_a, r_k, gn_g, gn_b, bsz, seq):
    r, v, kk, lw0, lw1, kd0, kd1, b0, b1, g, bonus = _rwkv_prep(
        p_b, mu_prev, mu_next, w0, w2, a0, a2, g2, k_k, k_a, r_k, seq)
    o0, o1 = _rwkv_scan(r, v, kk, lw0, lw1, kd0, kd1, b0, b1, bsz, seq)
    return _rwkv_post(o0, o1, bonus, g, gn_g, gn_b)


def _merge_kernel(ya_ref, yb_ref, yc_ref, ga_ref, gb_ref, gc_ref, wa_ref, wb_ref, wc_ref, o_ref):
    def branch(y_ref, w_ref, gate_ref):
        return _sigmoid(gate_ref[...]) * jnp.dot(y_ref[...], w_ref[...], preferred_element_type=F32)

    merged = branch(ya_ref, wa_ref, ga_ref) + branch(yb_ref, wb_ref, gb_ref) + branch(yc_ref, wc_ref, gc_ref)
    o_ref[...] = merged.astype(o_ref.dtype)


def _merge(y_a, y_b, y_c, p_g, wb_a, wb_b, wb_c):
    n_tok = y_a.shape[0]
    tm = min(MERGE_ROWS, n_tok)
    tn = MERGE_COLS
    nj = D_MODEL // tn

    def rows(width):
        return pl.BlockSpec((tm, width), lambda i, j: (i, 0))

    def gate(branch):
        return pl.BlockSpec((tm, tn), lambda i, j: (i, branch * nj + j))

    def weight(depth):
        return pl.BlockSpec((depth, tn), lambda i, j: (0, j))

    return pl.pallas_call(
        _merge_kernel,
        grid=(n_tok // tm, nj),
        in_specs=[rows(A_Q), rows(B_WIDTH), rows(C_OUT), gate(0), gate(1), gate(2),
                  weight(A_Q), weight(B_WIDTH), weight(C_OUT)],
        out_specs=pl.BlockSpec((tm, tn), lambda i, j: (i, j)),
        out_shape=jax.ShapeDtypeStruct((n_tok, D_MODEL), BF16),
        compiler_params=_params(2, 32),
        name="gated_merge",
    )(y_a, y_b, y_c, p_g, p_g, p_g, wb_a, wb_b, wb_c)


def _first_lane_of(mask, lane):
    return jnp.min(jnp.where(mask, lane, float(ROUTER_COLS)), axis=-1, keepdims=True)


def _route_rows(logits):
    lane = lax.broadcasted_iota(jnp.int32, logits.shape, 1).astype(F32)
    grp_logit = jnp.where(lane < N_GROUPS, logits, -jnp.inf)
    e = jnp.exp(grp_logit - jnp.max(grp_logit, axis=-1, keepdims=True))
    prob = e / jnp.sum(e, axis=-1, keepdims=True)
    grp_w = jnp.max(prob, axis=-1, keepdims=True)
    grp = _first_lane_of(prob == grp_w, lane)
    first = N_GROUPS + grp * EXPERTS_PER_GROUP
    cand = jnp.where((lane >= first) & (lane < first + EXPERTS_PER_GROUP), logits, -jnp.inf)
    top1 = jnp.max(cand, axis=-1, keepdims=True)
    lane1 = _first_lane_of(cand == top1, lane)
    cand = jnp.where(lane == lane1, -jnp.inf, cand)
    top2 = jnp.max(cand, axis=-1, keepdims=True)
    lane2 = _first_lane_of(cand == top2, lane)
    e2 = jnp.exp(top2 - top1)
    gate1 = grp_w / (1.0 + e2)
    gate2 = grp_w * e2 / (1.0 + e2)
    return jnp.where(lane == 0, gate1,
                     jnp.where(lane == 1, gate2,
                               jnp.where(lane == 2, lane1 - N_GROUPS,
                                         jnp.where(lane == 3, lane2 - N_GROUPS, 0.0))))


def _out_proj_kernel(m_ref, x_ref, w_ref, g_ref, b_ref, wr_ref, br_ref, x_o, route_o, *, alpha):
    z = alpha * x_ref[...] + jnp.dot(m_ref[...], w_ref[...], preferred_element_type=F32)
    x1 = _layer_norm(z, g_ref[...], b_ref[...])
    x_o[...] = x1
    x_hi = x1.astype(BF16)
    x_lo = (x1 - x_hi.astype(F32)).astype(BF16)
    by_hi = jnp.dot(x_hi, wr_ref[...], preferred_element_type=F32)
    by_lo = jnp.dot(x_lo, wr_ref[:, :ROUTER_COLS], preferred_element_type=F32)
    logits = by_hi[:, :ROUTER_COLS] + by_hi[:, ROUTER_COLS:] + by_lo + br_ref[...]
    route_o[...] = _route_rows(logits)


def _out_proj(merged, x, w_out, ln_g, ln_b, w_router, b_router, alpha):
    n_tok = x.shape[0]
    tm = min(OUT_ROWS, n_tok)
    rows = lambda width: pl.BlockSpec((tm, width), lambda i: (i, 0))
    full = lambda shape: pl.BlockSpec(shape, lambda i: (0, 0))
    return pl.pallas_call(
        functools.partial(_out_proj_kernel, alpha=alpha),
        grid=(n_tok // tm,),
        in_specs=[rows(D_MODEL), rows(D_MODEL), full((D_MODEL, D_MODEL)), full((1, D_MODEL)),
                  full((1, D_MODEL)), full((D_MODEL, 2 * ROUTER_COLS)), full((1, ROUTER_COLS))],
        out_specs=[rows(D_MODEL), rows(ROUTER_COLS)],
        out_shape=[jax.ShapeDtypeStruct((n_tok, D_MODEL), F32),
                   jax.ShapeDtypeStruct((n_tok, ROUTER_COLS), F32)],
        compiler_params=_params(1, 48),
        name="out_proj_ln_router",
    )(merged, x, w_out, ln_g.reshape(1, D_MODEL), ln_b.reshape(1, D_MODEL), w_router, b_router)


def _dispatch(expert):
    n_tok = expert.shape[0]
    flat_e = expert.reshape(-1)
    n_asg = flat_e.shape[0]
    order = jnp.argsort(flat_e)
    rank = jnp.argsort(order)
    counts = jnp.bincount(flat_e, length=N_EXPERTS)
    padded = (counts + MOE_ROWS - 1) // MOE_ROWS * MOE_ROWS
    pad_end = jnp.cumsum(padded)
    pad_start = pad_end - padded
    start = jnp.cumsum(counts) - counts
    dest = (pad_start - start)[flat_e] + rank
    n_rows = (n_asg + MOE_ROWS - 1) // MOE_ROWS * MOE_ROWS + N_EXPERTS * MOE_ROWS
    n_blk = n_rows // MOE_ROWS
    blk_expert = jnp.minimum(jnp.searchsorted(pad_end, jnp.arange(n_blk) * MOE_ROWS, side='right'),
                             N_EXPERTS - 1).astype(jnp.int32)
    row_expert = jnp.repeat(blk_expert, MOE_ROWS)
    offset = jnp.arange(n_rows) - pad_start[row_expert]
    source = order[jnp.clip(start[row_expert] + offset, 0, n_asg - 1)] // TOP_K
    row_tok = jnp.where(offset < counts[row_expert], source, 0).astype(jnp.int32)
    return dest.reshape(n_tok, TOP_K).astype(jnp.int32), row_tok, blk_expert


def _expert_kernel(blk_expert_ref, x_ref, wg_ref, wu_ref, wd_ref, o_ref):
    x = x_ref[...].astype(BF16)
    gate = jnp.dot(x, wg_ref[...].astype(BF16), preferred_element_type=F32)
    up = jnp.dot(x, wu_ref[...].astype(BF16), preferred_element_type=F32)
    hid = gate * _sigmoid(gate) * up
    o_ref[...] = jnp.dot(hid.astype(BF16), wd_ref[...].astype(BF16), preferred_element_type=F32)


def _experts(x_rows, blk_expert, w_gate, w_up, w_down, layer):
    n_rows = x_rows.shape[0]
    grid_spec = pltpu.PrefetchScalarGridSpec(
        num_scalar_prefetch=1,
        grid=(n_rows // MOE_ROWS,),
        in_specs=[pl.BlockSpec((MOE_ROWS, D_MODEL), lambda i, e: (i, 0)),
                  pl.BlockSpec((None, None, D_MODEL, EXPERT_FF), lambda i, e: (layer, e[i], 0, 0)),
                  pl.BlockSpec((None, None, D_MODEL, EXPERT_FF), lambda i, e: (layer, e[i], 0, 0)),
                  pl.BlockSpec((None, None, EXPERT_FF, D_MODEL), lambda i, e: (layer, e[i], 0, 0))],
        out_specs=pl.BlockSpec((MOE_ROWS, D_MODEL), lambda i, e: (i, 0)),
    )
    return pl.pallas_call(
        _expert_kernel,
        grid_spec=grid_spec,
        out_shape=jax.ShapeDtypeStruct((n_rows, D_MODEL), F32),
        compiler_params=_params(1, 56),
        name="expert_ffn",
    )(blk_expert, x_rows, w_gate, w_up, w_down)


def _combine_kernel(x_ref, ya_ref, yb_ref, gate_ref, g_ref, b_ref, x_o, xb_o, *, alpha):
    gate = gate_ref[...]
    moe = gate[:, 0:1] * ya_ref[...] + gate[:, 1:2] * yb_ref[...]
    x2 = _layer_norm(alpha * x_ref[...] + moe, g_ref[...], b_ref[...])
    x_o[...] = x2
    xb_o[...] = x2.astype(BF16)


def _combine(x1, y_first, y_second, route, ln_g, ln_b, alpha):
    n_tok = x1.shape[0]
    tm = min(OUT_ROWS, n_tok)
    rows = lambda width: pl.BlockSpec((tm, width), lambda i: (i, 0))
    vec = pl.BlockSpec((1, D_MODEL), lambda i: (0, 0))
    return pl.pallas_call(
        functools.partial(_combine_kernel, alpha=alpha),
        grid=(n_tok // tm,),
        in_specs=[rows(D_MODEL), rows(D_MODEL), rows(D_MODEL), rows(LANES), vec, vec],
        out_specs=[rows(D_MODEL), rows(D_MODEL)],
        out_shape=[jax.ShapeDtypeStruct((n_tok, D_MODEL), F32),
                   jax.ShapeDtypeStruct((n_tok, D_MODEL), BF16)],
        compiler_params=_params(1, 32),
        name="moe_combine_ln",
    )(x1, y_first, y_second, route, ln_g.reshape(1, D_MODEL), ln_b.reshape(1, D_MODEL))


def _moe(x1, route, w_gate, w_up, w_down, layer, ln_g, ln_b, alpha):
    dest, row_tok, blk_expert = _dispatch(route[:, TOP_K:2 * TOP_K].astype(jnp.int32))
    y_rows = _experts(x1[row_tok], blk_expert, w_gate, w_up, w_down, layer)
    return _combine(x1, y_rows[dest[:, 0]], y_rows[dest[:, 1]], route, ln_g, ln_b, alpha)


def kernel(x, w_in, w_branch, w_out, mu_prev, mu_next, rwkv_w0, rwkv_w2, rwkv_a0, rwkv_a2, rwkv_g2, rwkv_k_k, rwkv_k_a, rwkv_r_k, rwkv_gn_g, rwkv_gn_b, q_norm, k_norm, rel_bias, ln1_g, ln1_b, router_group_w, router_group_b, router_expert_w, router_expert_b, w_gate, w_up, w_down, ln2_g, ln2_b):
    bsz, seq, _ = x.shape
    depth = w_in.shape[0]
    alpha = (2.0 * depth) ** 0.25
    xf = x.reshape(bsz * seq, D_MODEL)
    xb = xf.astype(BF16)
    col_b = A_COLS
    col_c = A_COLS + B_COLS
    col_g = A_COLS + B_COLS + C_COLS
    for l in range(depth):
        w_l = w_in[l]
        p_a = _matmul(xb, w_l[:, :col_b].astype(BF16), 768)
        p_b = _matmul(xb, w_l[:, col_b:col_c].astype(BF16), 1152)
        p_c = _matmul(xb, w_l[:, col_c:col_g].astype(BF16), 1152)
        p_g = _matmul(xb, w_l[:, col_g:].astype(BF16), 1024)
        y_a = _attention_a(p_a, q_norm[l], k_norm[l], bsz, seq)
        y_b = _rwkv(p_b, mu_prev[l], mu_next[l], rwkv_w0[l], rwkv_w2[l], rwkv_a0[l], rwkv_a2[l],
                    rwkv_g2[l], rwkv_k_k[l], rwkv_k_a[l], rwkv_r_k[l], rwkv_gn_g[l], rwkv_gn_b[l],
                    bsz, seq)
        y_c = _attention_c(p_c, rel_bias, bsz, seq)
        wb = w_branch[l].astype(BF16)
        merged = _merge(y_a, y_b, y_c, p_g, wb[:A_Q], wb[A_Q:A_Q + B_WIDTH], wb[A_Q + B_WIDTH:])
        w_router = jnp.pad(jnp.concatenate([router_group_w[l], router_expert_w[l]], axis=1),
                           ((0, 0), (0, ROUTER_COLS - N_GROUPS - N_EXPERTS)))
        w_router_hi = w_router.astype(BF16)
        w_router = jnp.concatenate([w_router_hi, (w_router - w_router_hi.astype(F32)).astype(BF16)], axis=1)
        b_router = jnp.pad(jnp.concatenate([router_group_b[l], router_expert_b[l]]),
                           (0, ROUTER_COLS - N_GROUPS - N_EXPERTS)).reshape(1, ROUTER_COLS)
        x1, route = _out_proj(merged, xf, w_out[l].astype(BF16), ln1_g[l], ln1_b[l],
                              w_router, b_router, alpha)
        xf, xb = _moe(x1, route, w_gate, w_up, w_down, l, ln2_g[l], ln2_b[l], alpha)
    return xf.reshape(bsz, seq, D_MODEL)
```

```python
import functools

import numpy as np
import jax
import jax.numpy as jnp
from jax import lax
from jax.experimental import pallas as pl
from jax.experimental.pallas import tpu as pltpu

F32 = jnp.float32
BF16 = jnp.bfloat16
HIGHEST = lax.Precision.HIGHEST

D_MODEL = 2048
DEPTH = 4
GRID_W = 64
NEG_INF = -1e30
LN_EPS = 1e-5

A_HEADS = 8
A_KV_HEADS = 2
A_GROUP = A_HEADS // A_KV_HEADS
HEAD_DIM = 128
ROPE_THETA = 10000.0
QK_EPS = 1e-6

B_HEAD_DIM = 64
B_WIDTH = 1024
B_HEADS = B_WIDTH // B_HEAD_DIM
LORA_PAIR = 128
GATE_LORA = 128
GN_EPS = 64e-5

C_PATTERNS = ((128, 1), (512, 4), (2048, 16))
C_GROUPS = 3
C_HEADS_PER_GROUP = 4
C_HEADS = C_GROUPS * C_HEADS_PER_GROUP
C_HALF = 64
REL_BUCKETS = 32
REL_MAX_DISTANCE = 1024

N_GROUPS = 8
EXPERTS_PER_GROUP = 8
N_EXPERTS = N_GROUPS * EXPERTS_PER_GROUP
TOP_K = 2
EXPERT_FF = 384

A_Q = A_HEADS * HEAD_DIM
A_KV = A_KV_HEADS * HEAD_DIM
A_COLS = A_Q + 2 * A_KV
B_COLS = 3 * B_WIDTH + 2 * LORA_PAIR + GATE_LORA
C_QKV = C_HEADS * HEAD_DIM
C_COLS = 3 * C_QKV
N_BRANCHES = 3
GATE_COLS = N_BRANCHES * D_MODEL
C_OUT = C_HEADS_PER_GROUP * HEAD_DIM

LANES = 128
SUBLANES = 8
MIB = 1024 * 1024

MM_ROWS = 1024
ATTN_A_Q_ROWS = 256
ATTN_A_KV_CHUNK = 1024
ATTN_C_Q = 128
ATTN_C_K = 2 * ATTN_C_Q
ATTN_C_TOKENS = ATTN_C_Q * C_PATTERNS[-1][1]
RWKV_PREP_ROWS = 256
RWKV_CHUNK = 64
RWKV_SCAN_ROWS = 256
RWKV_SCAN_LANES = 256
MERGE_ROWS = 512
MERGE_COLS = 512
OUT_ROWS = 256
MOE_ROWS = 256
ROUTER_COLS = 128


def _params(n_grid, vmem_mib):
    return pltpu.CompilerParams(dimension_semantics=("arbitrary",) * n_grid,
                                vmem_limit_bytes=vmem_mib * MIB)


def _bdot(a, b):
    return jnp.dot(a.astype(BF16), b.astype(BF16), preferred_element_type=F32)


def _bdot_nt(a, b):
    return lax.dot_general(a.astype(BF16), b.astype(BF16), (((1,), (1,)), ((), ())),
                           preferred_element_type=F32)


def _bdot_tn(a, b):
    return lax.dot_general(a.astype(BF16), b.astype(BF16), (((0,), (0,)), ((), ())),
                           preferred_element_type=F32)


def _sigmoid(x):
    return 1.0 / (1.0 + jnp.exp(-x))


def _layer_norm(z, g, b):
    mu = jnp.mean(z, axis=-1, keepdims=True)
    zc = z - mu
    var = jnp.mean(zc * zc, axis=-1, keepdims=True)
    return zc * lax.rsqrt(var + LN_EPS) * g + b


def _mm_kernel(x_ref, w_ref, o_ref):
    o_ref[...] = jnp.dot(x_ref[...], w_ref[...], preferred_element_type=F32).astype(o_ref.dtype)


def _matmul(x, w, tn, out_dtype=F32):
    m, k = x.shape
    n = w.shape[1]
    tm = min(MM_ROWS, m)
    return pl.pallas_call(
        _mm_kernel,
        grid=(m // tm, n // tn),
        in_specs=[pl.BlockSpec((tm, k), lambda i, j: (i, 0)),
                  pl.BlockSpec((k, tn), lambda i, j: (0, j))],
        out_specs=pl.BlockSpec((tm, tn), lambda i, j: (i, j)),
        out_shape=jax.ShapeDtypeStruct((m, n), out_dtype),
        compiler_params=_params(2, 48),
        name="dense_proj",
    )(x, w)


def _rope_tables(seq):
    n = HEAD_DIM // 2
    inv = ROPE_THETA ** (-jnp.arange(0, n, 2, dtype=F32) / n)
    pos = jnp.arange(seq)
    ang_r = (pos // GRID_W).astype(F32)[:, None] * inv[None, :]
    ang_c = (pos % GRID_W).astype(F32)[:, None] * inv[None, :]
    cos = jnp.concatenate([jnp.cos(ang_r)] * 2 + [jnp.cos(ang_c)] * 2, axis=-1)
    sin = jnp.concatenate([-jnp.sin(ang_r), jnp.sin(ang_r), -jnp.sin(ang_c), jnp.sin(ang_c)], axis=-1)
    return cos, sin


def _norm_rope(t, gain, cos, sin):
    t = t * lax.rsqrt(jnp.mean(t * t, axis=-1, keepdims=True) + QK_EPS) * gain
    lane = lax.broadcasted_iota(jnp.int32, t.shape, 1)
    quarter = HEAD_DIM // 4
    partner = jnp.where(lane % (2 * quarter) < quarter,
                        pltpu.roll(t, HEAD_DIM - quarter, axis=1), pltpu.roll(t, quarter, axis=1))
    return t * cos + partner * sin


def _attn_a_kernel(q_ref, k_ref, v_ref, cq_ref, sq_ref, ck_ref, sk_ref, qg_ref, kg_ref, o_ref,
                   kp_ref, vp_ref):
    @pl.when((pl.program_id(2) == 0) & (pl.program_id(3) == 0))
    def _():
        kp_ref[...] = _norm_rope(k_ref[...], kg_ref[...], ck_ref[...], sk_ref[...]).astype(BF16)
        vp_ref[:, :HEAD_DIM] = v_ref[...].astype(BF16)
        vp_ref[:, HEAD_DIM:] = jnp.ones((v_ref.shape[0], HEAD_DIM), BF16)

    q = (_norm_rope(q_ref[...], qg_ref[...], cq_ref[...], sq_ref[...]) * HEAD_DIM ** -0.5).astype(BF16)
    seq = kp_ref.shape[0]
    tk = min(ATTN_A_KV_CHUNK, seq)

    def scores(c):
        return lax.dot_general(q, kp_ref[c * tk:(c + 1) * tk, :], (((1,), (1,)), ((), ())),
                               preferred_element_type=F32)

    m = acc = None
    s_next = scores(0)
    for c in range(seq // tk):
        s = s_next
        if (c + 1) * tk < seq:
            s_next = scores(c + 1)
        m_c = jnp.max(s, axis=-1, keepdims=True)
        m_new = m_c if m is None else jnp.maximum(m, m_c)
        p = jnp.exp(s - m_new)
        pv = jnp.dot(p.astype(BF16), vp_ref[c * tk:(c + 1) * tk, :], preferred_element_type=F32)
        acc = pv if m is None else acc * jnp.exp(m - m_new) + pv
        m = m_new
    o_ref[...] = (acc[:, :HEAD_DIM] / acc[:, HEAD_DIM:]).astype(o_ref.dtype)


def _attention_a(p_a, q_gain, k_gain, bsz, seq):
    tq = min(ATTN_A_Q_ROWS, seq)
    nq = seq // tq
    cos, sin = _rope_tables(seq)
    k_col = A_HEADS
    v_col = A_HEADS + A_KV_HEADS
    q_tab = pl.BlockSpec((tq, HEAD_DIM), lambda b, kv, g, i: (i, 0))
    k_tab = pl.BlockSpec((seq, HEAD_DIM), lambda b, kv, g, i: (0, 0))
    gain = pl.BlockSpec((1, HEAD_DIM), lambda b, kv, g, i: (0, 0))
    return pl.pallas_call(
        _attn_a_kernel,
        grid=(bsz, A_KV_HEADS, A_GROUP, nq),
        in_specs=[
            pl.BlockSpec((tq, HEAD_DIM), lambda b, kv, g, i: (b * nq + i, kv * A_GROUP + g)),
            pl.BlockSpec((seq, HEAD_DIM), lambda b, kv, g, i: (b, k_col + kv)),
            pl.BlockSpec((seq, HEAD_DIM), lambda b, kv, g, i: (b, v_col + kv)),
            q_tab, q_tab, k_tab, k_tab, gain, gain,
        ],
        out_specs=pl.BlockSpec((tq, HEAD_DIM), lambda b, kv, g, i: (b * nq + i, kv * A_GROUP + g)),
        out_shape=jax.ShapeDtypeStruct((bsz * seq, A_Q), BF16),
        scratch_shapes=[pltpu.VMEM((seq, HEAD_DIM), BF16), pltpu.VMEM((seq, 2 * HEAD_DIM), BF16)],
        compiler_params=_params(4, 48),
        name="axial_gqa",
    )(p_a, p_a, p_a, cos, sin, cos, sin, q_gain.reshape(1, HEAD_DIM), k_gain.reshape(1, HEAD_DIM))


def _t5_bucket(rel):
    nb = REL_BUCKETS // 2
    max_exact = nb // 2
    n = np.abs(rel)
    large = max_exact + (np.log(np.maximum(n, 1) / max_exact) / np.log(REL_MAX_DISTANCE / max_exact)
                         * (nb - max_exact)).astype(np.int32)
    large = np.minimum(large, nb - 1)
    return (rel > 0).astype(np.int32) * nb + np.where(n < max_exact, n, large)


_C_WINDOW_OFFSETS = (0, -C_HALF, ATTN_C_Q - ATTN_C_K)


def _dilated_bias(rel_bias):
    per_group = []
    for gi, (_, dilation) in enumerate(C_PATTERNS):
        hs = slice(gi * C_HEADS_PER_GROUP, (gi + 1) * C_HEADS_PER_GROUP)
        variants = []
        for off in _C_WINDOW_OFFSETS:
            delta = off + np.arange(ATTN_C_K)[None, :] - np.arange(ATTN_C_Q)[:, None]
            vals = jnp.moveaxis(rel_bias[_t5_bucket(delta * dilation)][..., hs], -1, 0).astype(F32)
            variants.append(jnp.where((np.abs(delta) <= C_HALF)[None], vals, NEG_INF))
        per_group.append(jnp.stack(variants, axis=1))
    return jnp.stack(per_group, axis=0)


def _strided(start, size, stride):
    return pl.ds(start, size) if stride == 1 else pl.ds(start, size, stride=stride)


def _attn_c_kernel(q0, q1, q2, k0, k1, k2, v0, v1, v2, bias_ref, o_ref,
                   og0, og1, og2, ls0, ls1, ls2, *, seq):
    tile = pl.program_id(2)
    scale = HEAD_DIM ** -0.5
    groups = ((q0, k0, v0, og0, ls0), (q1, k1, v1, og1, ls1), (q2, k2, v2, og2, ls2))
    for gi, (q_ref, k_ref, v_ref, og_ref, ls_ref) in enumerate(groups):
        dil = C_PATTERNS[gi][1]
        n = seq // dil
        per_tile = ATTN_C_TOKENS // dil
        for res in range(dil):
            for blk in range(per_tile // ATTN_C_Q):
                m_local = blk * ATTN_C_Q
                m0 = tile * per_tile + m_local
                k_start = jnp.clip(m0 + _C_WINDOW_OFFSETS[1], 0, n - ATTN_C_K)
                variant = jnp.where(m0 == 0, 0, jnp.where(m0 == n - ATTN_C_Q, 2, 1))
                q_rows = _strided(m_local * dil + res, ATTN_C_Q, dil)
                k_rows = _strided(k_start * dil + res, ATTN_C_K, dil)
                s = _bdot_nt(q_ref[q_rows, :], k_ref[k_rows, :]) * scale + bias_ref[gi, variant]
                m = jnp.max(s, axis=-1, keepdims=True)
                p = jnp.exp(s - m)
                l = jnp.sum(p, axis=-1, keepdims=True)
                o = _bdot(p, v_ref[k_rows, :])
                og_ref[q_rows, :] = o / l
                ls_ref[q_rows, :] = jnp.broadcast_to(m + jnp.log(l), (ATTN_C_Q, HEAD_DIM))
    top = jnp.maximum(jnp.maximum(ls0[...], ls1[...]), ls2[...])
    e0, e1, e2 = jnp.exp(ls0[...] - top), jnp.exp(ls1[...] - top), jnp.exp(ls2[...] - top)
    o_ref[...] = ((e0 * og0[...] + e1 * og1[...] + e2 * og2[...]) / (e0 + e1 + e2)).astype(o_ref.dtype)


def _attention_c(p_c, rel_bias, bsz, seq):
    assert seq % ATTN_C_TOKENS == 0 and seq // C_PATTERNS[-1][1] >= ATTN_C_K
    nt = seq // ATTN_C_TOKENS
    bias = _dilated_bias(rel_bias)

    def head_col(part, gi):
        return lambda b, j, t: part * C_HEADS + gi * C_HEADS_PER_GROUP + j

    def q_spec(gi):
        col = head_col(0, gi)
        return pl.BlockSpec((ATTN_C_TOKENS, HEAD_DIM), lambda b, j, t: (b * nt + t, col(b, j, t)))

    def kv_spec(part, gi):
        col = head_col(part, gi)
        return pl.BlockSpec((seq, HEAD_DIM), lambda b, j, t: (b, col(b, j, t)))

    tile_buf = pltpu.VMEM((ATTN_C_TOKENS, HEAD_DIM), F32)
    return pl.pallas_call(
        functools.partial(_attn_c_kernel, seq=seq),
        grid=(bsz, C_HEADS_PER_GROUP, nt),
        in_specs=[q_spec(0), q_spec(1), q_spec(2),
                  kv_spec(1, 0), kv_spec(1, 1), kv_spec(1, 2),
                  kv_spec(2, 0), kv_spec(2, 1), kv_spec(2, 2),
                  pl.BlockSpec((C_GROUPS, None, len(_C_WINDOW_OFFSETS), ATTN_C_Q, ATTN_C_K),
                               lambda b, j, t: (0, j, 0, 0, 0))],
        out_specs=pl.BlockSpec((ATTN_C_TOKENS, HEAD_DIM), lambda b, j, t: (b * nt + t, j)),
        out_shape=jax.ShapeDtypeStruct((bsz * seq, C_OUT), BF16),
        scratch_shapes=[tile_buf] * 6,
        compiler_params=_params(3, 56),
        name="dilated_attn",
    )(*([p_c] * 9), bias)


def _head_ones():
    r = lax.broadcasted_iota(jnp.int32, (LANES, LANES), 0) // B_HEAD_DIM
    c = lax.broadcasted_iota(jnp.int32, (LANES, LANES), 1) // B_HEAD_DIM
    return (r == c).astype(F32)


def _head_sum(x, ones):
    parts = [jnp.dot(x[:, i:i + LANES], ones, precision=HIGHEST, preferred_element_type=F32)
             for i in range(0, x.shape[1], LANES)]
    return jnp.concatenate(parts, axis=1)


def _rwkv_prep_kernel(y_ref, yp_ref, yn_ref, mup_ref, mun_ref, w0_ref, w2_ref, a0_ref, a2_ref, g2_ref,
                      kk_ref, ka_ref, rk_ref,
                      r_o, v_o, kk_o, lw0_o, lw1_o, kd0_o, kd1_o, b0_o, b1_o, g_o, bonus_o,
                      *, blocks_per_seq):
    i = pl.program_id(0)
    ts = y_ref.shape[0]
    at_start = (i % blocks_per_seq) == 0
    at_end = (i % blocks_per_seq) == blocks_per_seq - 1
    rows = lax.broadcasted_iota(jnp.int32, (ts, 1), 0)

    def mixed(c0, c1):
        y = y_ref[:, c0:c1]
        before = jnp.where(at_start, 0.0, yp_ref[SUBLANES - 1:SUBLANES, c0:c1])
        after = jnp.where(at_end, 0.0, yn_ref[0:1, c0:c1])
        prev = jnp.where(rows == 0, before, pltpu.roll(y, 1, axis=0))
        nxt = jnp.where(rows == ts - 1, after, pltpu.roll(y, ts - 1, axis=0))
        return y + mup_ref[:, c0:c1] * (prev - y) + mun_ref[:, c0:c1] * (nxt - y)

    w = B_WIDTH
    r = mixed(0, w)
    k = mixed(w, 2 * w)
    v = mixed(2 * w, 3 * w)
    hw = mixed(3 * w, 3 * w + LORA_PAIR)
    ha = mixed(3 * w + LORA_PAIR, 3 * w + 2 * LORA_PAIR)
    hg = mixed(3 * w + 2 * LORA_PAIR, 3 * w + 2 * LORA_PAIR + GATE_LORA)
    ones = _head_ones()

    r_o[...] = r
    v_o[...] = v
    g_o[...] = _bdot(_sigmoid(hg), g2_ref[...])
    kk = k * kk_ref[...]
    kk = kk * lax.rsqrt(_head_sum(kk * kk, ones) + 1e-12)
    kk_o[...] = kk
    tanh_hw = jnp.tanh(hw)
    rrk = r * rk_ref[...]
    bonus = jnp.zeros_like(r)
    for d, (lw_o, kd_o, b_o) in enumerate(((lw0_o, kd0_o, b0_o), (lw1_o, kd1_o, b1_o))):
        z = -(w0_ref[d:d + 1, :] + _bdot(tanh_hw, w2_ref[d]))
        softplus = jnp.maximum(z, 0.0) + jnp.log(1.0 + jnp.exp(-jnp.abs(z)))
        lw_o[...] = -jnp.exp(-softplus - 0.5)
        a = _sigmoid(a0_ref[d:d + 1, :] + _bdot(ha, a2_ref[d]))
        kd = k * (1.0 + (a - 1.0) * ka_ref[...])
        kd_o[...] = kd
        b_o[...] = kk * a
        bonus = bonus + rrk * kd
    bonus_o[...] = _head_sum(bonus, ones) * v


def _rwkv_prep(p_b, mu_prev, mu_next, w0, w2, a0, a2, g2, k_k, k_a, r_k, seq):
    n_tok = p_b.shape[0]
    ts = min(RWKV_PREP_ROWS, seq)
    blocks_per_seq = seq // ts
    halo = ts // SUBLANES
    n_halo = n_tok // SUBLANES
    zeros = jnp.zeros_like(w2[0])
    w2p = jnp.stack([jnp.concatenate([w2[0], zeros], 0), jnp.concatenate([zeros, w2[1]], 0)]).astype(BF16)
    a2p = jnp.stack([jnp.concatenate([a2[0], zeros], 0), jnp.concatenate([zeros, a2[1]], 0)]).astype(BF16)
    full = lambda shape: pl.BlockSpec(shape, lambda i: (0,) * len(shape))
    row_blk = pl.BlockSpec((ts, B_WIDTH), lambda i: (i, 0))
    out = jax.ShapeDtypeStruct((n_tok, B_WIDTH), F32)
    return pl.pallas_call(
        functools.partial(_rwkv_prep_kernel, blocks_per_seq=blocks_per_seq),
        grid=(n_tok // ts,),
        in_specs=[
            pl.BlockSpec((ts, B_COLS), lambda i: (i, 0)),
            pl.BlockSpec((SUBLANES, B_COLS), lambda i: (jnp.maximum(i * halo - 1, 0), 0)),
            pl.BlockSpec((SUBLANES, B_COLS), lambda i: (jnp.minimum((i + 1) * halo, n_halo - 1), 0)),
            full((1, B_COLS)), full((1, B_COLS)),
            full((2, B_WIDTH)), full((2, LORA_PAIR, B_WIDTH)),
            full((2, B_WIDTH)), full((2, LORA_PAIR, B_WIDTH)),
            full((GATE_LORA, B_WIDTH)),
            full((1, B_WIDTH)), full((1, B_WIDTH)), full((1, B_WIDTH)),
        ],
        out_specs=[row_blk] * 11,
        out_shape=[out] * 11,
        compiler_params=_params(1, 56),
        name="rwkv_prep",
    )(p_b, p_b, p_b, mu_prev.reshape(1, B_COLS), mu_next.reshape(1, B_COLS), w0, w2p, a0, a2p,
      g2.astype(BF16), k_k.reshape(1, B_WIDTH), k_a.reshape(1, B_WIDTH), r_k.reshape(1, B_WIDTH))


def _split3(x):
    hi = x.astype(BF16)
    rest = x - hi.astype(F32)
    mid = rest.astype(BF16)
    return hi, mid, (rest - mid.astype(F32)).astype(BF16)


def _rwkv_scan_kernel(r0, v0, kk0, lw0, kd0, b0, r1, v1, kk1, lw1, kd1, b1, o0_ref, o1_ref, s_ref, *, chunk):
    @pl.when(pl.program_id(2) == 0)
    def _():
        s_ref[...] = jnp.zeros_like(s_ref)

    t, w = r0.shape
    heads = w // B_HEAD_DIM
    n_chunks = t // chunk
    row = lax.broadcasted_iota(jnp.int32, (t, t), 0)
    col = lax.broadcasted_iota(jnp.int32, (t, t), 1)
    same = (row // chunk) == (col // chunk)
    eye = (row == col).astype(F32)
    head_blk = (lax.broadcasted_iota(jnp.int32, (w, w), 0) // B_HEAD_DIM
                == lax.broadcasted_iota(jnp.int32, (w, w), 1) // B_HEAD_DIM)
    chunk_of_row = lax.broadcasted_iota(jnp.int32, (t, 1), 0) // chunk

    def by_chunk(x):
        return jnp.concatenate([jnp.where(chunk_of_row == c, x, 0.0) for c in range(n_chunks)], axis=1)

    def head(x, h):
        return x[:, h * B_HEAD_DIM:(h + 1) * B_HEAD_DIM]

    dirs = ((r0, v0, kk0, lw0, kd0, b0), (r1, v1, kk1, lw1, kd1, b1))
    out_refs = (o0_ref, o1_ref)
    strict, incl, pre = [], [], []
    for d, (r_ref, v_ref, kk_ref, lw_ref, kd_ref, b_ref) in enumerate(dirs):
        earlier = (col > row) if d == 1 else (col < row)
        strict.append(same & earlier)
        incl.append(same & (earlier | (col == row)))
        lw = lw_ref[...]
        sums = jnp.dot(jnp.concatenate([incl[d], same], axis=0).astype(BF16),
                       jnp.concatenate(_split3(lw), axis=1), preferred_element_type=F32)
        sums = sums[:, :w] + sums[:, w:2 * w] + sums[:, 2 * w:]
        cum, tot = sums[:t], sums[t:]
        grow, shrink, to_end = jnp.exp(cum), jnp.exp(-cum), jnp.exp(tot - cum)
        k, b = kd_ref[...], b_ref[...]
        pre.append(dict(v=v_ref[...], rt=r_ref[...] * grow, kb=kk_ref[...] * jnp.exp(cum - lw),
                        kt=k * shrink, bt=b * shrink, kt_end=k * to_end, bt_end=b * to_end,
                        decay_end=jnp.exp(tot)))

    units = [(d, h) for d in range(2) for h in range(heads)]
    part = lambda name, u: head(pre[u[0]][name], u[1])
    cross = {u: _bdot_nt(jnp.concatenate([part("kb", u), part("rt", u)], axis=0),
                         jnp.concatenate([part("kt", u), part("bt", u)], axis=0)) for u in units}
    a_k = {u: jnp.where(strict[u[0]], cross[u][:t, :t], 0.0) for u in units}
    a_b = {u: jnp.where(strict[u[0]], cross[u][:t, t:], 0.0) for u in units}
    a_rk = {u: jnp.where(incl[u[0]], cross[u][t:, :t], 0.0) for u in units}
    a_rb = {u: jnp.where(incl[u[0]], cross[u][t:, t:], 0.0) for u in units}
    inv = {u: eye - a_b[u] for u in units}
    power = {u: _bdot(a_b[u], a_b[u]) for u in units}
    akv = {u: _bdot(a_k[u], part("v", u)) for u in units}
    span = 2
    while span < chunk:
        if 2 * span < chunk:
            both = {u: _bdot(jnp.concatenate([inv[u], power[u]], axis=0), power[u]) for u in units}
            inv = {u: inv[u] + both[u][:t] for u in units}
            power = {u: both[u][t:] for u in units}
        else:
            inv = {u: inv[u] + _bdot(inv[u], power[u]) for u in units}
        span *= 2
    sol = {u: _bdot(inv[u], jnp.concatenate([part("kb", u), akv[u]], axis=1)) for u in units}
    free = {}
    for u in units:
        v = part("v", u)
        p, q = sol[u][:, :B_HEAD_DIM], sol[u][:, B_HEAD_DIM:]
        rhs = jnp.concatenate([jnp.concatenate([v, jnp.zeros_like(v)], axis=1),
                               jnp.concatenate([-q, -p], axis=1)], axis=0)
        free[u] = _bdot(jnp.concatenate([a_rk[u], a_rb[u]], axis=1), rhs)

    def all_heads(table, d, lo):
        return jnp.concatenate([table[(d, h)][:, lo:lo + B_HEAD_DIM] for h in range(heads)], axis=1)

    read, out_free, trans, inject = [], [], [], []
    for d in range(2):
        p_all, q_all = all_heads(sol, d, 0), all_heads(sol, d, B_HEAD_DIM)
        out_free.append(all_heads(free, d, 0))
        read.append(pre[d]["rt"] + all_heads(free, d, B_HEAD_DIM))
        bt_end_c = by_chunk(pre[d]["bt_end"])
        trans.append(_bdot_tn(p_all, bt_end_c))
        inject.append(_bdot_tn(jnp.concatenate([pre[d]["v"], q_all], axis=0),
                               jnp.concatenate([by_chunk(pre[d]["kt_end"]), -bt_end_c], axis=0)))

    state = [s_ref[0], s_ref[1]]
    for i in range(n_chunks):
        for d in range(2):
            c = n_chunks - 1 - i if d == 1 else i
            rows = slice(c * chunk, (c + 1) * chunk)
            cols = slice(c * w, (c + 1) * w)
            blocks = jnp.where(head_blk, jnp.concatenate([state[d]] * heads, axis=0), 0.0)
            out_refs[d][rows, :] = _bdot_nt(read[d][rows], blocks) + out_free[d][rows]
            inj = jnp.where(head_blk, inject[d][:, cols], 0.0)
            inj = sum(inj[h * B_HEAD_DIM:(h + 1) * B_HEAD_DIM] for h in range(heads))
            state[d] = (state[d] * pre[d]["decay_end"][c * chunk:c * chunk + 1, :]
                        - _bdot(state[d], jnp.where(head_blk, trans[d][:, cols], 0.0)) + inj)
    s_ref[0] = state[0]
    s_ref[1] = state[1]


def _rwkv_scan(r, v, kk, lw0, lw1, kd0, kd1, b0, b1, bsz, seq):
    tile = min(RWKV_SCAN_ROWS, seq)
    chunk = min(RWKV_CHUNK, tile)
    nb = seq // tile
    fwd = pl.BlockSpec((tile, RWKV_SCAN_LANES), lambda b, hg, c: (b * nb + c, hg))
    bwd = pl.BlockSpec((tile, RWKV_SCAN_LANES), lambda b, hg, c: (b * nb + nb - 1 - c, hg))
    out = jax.ShapeDtypeStruct(r.shape, F32)
    return pl.pallas_call(
        functools.partial(_rwkv_scan_kernel, chunk=chunk),
        grid=(bsz, B_WIDTH // RWKV_SCAN_LANES, nb),
        in_specs=[fwd] * 6 + [bwd] * 6,
        out_specs=[fwd, bwd],
        out_shape=[out, out],
        scratch_shapes=[pltpu.VMEM((2, B_HEAD_DIM, RWKV_SCAN_LANES), F32)],
        compiler_params=_params(3, 48),
        name="rwkv_scan",
    )(r, v, kk, lw0, kd0, b0, r, v, kk, lw1, kd1, b1)


def _rwkv_post_kernel(o0_ref, o1_ref, bonus_ref, g_ref, gn_g_ref, gn_b_ref, y_ref):
    ones = _head_ones()
    o = o0_ref[...] + o1_ref[...]
    mu = _head_sum(o, ones) * (1.0 / B_HEAD_DIM)
    oc = o - mu
    var = _head_sum(oc * oc, ones) * (1.0 / B_HEAD_DIM)
    normed = oc * lax.rsqrt(var + GN_EPS) * gn_g_ref[...] + gn_b_ref[...]
    y_ref[...] = ((normed + bonus_ref[...]) * g_ref[...]).astype(y_ref.dtype)


def _rwkv_post(o0, o1, bonus, g, gn_g, gn_b):
    n_tok = o0.shape[0]
    ts = min(RWKV_PREP_ROWS, n_tok)
    row_blk = pl.BlockSpec((ts, B_WIDTH), lambda i: (i, 0))
    vec = pl.BlockSpec((1, B_WIDTH), lambda i: (0, 0))
    return pl.pallas_call(
        _rwkv_post_kernel,
        grid=(n_tok // ts,),
        in_specs=[row_blk] * 4 + [vec, vec],
        out_specs=row_blk,
        out_shape=jax.ShapeDtypeStruct((n_tok, B_WIDTH), BF16),
        compiler_params=_params(1, 32),
        name="rwkv_post",
    )(o0, o1, bonus, g, gn_g.reshape(1, B_WIDTH), gn_b.reshape(1, B_WIDTH))


def _rwkv(p_b, mu_prev, mu_next, w0, w2, a0, a2, g2, k_k, k_a, r_k, gn_g, gn_b, bsz, seq):
    r, v, kk, lw0, lw1, kd0, kd1, b0, b1, g, bonus = _rwkv_prep(
        p_b, mu_prev, mu_next, w0, w2, a0, a2, g2, k_k, k_a, r_k, seq)
    o0, o1 = _rwkv_scan(r, v, kk, lw0, lw1, kd0, kd1, b0, b1, bsz, seq)
    return _rwkv_post(o0, o1, bonus, g, gn_g, gn_b)


def _merge_kernel(ya_ref, yb_ref, yc_ref, ga_ref, gb_ref, gc_ref, wa_ref, wb_ref, wc_ref, o_ref):
    def branch(y_ref, w_ref, gate_ref):
        return _sigmoid(gate_ref[...]) * jnp.dot(y_ref[...], w_ref[...], preferred_element_type=F32)

    merged = branch(ya_ref, wa_ref, ga_ref) + branch(yb_ref, wb_ref, gb_ref) + branch(yc_ref, wc_ref, gc_ref)
    o_ref[...] = merged.astype(o_ref.dtype)


def _merge(y_a, y_b, y_c, p_g, wb_a, wb_b, wb_c):
    n_tok = y_a.shape[0]
    tm = min(MERGE_ROWS, n_tok)
    tn = MERGE_COLS
    nj = D_MODEL // tn

    def rows(width):
        return pl.BlockSpec((tm, width), lambda i, j: (i, 0))

    def gate(branch):
        return pl.BlockSpec((tm, tn), lambda i, j: (i, branch * nj + j))

    def weight(depth):
        return pl.BlockSpec((depth, tn), lambda i, j: (0, j))

    return pl.pallas_call(
        _merge_kernel,
        grid=(n_tok // tm, nj),
        in_specs=[rows(A_Q), rows(B_WIDTH), rows(C_OUT), gate(0), gate(1), gate(2),
                  weight(A_Q), weight(B_WIDTH), weight(C_OUT)],
        out_specs=pl.BlockSpec((tm, tn), lambda i, j: (i, j)),
        out_shape=jax.ShapeDtypeStruct((n_tok, D_MODEL), BF16),
        compiler_params=_params(2, 32),
        name="gated_merge",
    )(y_a, y_b, y_c, p_g, p_g, p_g, wb_a, wb_b, wb_c)


def _first_lane_of(mask, lane):
    return jnp.min(jnp.where(mask, lane, float(ROUTER_COLS)), axis=-1, keepdims=True)


def _route_rows(logits):
    lane = lax.broadcasted_iota(jnp.int32, logits.shape, 1).astype(F32)
    grp_logit = jnp.where(lane < N_GROUPS, logits, -jnp.inf)
    e = jnp.exp(grp_logit - jnp.max(grp_logit, axis=-1, keepdims=True))
    prob = e / jnp.sum(e, axis=-1, keepdims=True)
    grp_w = jnp.max(prob, axis=-1, keepdims=True)
    grp = _first_lane_of(prob == grp_w, lane)
    first = N_GROUPS + grp * EXPERTS_PER_GROUP
    cand = jnp.where((lane >= first) & (lane < first + EXPERTS_PER_GROUP), logits, -jnp.inf)
    top1 = jnp.max(cand, axis=-1, keepdims=True)
    lane1 = _first_lane_of(cand == top1, lane)
    cand = jnp.where(lane == lane1, -jnp.inf, cand)
    top2 = jnp.max(cand, axis=-1, keepdims=True)
    lane2 = _first_lane_of(cand == top2, lane)
    e2 = jnp.exp(top2 - top1)
    gate1 = grp_w / (1.0 + e2)
    gate2 = grp_w * e2 / (1.0 + e2)
    return jnp.where(lane == 0, gate1,
                     jnp.where(lane == 1, gate2,
                               jnp.where(lane == 2, lane1 - N_GROUPS,
                                         jnp.where(lane == 3, lane2 - N_GROUPS, 0.0))))


def _out_proj_kernel(m_ref, x_ref, w_ref, g_ref, b_ref, wr_ref, br_ref, x_o, route_o, *, alpha):
    z = alpha * x_ref[...] + jnp.dot(m_ref[...], w_ref[...], preferred_element_type=F32)
    x1 = _layer_norm(z, g_ref[...], b_ref[...])
    x_o[...] = x1
    x_hi = x1.astype(BF16)
    x_lo = (x1 - x_hi.astype(F32)).astype(BF16)
    by_hi = jnp.dot(x_hi, wr_ref[...], preferred_element_type=F32)
    by_lo = jnp.dot(x_lo, wr_ref[:, :ROUTER_COLS], preferred_element_type=F32)
    logits = by_hi[:, :ROUTER_COLS] + by_hi[:, ROUTER_COLS:] + by_lo + br_ref[...]
    route_o[...] = _route_rows(logits)


def _out_proj(merged, x, w_out, ln_g, ln_b, w_router, b_router, alpha):
    n_tok = x.shape[0]
    tm = min(OUT_ROWS, n_tok)
    rows = lambda width: pl.BlockSpec((tm, width), lambda i: (i, 0))
    full = lambda shape: pl.BlockSpec(shape, lambda i: (0, 0))
    return pl.pallas_call(
        functools.partial(_out_proj_kernel, alpha=alpha),
        grid=(n_tok // tm,),
        in_specs=[rows(D_MODEL), rows(D_MODEL), full((D_MODEL, D_MODEL)), full((1, D_MODEL)),
                  full((1, D_MODEL)), full((D_MODEL, 2 * ROUTER_COLS)), full((1, ROUTER_COLS))],
        out_specs=[rows(D_MODEL), rows(ROUTER_COLS)],
        out_shape=[jax.ShapeDtypeStruct((n_tok, D_MODEL), F32),
                   jax.ShapeDtypeStruct((n_tok, ROUTER_COLS), F32)],
        compiler_params=_params(1, 48),
        name="out_proj_ln_router",
    )(merged, x, w_out, ln_g.reshape(1, D_MODEL), ln_b.reshape(1, D_MODEL), w_router, b_router)


def _dispatch(expert):
    n_tok = expert.shape[0]
    flat_e = expert.reshape(-1)
    n_asg = flat_e.shape[0]
    order = jnp.argsort(flat_e)
    rank = jnp.argsort(order)
    counts = jnp.bincount(flat_e, length=N_EXPERTS)
    padded = (counts + MOE_ROWS - 1) // MOE_ROWS * MOE_ROWS
    pad_end = jnp.cumsum(padded)
    pad_start = pad_end - padded
    start = jnp.cumsum(counts) - counts
    dest = (pad_start - start)[flat_e] + rank
    n_rows = (n_asg + MOE_ROWS - 1) // MOE_ROWS * MOE_ROWS + N_EXPERTS * MOE_ROWS
    n_blk = n_rows // MOE_ROWS
    blk_expert = jnp.minimum(jnp.searchsorted(pad_end, jnp.arange(n_blk) * MOE_ROWS, side='right'),
                             N_EXPERTS - 1).astype(jnp.int32)
    row_expert = jnp.repeat(blk_expert, MOE_ROWS)
    offset = jnp.arange(n_rows) - pad_start[row_expert]
    source = order[jnp.clip(start[row_expert] + offset, 0, n_asg - 1)] // TOP_K
    row_tok = jnp.where(offset < counts[row_expert], source, 0).astype(jnp.int32)
    return dest.reshape(n_tok, TOP_K).astype(jnp.int32), row_tok, blk_expert


def _expert_kernel(blk_expert_ref, row_tok_ref, x_hbm, wg_ref, wu_ref, wd_ref, o_ref, x_buf, sem):
    i = pl.program_id(0)
    slot = i % 2

    def gather(block, into):
        for r in range(MOE_ROWS):
            tok = row_tok_ref[block * MOE_ROWS + r]
            pltpu.make_async_copy(x_hbm.at[pl.ds(tok, 1)], x_buf.at[into, pl.ds(r, 1)], sem.at[into]).start()

    @pl.when(i == 0)
    def _():
        gather(0, 0)

    @pl.when(i + 1 < pl.num_programs(0))
    def _():
        gather(i + 1, 1 - slot)

    pltpu.make_async_copy(x_hbm.at[pl.ds(0, MOE_ROWS)], x_buf.at[slot], sem.at[slot]).wait()
    x = x_buf[slot].astype(BF16)
    gate = jnp.dot(x, wg_ref[...].astype(BF16), preferred_element_type=F32)
    up = jnp.dot(x, wu_ref[...].astype(BF16), preferred_element_type=F32)
    hid = gate * _sigmoid(gate) * up
    o_ref[...] = jnp.dot(hid.astype(BF16), wd_ref[...].astype(BF16), preferred_element_type=F32)


def _experts(x, row_tok, blk_expert, w_gate, w_up, w_down, layer):
    n_rows = row_tok.shape[0]
    grid_spec = pltpu.PrefetchScalarGridSpec(
        num_scalar_prefetch=2,
        grid=(n_rows // MOE_ROWS,),
        in_specs=[pl.BlockSpec(memory_space=pl.ANY),
                  pl.BlockSpec((None, None, D_MODEL, EXPERT_FF), lambda i, e, t: (layer, e[i], 0, 0)),
                  pl.BlockSpec((None, None, D_MODEL, EXPERT_FF), lambda i, e, t: (layer, e[i], 0, 0)),
                  pl.BlockSpec((None, None, EXPERT_FF, D_MODEL), lambda i, e, t: (layer, e[i], 0, 0))],
        out_specs=pl.BlockSpec((MOE_ROWS, D_MODEL), lambda i, e, t: (i, 0)),
        scratch_shapes=[pltpu.VMEM((2, MOE_ROWS, D_MODEL), F32), pltpu.SemaphoreType.DMA((2,))],
    )
    return pl.pallas_call(
        _expert_kernel,
        grid_spec=grid_spec,
        out_shape=jax.ShapeDtypeStruct((n_rows, D_MODEL), F32),
        compiler_params=_params(1, 56),
        name="expert_ffn",
    )(blk_expert, row_tok, x, w_gate, w_up, w_down)


def _combine_kernel(x_ref, ya_ref, yb_ref, gate_ref, g_ref, b_ref, x_o, xb_o, *, alpha):
    gate = gate_ref[...]
    moe = gate[:, 0:1] * ya_ref[...] + gate[:, 1:2] * yb_ref[...]
    x2 = _layer_norm(alpha * x_ref[...] + moe, g_ref[...], b_ref[...])
    x_o[...] = x2
    xb_o[...] = x2.astype(BF16)


def _combine(x1, y_first, y_second, route, ln_g, ln_b, alpha):
    n_tok = x1.shape[0]
    tm = min(OUT_ROWS, n_tok)
    rows = lambda width: pl.BlockSpec((tm, width), lambda i: (i, 0))
    vec = pl.BlockSpec((1, D_MODEL), lambda i: (0, 0))
    return pl.pallas_call(
        functools.partial(_combine_kernel, alpha=alpha),
        grid=(n_tok // tm,),
        in_specs=[rows(D_MODEL), rows(D_MODEL), rows(D_MODEL), rows(LANES), vec, vec],
        out_specs=[rows(D_MODEL), rows(D_MODEL)],
        out_shape=[jax.ShapeDtypeStruct((n_tok, D_MODEL), F32),
                   jax.ShapeDtypeStruct((n_tok, D_MODEL), BF16)],
        compiler_params=_params(1, 32),
        name="moe_combine_ln",
    )(x1, y_first, y_second, route, ln_g.reshape(1, D_MODEL), ln_b.reshape(1, D_MODEL))


def _moe(x1, route, w_gate, w_up, w_down, layer, ln_g, ln_b, alpha):
    dest, row_tok, blk_expert = _dispatch(route[:, TOP_K:2 * TOP_K].astype(jnp.int32))
    y_rows = _experts(x1, row_tok, blk_expert, w_gate, w_up, w_down, layer)
    return _combine(x1, y_rows[dest[:, 0]], y_rows[dest[:, 1]], route, ln_g, ln_b, alpha)


def kernel(x, w_in, w_branch, w_out, mu_prev, mu_next, rwkv_w0, rwkv_w2, rwkv_a0, rwkv_a2, rwkv_g2, rwkv_k_k, rwkv_k_a, rwkv_r_k, rwkv_gn_g, rwkv_gn_b, q_norm, k_norm, rel_bias, ln1_g, ln1_b, router_group_w, router_group_b, router_expert_w, router_expert_b, w_gate, w_up, w_down, ln2_g, ln2_b):
    bsz, seq, _ = x.shape
    depth = w_in.shape[0]
    alpha = (2.0 * depth) ** 0.25
    xf = x.reshape(bsz * seq, D_MODEL)
    xb = xf.astype(BF16)
    col_b = A_COLS
    col_c = A_COLS + B_COLS
    col_g = A_COLS + B_COLS + C_COLS
    for l in range(depth):
        w_l = w_in[l]
        p_a = _matmul(xb, w_l[:, :col_b].astype(BF16), 768)
        p_b = _matmul(xb, w_l[:, col_b:col_c].astype(BF16), 1152)
        p_c = _matmul(xb, w_l[:, col_c:col_g].astype(BF16), 1152)
        p_g = _matmul(xb, w_l[:, col_g:].astype(BF16), 1024)
        y_a = _attention_a(p_a, q_norm[l], k_norm[l], bsz, seq)
        y_b = _rwkv(p_b, mu_prev[l], mu_next[l], rwkv_w0[l], rwkv_w2[l], rwkv_a0[l], rwkv_a2[l],
                    rwkv_g2[l], rwkv_k_k[l], rwkv_k_a[l], rwkv_r_k[l], rwkv_gn_g[l], rwkv_gn_b[l],
                    bsz, seq)
        y_c = _attention_c(p_c, rel_bias, bsz, seq)
        wb = w_branch[l].astype(BF16)
        merged = _merge(y_a, y_b, y_c, p_g, wb[:A_Q], wb[A_Q:A_Q + B_WIDTH], wb[A_Q + B_WIDTH:])
        w_router = jnp.pad(jnp.concatenate([router_group_w[l], router_expert_w[l]], axis=1),
                           ((0, 0), (0, ROUTER_COLS - N_GROUPS - N_EXPERTS)))
        w_router_hi = w_router.astype(BF16)
        w_router = jnp.concatenate([w_router_hi, (w_router - w_router_hi.astype(F32)).astype(BF16)], axis=1)
        b_router = jnp.pad(jnp.concatenate([router_group_b[l], router_expert_b[l]]),
                           (0, ROUTER_COLS - N_GROUPS - N_EXPERTS)).reshape(1, ROUTER_COLS)
        x1, route = _out_proj(merged, xf, w_out[l].astype(BF16), ln1_g[l], ln1_b[l],
                              w_router, b_router, alpha)
        xf, xb = _moe(x1, route, w_gate, w_up, w_down, l, ln2_g[l], ln2_b[l], alpha)
    return xf.reshape(bsz, seq, D_MODEL)
```

```python
import functools

import numpy as np
import jax
import jax.numpy as jnp
from jax import lax
from jax.experimental import pallas as pl
from jax.experimental.pallas import tpu as pltpu

F32 = jnp.float32
BF16 = jnp.bfloat16
HIGHEST = lax.Precision.HIGHEST

D_MODEL = 2048
DEPTH = 4
GRID_W = 64
NEG_INF = -1e30
LN_EPS = 1e-5

A_HEADS = 8
A_KV_HEADS = 2
A_GROUP = A_HEADS // A_KV_HEADS
HEAD_DIM = 128
ROPE_THETA = 10000.0
QK_EPS = 1e-6

B_HEAD_DIM = 64
B_WIDTH = 1024
B_HEADS = B_WIDTH // B_HEAD_DIM
LORA_PAIR = 128
GATE_LORA = 128
GN_EPS = 64e-5

C_PATTERNS = ((128, 1), (512, 4), (2048, 16))
C_GROUPS = 3
C_HEADS_PER_GROUP = 4
C_HEADS = C_GROUPS * C_HEADS_PER_GROUP
C_HALF = 64
REL_BUCKETS = 32
REL_MAX_DISTANCE = 1024

N_GROUPS = 8
EXPERTS_PER_GROUP = 8
N_EXPERTS = N_GROUPS * EXPERTS_PER_GROUP
TOP_K = 2
EXPERT_FF = 384

A_Q = A_HEADS * HEAD_DIM
A_KV = A_KV_HEADS * HEAD_DIM
A_COLS = A_Q + 2 * A_KV
B_COLS = 3 * B_WIDTH + 2 * LORA_PAIR + GATE_LORA
C_QKV = C_HEADS * HEAD_DIM
C_COLS = 3 * C_QKV
N_BRANCHES = 3
GATE_COLS = N_BRANCHES * D_MODEL
C_OUT = C_HEADS_PER_GROUP * HEAD_DIM

LANES = 128
SUBLANES = 8
MIB = 1024 * 1024

MM_ROWS = 1024
ATTN_A_Q_ROWS = 256
ATTN_A_KV_CHUNK = 1024
ATTN_C_Q = 128
ATTN_C_K = 2 * ATTN_C_Q
ATTN_C_TOKENS = ATTN_C_Q * C_PATTERNS[-1][1]
RWKV_PREP_ROWS = 256
RWKV_CHUNK = 64
RWKV_SCAN_ROWS = 128
RWKV_SCAN_LANES = 512
MERGE_ROWS = 512
MERGE_COLS = 512
OUT_ROWS = 256
MOE_ROWS = 256
ROUTER_COLS = 128


def _params(n_grid, vmem_mib):
    return pltpu.CompilerParams(dimension_semantics=("arbitrary",) * n_grid,
                                vmem_limit_bytes=vmem_mib * MIB)


def _bdot(a, b):
    return jnp.dot(a.astype(BF16), b.astype(BF16), preferred_element_type=F32)


def _bdot_nt(a, b):
    return lax.dot_general(a.astype(BF16), b.astype(BF16), (((1,), (1,)), ((), ())),
                           preferred_element_type=F32)


def _bdot_tn(a, b):
    return lax.dot_general(a.astype(BF16), b.astype(BF16), (((0,), (0,)), ((), ())),
                           preferred_element_type=F32)


def _sigmoid(x):
    return 1.0 / (1.0 + jnp.exp(-x))


def _layer_norm(z, g, b):
    mu = jnp.mean(z, axis=-1, keepdims=True)
    zc = z - mu
    var = jnp.mean(zc * zc, axis=-1, keepdims=True)
    return zc * lax.rsqrt(var + LN_EPS) * g + b


def _mm_kernel(x_ref, w_ref, o_ref):
    o_ref[...] = jnp.dot(x_ref[...], w_ref[...], preferred_element_type=F32).astype(o_ref.dtype)


def _matmul(x, w, tn, out_dtype=F32):
    m, k = x.shape
    n = w.shape[1]
    tm = min(MM_ROWS, m)
    return pl.pallas_call(
        _mm_kernel,
        grid=(m // tm, n // tn),
        in_specs=[pl.BlockSpec((tm, k), lambda i, j: (i, 0)),
                  pl.BlockSpec((k, tn), lambda i, j: (0, j))],
        out_specs=pl.BlockSpec((tm, tn), lambda i, j: (i, j)),
        out_shape=jax.ShapeDtypeStruct((m, n), out_dtype),
        compiler_params=_params(2, 48),
        name="dense_proj",
    )(x, w)


def _rope_tables(seq):
    n = HEAD_DIM // 2
    inv = ROPE_THETA ** (-jnp.arange(0, n, 2, dtype=F32) / n)
    pos = jnp.arange(seq)
    ang_r = (pos // GRID_W).astype(F32)[:, None] * inv[None, :]
    ang_c = (pos % GRID_W).astype(F32)[:, None] * inv[None, :]
    cos = jnp.concatenate([jnp.cos(ang_r)] * 2 + [jnp.cos(ang_c)] * 2, axis=-1)
    sin = jnp.concatenate([-jnp.sin(ang_r), jnp.sin(ang_r), -jnp.sin(ang_c), jnp.sin(ang_c)], axis=-1)
    return cos, sin


def _norm_rope(t, gain, cos, sin):
    t = t * lax.rsqrt(jnp.mean(t * t, axis=-1, keepdims=True) + QK_EPS) * gain
    lane = lax.broadcasted_iota(jnp.int32, t.shape, 1)
    quarter = HEAD_DIM // 4
    partner = jnp.where(lane % (2 * quarter) < quarter,
                        pltpu.roll(t, HEAD_DIM - quarter, axis=1), pltpu.roll(t, quarter, axis=1))
    return t * cos + partner * sin


def _attn_a_kernel(q_ref, k_ref, v_ref, cq_ref, sq_ref, ck_ref, sk_ref, qg_ref, kg_ref, o_ref,
                   kp_ref, vp_ref):
    @pl.when((pl.program_id(2) == 0) & (pl.program_id(3) == 0))
    def _():
        kp_ref[...] = _norm_rope(k_ref[...], kg_ref[...], ck_ref[...], sk_ref[...]).astype(BF16)
        vp_ref[:, :HEAD_DIM] = v_ref[...].astype(BF16)
        vp_ref[:, HEAD_DIM:] = jnp.ones((v_ref.shape[0], HEAD_DIM), BF16)

    q = (_norm_rope(q_ref[...], qg_ref[...], cq_ref[...], sq_ref[...]) * HEAD_DIM ** -0.5).astype(BF16)
    seq = kp_ref.shape[0]
    tk = min(ATTN_A_KV_CHUNK, seq)

    def scores(c):
        return lax.dot_general(q, kp_ref[c * tk:(c + 1) * tk, :], (((1,), (1,)), ((), ())),
                               preferred_element_type=F32)

    m = acc = None
    s_next = scores(0)
    for c in range(seq // tk):
        s = s_next
        if (c + 1) * tk < seq:
            s_next = scores(c + 1)
        m_c = jnp.max(s, axis=-1, keepdims=True)
        m_new = m_c if m is None else jnp.maximum(m, m_c)
        p = jnp.exp(s - m_new)
        pv = jnp.dot(p.astype(BF16), vp_ref[c * tk:(c + 1) * tk, :], preferred_element_type=F32)
        acc = pv if m is None else acc * jnp.exp(m - m_new) + pv
        m = m_new
    o_ref[...] = (acc[:, :HEAD_DIM] / acc[:, HEAD_DIM:]).astype(o_ref.dtype)


def _attention_a(p_a, q_gain, k_gain, bsz, seq):
    tq = min(ATTN_A_Q_ROWS, seq)
    nq = seq // tq
    cos, sin = _rope_tables(seq)
    k_col = A_HEADS
    v_col = A_HEADS + A_KV_HEADS
    q_tab = pl.BlockSpec((tq, HEAD_DIM), lambda b, kv, g, i: (i, 0))
    k_tab = pl.BlockSpec((seq, HEAD_DIM), lambda b, kv, g, i: (0, 0))
    gain = pl.BlockSpec((1, HEAD_DIM), lambda b, kv, g, i: (0, 0))
    return pl.pallas_call(
        _attn_a_kernel,
        grid=(bsz, A_KV_HEADS, A_GROUP, nq),
        in_specs=[
            pl.BlockSpec((tq, HEAD_DIM), lambda b, kv, g, i: (b * nq + i, kv * A_GROUP + g)),
            pl.BlockSpec((seq, HEAD_DIM), lambda b, kv, g, i: (b, k_col + kv)),
            pl.BlockSpec((seq, HEAD_DIM), lambda b, kv, g, i: (b, v_col + kv)),
            q_tab, q_tab, k_tab, k_tab, gain, gain,
        ],
        out_specs=pl.BlockSpec((tq, HEAD_DIM), lambda b, kv, g, i: (b * nq + i, kv * A_GROUP + g)),
        out_shape=jax.ShapeDtypeStruct((bsz * seq, A_Q), BF16),
        scratch_shapes=[pltpu.VMEM((seq, HEAD_DIM), BF16), pltpu.VMEM((seq, 2 * HEAD_DIM), BF16)],
        compiler_params=_params(4, 48),
        name="axial_gqa",
    )(p_a, p_a, p_a, cos, sin, cos, sin, q_gain.reshape(1, HEAD_DIM), k_gain.reshape(1, HEAD_DIM))


def _t5_bucket(rel):
    nb = REL_BUCKETS // 2
    max_exact = nb // 2
    n = np.abs(rel)
    large = max_exact + (np.log(np.maximum(n, 1) / max_exact) / np.log(REL_MAX_DISTANCE / max_exact)
                         * (nb - max_exact)).astype(np.int32)
    large = np.minimum(large, nb - 1)
    return (rel > 0).astype(np.int32) * nb + np.where(n < max_exact, n, large)


_C_WINDOW_OFFSETS = (0, -C_HALF, ATTN_C_Q - ATTN_C_K)


def _dilated_bias(rel_bias):
    per_group = []
    for gi, (_, dilation) in enumerate(C_PATTERNS):
        hs = slice(gi * C_HEADS_PER_GROUP, (gi + 1) * C_HEADS_PER_GROUP)
        variants = []
        for off in _C_WINDOW_OFFSETS:
            delta = off + np.arange(ATTN_C_K)[None, :] - np.arange(ATTN_C_Q)[:, None]
            vals = jnp.moveaxis(rel_bias[_t5_bucket(delta * dilation)][..., hs], -1, 0).astype(F32)
            variants.append(jnp.where((np.abs(delta) <= C_HALF)[None], vals, NEG_INF))
        per_group.append(jnp.stack(variants, axis=1))
    return jnp.stack(per_group, axis=0)


def _strided(start, size, stride):
    return pl.ds(start, size) if stride == 1 else pl.ds(start, size, stride=stride)


def _attn_c_kernel(q0, q1, q2, k0, k1, k2, v0, v1, v2, bias_ref, o_ref,
                   og0, og1, og2, ls0, ls1, ls2, *, seq):
    tile = pl.program_id(2)
    scale = HEAD_DIM ** -0.5
    groups = ((q0, k0, v0, og0, ls0), (q1, k1, v1, og1, ls1), (q2, k2, v2, og2, ls2))
    for gi, (q_ref, k_ref, v_ref, og_ref, ls_ref) in enumerate(groups):
        dil = C_PATTERNS[gi][1]
        n = seq // dil
        per_tile = ATTN_C_TOKENS // dil
        for res in range(dil):
            for blk in range(per_tile // ATTN_C_Q):
                m_local = blk * ATTN_C_Q
                m0 = tile * per_tile + m_local
                k_start = jnp.clip(m0 + _C_WINDOW_OFFSETS[1], 0, n - ATTN_C_K)
                variant = jnp.where(m0 == 0, 0, jnp.where(m0 == n - ATTN_C_Q, 2, 1))
                q_rows = _strided(m_local * dil + res, ATTN_C_Q, dil)
                k_rows = _strided(k_start * dil + res, ATTN_C_K, dil)
                s = _bdot_nt(q_ref[q_rows, :], k_ref[k_rows, :]) * scale + bias_ref[gi, variant]
                m = jnp.max(s, axis=-1, keepdims=True)
                p = jnp.exp(s - m)
                l = jnp.sum(p, axis=-1, keepdims=True)
                o = _bdot(p, v_ref[k_rows, :])
                og_ref[q_rows, :] = o / l
                ls_ref[q_rows, :] = jnp.broadcast_to(m + jnp.log(l), (ATTN_C_Q, HEAD_DIM))
    top = jnp.maximum(jnp.maximum(ls0[...], ls1[...]), ls2[...])
    e0, e1, e2 = jnp.exp(ls0[...] - top), jnp.exp(ls1[...] - top), jnp.exp(ls2[...] - top)
    o_ref[...] = ((e0 * og0[...] + e1 * og1[...] + e2 * og2[...]) / (e0 + e1 + e2)).astype(o_ref.dtype)


def _attention_c(p_c, rel_bias, bsz, seq):
    assert seq % ATTN_C_TOKENS == 0 and seq // C_PATTERNS[-1][1] >= ATTN_C_K
    nt = seq // ATTN_C_TOKENS
    bias = _dilated_bias(rel_bias)

    def head_col(part, gi):
        return lambda b, j, t: part * C_HEADS + gi * C_HEADS_PER_GROUP + j

    def q_spec(gi):
        col = head_col(0, gi)
        return pl.BlockSpec((ATTN_C_TOKENS, HEAD_DIM), lambda b, j, t: (b * nt + t, col(b, j, t)))

    def kv_spec(part, gi):
        col = head_col(part, gi)
        return pl.BlockSpec((seq, HEAD_DIM), lambda b, j, t: (b, col(b, j, t)))

    tile_buf = pltpu.VMEM((ATTN_C_TOKENS, HEAD_DIM), F32)
    return pl.pallas_call(
        functools.partial(_attn_c_kernel, seq=seq),
        grid=(bsz, C_HEADS_PER_GROUP, nt),
        in_specs=[q_spec(0), q_spec(1), q_spec(2),
                  kv_spec(1, 0), kv_spec(1, 1), kv_spec(1, 2),
                  kv_spec(2, 0), kv_spec(2, 1), kv_spec(2, 2),
                  pl.BlockSpec((C_GROUPS, None, len(_C_WINDOW_OFFSETS), ATTN_C_Q, ATTN_C_K),
                               lambda b, j, t: (0, j, 0, 0, 0))],
        out_specs=pl.BlockSpec((ATTN_C_TOKENS, HEAD_DIM), lambda b, j, t: (b * nt + t, j)),
        out_shape=jax.ShapeDtypeStruct((bsz * seq, C_OUT), BF16),
        scratch_shapes=[tile_buf] * 6,
        compiler_params=_params(3, 56),
        name="dilated_attn",
    )(*([p_c] * 9), bias)


def _head_ones():
    r = lax.broadcasted_iota(jnp.int32, (LANES, LANES), 0) // B_HEAD_DIM
    c = lax.broadcasted_iota(jnp.int32, (LANES, LANES), 1) // B_HEAD_DIM
    return (r == c).astype(F32)


def _head_sum(x, ones):
    parts = [jnp.dot(x[:, i:i + LANES], ones, precision=HIGHEST, preferred_element_type=F32)
             for i in range(0, x.shape[1], LANES)]
    return jnp.concatenate(parts, axis=1)


def _rwkv_prep_kernel(y_ref, yp_ref, yn_ref, mup_ref, mun_ref, w0_ref, w2_ref, a0_ref, a2_ref, g2_ref,
                      kk_ref, ka_ref, rk_ref,
                      r_o, v_o, kk_o, lw0_o, lw1_o, kd0_o, kd1_o, b0_o, b1_o, g_o, bonus_o,
                      *, blocks_per_seq):
    i = pl.program_id(0)
    ts = y_ref.shape[0]
    at_start = (i % blocks_per_seq) == 0
    at_end = (i % blocks_per_seq) == blocks_per_seq - 1
    rows = lax.broadcasted_iota(jnp.int32, (ts, 1), 0)

    def mixed(c0, c1):
        y = y_ref[:, c0:c1]
        before = jnp.where(at_start, 0.0, yp_ref[SUBLANES - 1:SUBLANES, c0:c1])
        after = jnp.where(at_end, 0.0, yn_ref[0:1, c0:c1])
        prev = jnp.where(rows == 0, before, pltpu.roll(y, 1, axis=0))
        nxt = jnp.where(rows == ts - 1, after, pltpu.roll(y, ts - 1, axis=0))
        return y + mup_ref[:, c0:c1] * (prev - y) + mun_ref[:, c0:c1] * (nxt - y)

    w = B_WIDTH
    r = mixed(0, w)
    k = mixed(w, 2 * w)
    v = mixed(2 * w, 3 * w)
    hw = mixed(3 * w, 3 * w + LORA_PAIR)
    ha = mixed(3 * w + LORA_PAIR, 3 * w + 2 * LORA_PAIR)
    hg = mixed(3 * w + 2 * LORA_PAIR, 3 * w + 2 * LORA_PAIR + GATE_LORA)
    ones = _head_ones()

    r_o[...] = r
    v_o[...] = v
    g_o[...] = _bdot(_sigmoid(hg), g2_ref[...])
    kk = k * kk_ref[...]
    kk = kk * lax.rsqrt(_head_sum(kk * kk, ones) + 1e-12)
    kk_o[...] = kk
    tanh_hw = jnp.tanh(hw)
    rrk = r * rk_ref[...]
    bonus = jnp.zeros_like(r)
    for d, (lw_o, kd_o, b_o) in enumerate(((lw0_o, kd0_o, b0_o), (lw1_o, kd1_o, b1_o))):
        z = -(w0_ref[d:d + 1, :] + _bdot(tanh_hw, w2_ref[d]))
        softplus = jnp.maximum(z, 0.0) + jnp.log(1.0 + jnp.exp(-jnp.abs(z)))
        lw_o[...] = -jnp.exp(-softplus - 0.5)
        a = _sigmoid(a0_ref[d:d + 1, :] + _bdot(ha, a2_ref[d]))
        kd = k * (1.0 + (a - 1.0) * ka_ref[...])
        kd_o[...] = kd
        b_o[...] = kk * a
        bonus = bonus + rrk * kd
    bonus_o[...] = _head_sum(bonus, ones) * v


def _rwkv_prep(p_b, mu_prev, mu_next, w0, w2, a0, a2, g2, k_k, k_a, r_k, seq):
    n_tok = p_b.shape[0]
    ts = min(RWKV_PREP_ROWS, seq)
    blocks_per_seq = seq // ts
    halo = ts // SUBLANES
    n_halo = n_tok // SUBLANES
    zeros = jnp.zeros_like(w2[0])
    w2p = jnp.stack([jnp.concatenate([w2[0], zeros], 0), jnp.concatenate([zeros, w2[1]], 0)]).astype(BF16)
    a2p = jnp.stack([jnp.concatenate([a2[0], zeros], 0), jnp.concatenate([zeros, a2[1]], 0)]).astype(BF16)
    full = lambda shape: pl.BlockSpec(shape, lambda i: (0,) * len(shape))
    row_blk = pl.BlockSpec((ts, B_WIDTH), lambda i: (i, 0))
    out = jax.ShapeDtypeStruct((n_tok, B_WIDTH), F32)
    return pl.pallas_call(
        functools.partial(_rwkv_prep_kernel, blocks_per_seq=blocks_per_seq),
        grid=(n_tok // ts,),
        in_specs=[
            pl.BlockSpec((ts, B_COLS), lambda i: (i, 0)),
            pl.BlockSpec((SUBLANES, B_COLS), lambda i: (jnp.maximum(i * halo - 1, 0), 0)),
            pl.BlockSpec((SUBLANES, B_COLS), lambda i: (jnp.minimum((i + 1) * halo, n_halo - 1), 0)),
            full((1, B_COLS)), full((1, B_COLS)),
            full((2, B_WIDTH)), full((2, LORA_PAIR, B_WIDTH)),
            full((2, B_WIDTH)), full((2, LORA_PAIR, B_WIDTH)),
            full((GATE_LORA, B_WIDTH)),
            full((1, B_WIDTH)), full((1, B_WIDTH)), full((1, B_WIDTH)),
        ],
        out_specs=[row_blk] * 11,
        out_shape=[out] * 11,
        compiler_params=_params(1, 56),
        name="rwkv_prep",
    )(p_b, p_b, p_b, mu_prev.reshape(1, B_COLS), mu_next.reshape(1, B_COLS), w0, w2p, a0, a2p,
      g2.astype(BF16), k_k.reshape(1, B_WIDTH), k_a.reshape(1, B_WIDTH), r_k.reshape(1, B_WIDTH))


def _split3(x):
    hi = x.astype(BF16)
    rest = x - hi.astype(F32)
    mid = rest.astype(BF16)
    return hi, mid, (rest - mid.astype(F32)).astype(BF16)


def _rwkv_scan_kernel(r0, v0, kk0, lw0, kd0, b0, r1, v1, kk1, lw1, kd1, b1, o0_ref, o1_ref, s_ref, *, chunk):
    @pl.when(pl.program_id(2) == 0)
    def _():
        s_ref[...] = jnp.zeros_like(s_ref)

    t, w = r0.shape
    heads = w // B_HEAD_DIM
    n_chunks = t // chunk
    row = lax.broadcasted_iota(jnp.int32, (t, t), 0)
    col = lax.broadcasted_iota(jnp.int32, (t, t), 1)
    same = (row // chunk) == (col // chunk)
    eye = (row == col).astype(F32)
    head_blk = (lax.broadcasted_iota(jnp.int32, (w, w), 0) // B_HEAD_DIM
                == lax.broadcasted_iota(jnp.int32, (w, w), 1) // B_HEAD_DIM)
    chunk_of_row = lax.broadcasted_iota(jnp.int32, (t, 1), 0) // chunk

    def by_chunk(x):
        return jnp.concatenate([jnp.where(chunk_of_row == c, x, 0.0) for c in range(n_chunks)], axis=1)

    def head(x, h):
        return x[:, h * B_HEAD_DIM:(h + 1) * B_HEAD_DIM]

    dirs = ((r0, v0, kk0, lw0, kd0, b0), (r1, v1, kk1, lw1, kd1, b1))
    out_refs = (o0_ref, o1_ref)
    strict, incl, pre = [], [], []
    for d, (r_ref, v_ref, kk_ref, lw_ref, kd_ref, b_ref) in enumerate(dirs):
        earlier = (col > row) if d == 1 else (col < row)
        strict.append(same & earlier)
        incl.append(same & (earlier | (col == row)))
        lw = lw_ref[...]
        sums = jnp.dot(jnp.concatenate([incl[d], same], axis=0).astype(BF16),
                       jnp.concatenate(_split3(lw), axis=1), preferred_element_type=F32)
        sums = sums[:, :w] + sums[:, w:2 * w] + sums[:, 2 * w:]
        cum, tot = sums[:t], sums[t:]
        grow, shrink, to_end = jnp.exp(cum), jnp.exp(-cum), jnp.exp(tot - cum)
        k, b = kd_ref[...], b_ref[...]
        pre.append(dict(v=v_ref[...], rt=r_ref[...] * grow, kb=kk_ref[...] * jnp.exp(cum - lw),
                        kt=k * shrink, bt=b * shrink, kt_end=k * to_end, bt_end=b * to_end,
                        decay_end=jnp.exp(tot)))

    units = [(d, h) for d in range(2) for h in range(heads)]
    part = lambda name, u: head(pre[u[0]][name], u[1])
    cross = {u: _bdot_nt(jnp.concatenate([part("kb", u), part("rt", u)], axis=0),
                         jnp.concatenate([part("kt", u), part("bt", u)], axis=0)) for u in units}
    a_k = {u: jnp.where(strict[u[0]], cross[u][:t, :t], 0.0) for u in units}
    a_b = {u: jnp.where(strict[u[0]], cross[u][:t, t:], 0.0) for u in units}
    a_rk = {u: jnp.where(incl[u[0]], cross[u][t:, :t], 0.0) for u in units}
    a_rb = {u: jnp.where(incl[u[0]], cross[u][t:, t:], 0.0) for u in units}
    inv = {u: eye - a_b[u] for u in units}
    power = {u: _bdot(a_b[u], a_b[u]) for u in units}
    akv = {u: _bdot(a_k[u], part("v", u)) for u in units}
    span = 2
    while span < chunk:
        if 2 * span < chunk:
            both = {u: _bdot(jnp.concatenate([inv[u], power[u]], axis=0), power[u]) for u in units}
            inv = {u: inv[u] + both[u][:t] for u in units}
            power = {u: both[u][t:] for u in units}
        else:
            inv = {u: inv[u] + _bdot(inv[u], power[u]) for u in units}
        span *= 2
    sol = {u: _bdot(inv[u], jnp.concatenate([part("kb", u), akv[u]], axis=1)) for u in units}
    free = {}
    for u in units:
        v = part("v", u)
        p, q = sol[u][:, :B_HEAD_DIM], sol[u][:, B_HEAD_DIM:]
        rhs = jnp.concatenate([jnp.concatenate([v, jnp.zeros_like(v)], axis=1),
                               jnp.concatenate([-q, -p], axis=1)], axis=0)
        free[u] = _bdot(jnp.concatenate([a_rk[u], a_rb[u]], axis=1), rhs)

    def all_heads(table, d, lo):
        return jnp.concatenate([table[(d, h)][:, lo:lo + B_HEAD_DIM] for h in range(heads)], axis=1)

    read, out_free, trans, inject = [], [], [], []
    for d in range(2):
        p_all, q_all = all_heads(sol, d, 0), all_heads(sol, d, B_HEAD_DIM)
        out_free.append(all_heads(free, d, 0))
        read.append(pre[d]["rt"] + all_heads(free, d, B_HEAD_DIM))
        bt_end_c = by_chunk(pre[d]["bt_end"])
        trans.append(_bdot_tn(p_all, bt_end_c))
        inject.append(_bdot_tn(jnp.concatenate([pre[d]["v"], q_all], axis=0),
                               jnp.concatenate([by_chunk(pre[d]["kt_end"]), -bt_end_c], axis=0)))

    state = [s_ref[0], s_ref[1]]
    for i in range(n_chunks):
        for d in range(2):
            c = n_chunks - 1 - i if d == 1 else i
            rows = slice(c * chunk, (c + 1) * chunk)
            cols = slice(c * w, (c + 1) * w)
            blocks = jnp.where(head_blk, jnp.concatenate([state[d]] * heads, axis=0), 0.0)
            out_refs[d][rows, :] = _bdot_nt(read[d][rows], blocks) + out_free[d][rows]
            inj = jnp.where(head_blk, inject[d][:, cols], 0.0)
            inj = sum(inj[h * B_HEAD_DIM:(h + 1) * B_HEAD_DIM] for h in range(heads))
            state[d] = (state[d] * pre[d]["decay_end"][c * chunk:c * chunk + 1, :]
                        - _bdot(state[d], jnp.where(head_blk, trans[d][:, cols], 0.0)) + inj)
    s_ref[0] = state[0]
    s_ref[1] = state[1]


def _rwkv_scan(r, v, kk, lw0, lw1, kd0, kd1, b0, b1, bsz, seq):
    tile = min(RWKV_SCAN_ROWS, seq)
    chunk = min(RWKV_CHUNK, tile)
    nb = seq // tile
    fwd = pl.BlockSpec((tile, RWKV_SCAN_LANES), lambda b, hg, c: (b * nb + c, hg))
    bwd = pl.BlockSpec((tile, RWKV_SCAN_LANES), lambda b, hg, c: (b * nb + nb - 1 - c, hg))
    out = jax.ShapeDtypeStruct(r.shape, F32)
    return pl.pallas_call(
        functools.partial(_rwkv_scan_kernel, chunk=chunk),
        grid=(bsz, B_WIDTH // RWKV_SCAN_LANES, nb),
        in_specs=[fwd] * 6 + [bwd] * 6,
        out_specs=[fwd, bwd],
        out_shape=[out, out],
        scratch_shapes=[pltpu.VMEM((2, B_HEAD_DIM, RWKV_SCAN_LANES), F32)],
        compiler_params=_params(3, 48),
        name="rwkv_scan",
    )(r, v, kk, lw0, kd0, b0, r, v, kk, lw1, kd1, b1)


def _rwkv_post_kernel(o0_ref, o1_ref, bonus_ref, g_ref, gn_g_ref, gn_b_ref, y_ref):
    ones = _head_ones()
    o = o0_ref[...] + o1_ref[...]
    mu = _head_sum(o, ones) * (1.0 / B_HEAD_DIM)
    oc = o - mu
    var = _head_sum(oc * oc, ones) * (1.0 / B_HEAD_DIM)
    normed = oc * lax.rsqrt(var + GN_EPS) * gn_g_ref[...] + gn_b_ref[...]
    y_ref[...] = ((normed + bonus_ref[...]) * g_ref[...]).astype(y_ref.dtype)


def _rwkv_post(o0, o1, bonus, g, gn_g, gn_b):
    n_tok = o0.shape[0]
    ts = min(RWKV_PREP_ROWS, n_tok)
    row_blk = pl.BlockSpec((ts, B_WIDTH), lambda i: (i, 0))
    vec = pl.BlockSpec((1, B_WIDTH), lambda i: (0, 0))
    return pl.pallas_call(
        _rwkv_post_kernel,
        grid=(n_tok // ts,),
        in_specs=[row_blk] * 4 + [vec, vec],
        out_specs=row_blk,
        out_shape=jax.ShapeDtypeStruct((n_tok, B_WIDTH), BF16),
        compiler_params=_params(1, 32),
        name="rwkv_post",
    )(o0, o1, bonus, g, gn_g.reshape(1, B_WIDTH), gn_b.reshape(1, B_WIDTH))


def _rwkv(p_b, mu_prev, mu_next, w0, w2, a0, a2, g2, k_k, k_a, r_k, gn_g, gn_b, bsz, seq):
    r, v, kk, lw0, lw1, kd0, kd1, b0, b1, g, bonus = _rwkv_prep(
        p_b, mu_prev, mu_next, w0, w2, a0, a2, g2, k_k, k_a, r_k, seq)
    o0, o1 = _rwkv_scan(r, v, kk, lw0, lw1, kd0, kd1, b0, b1, bsz, seq)
    return _rwkv_post(o0, o1, bonus, g, gn_g, gn_b)


def _merge_kernel(ya_ref, yb_ref, yc_ref, ga_ref, gb_ref, gc_ref, wa_ref, wb_ref, wc_ref, o_ref):
    def branch(y_ref, w_ref, gate_ref):
        return _sigmoid(gate_ref[...]) * jnp.dot(y_ref[...], w_ref[...], preferred_element_type=F32)

    merged = branch(ya_ref, wa_ref, ga_ref) + branch(yb_ref, wb_ref, gb_ref) + branch(yc_ref, wc_ref, gc_ref)
    o_ref[...] = merged.astype(o_ref.dtype)


def _merge(y_a, y_b, y_c, p_g, wb_a, wb_b, wb_c):
    n_tok = y_a.shape[0]
    tm = min(MERGE_ROWS, n_tok)
    tn = MERGE_COLS
    nj = D_MODEL // tn

    def rows(width):
        return pl.BlockSpec((tm, width), lambda i, j: (i, 0))

    def gate(branch):
        return pl.BlockSpec((tm, tn), lambda i, j: (i, branch * nj + j))

    def weight(depth):
        return pl.BlockSpec((depth, tn), lambda i, j: (0, j))

    return pl.pallas_call(
        _merge_kernel,
        grid=(n_tok // tm, nj),
        in_specs=[rows(A_Q), rows(B_WIDTH), rows(C_OUT), gate(0), gate(1), gate(2),
                  weight(A_Q), weight(B_WIDTH), weight(C_OUT)],
        out_specs=pl.BlockSpec((tm, tn), lambda i, j: (i, j)),
        out_shape=jax.ShapeDtypeStruct((n_tok, D_MODEL), BF16),
        compiler_params=_params(2, 32),
        name="gated_merge",
    )(y_a, y_b, y_c, p_g, p_g, p_g, wb_a, wb_b, wb_c)


def _first_lane_of(mask, lane):
    return jnp.min(jnp.where(mask, lane, float(ROUTER_COLS)), axis=-1, keepdims=True)


def _route_rows(logits):
    lane = lax.broadcasted_iota(jnp.int32, logits.shape, 1).astype(F32)
    grp_logit = jnp.where(lane < N_GROUPS, logits, -jnp.inf)
    e = jnp.exp(grp_logit - jnp.max(grp_logit, axis=-1, keepdims=True))
    prob = e / jnp.sum(e, axis=-1, keepdims=True)
    grp_w = jnp.max(prob, axis=-1, keepdims=True)
    grp = _first_lane_of(prob == grp_w, lane)
    first = N_GROUPS + grp * EXPERTS_PER_GROUP
    cand = jnp.where((lane >= first) & (lane < first + EXPERTS_PER_GROUP), logits, -jnp.inf)
    top1 = jnp.max(cand, axis=-1, keepdims=True)
    lane1 = _first_lane_of(cand == top1, lane)
    cand = jnp.where(lane == lane1, -jnp.inf, cand)
    top2 = jnp.max(cand, axis=-1, keepdims=True)
    lane2 = _first_lane_of(cand == top2, lane)
    e2 = jnp.exp(top2 - top1)
    gate1 = grp_w / (1.0 + e2)
    gate2 = grp_w * e2 / (1.0 + e2)
    return jnp.where(lane == 0, gate1,
                     jnp.where(lane == 1, gate2,
                               jnp.where(lane == 2, lane1 - N_GROUPS,
                                         jnp.where(lane == 3, lane2 - N_GROUPS, 0.0))))


def _out_proj_kernel(m_ref, x_ref, w_ref, g_ref, b_ref, wr_ref, br_ref, x_o, route_o, *, alpha):
    z = alpha * x_ref[...] + jnp.dot(m_ref[...], w_ref[...], preferred_element_type=F32)
    x1 = _layer_norm(z, g_ref[...], b_ref[...])
    x_o[...] = x1
    x_hi = x1.astype(BF16)
    x_lo = (x1 - x_hi.astype(F32)).astype(BF16)
    by_hi = jnp.dot(x_hi, wr_ref[...], preferred_element_type=F32)
    by_lo = jnp.dot(x_lo, wr_ref[:, :ROUTER_COLS], preferred_element_type=F32)
    logits = by_hi[:, :ROUTER_COLS] + by_hi[:, ROUTER_COLS:] + by_lo + br_ref[...]
    route_o[...] = _route_rows(logits)


def _out_proj(merged, x, w_out, ln_g, ln_b, w_router, b_router, alpha):
    n_tok = x.shape[0]
    tm = min(OUT_ROWS, n_tok)
    rows = lambda width: pl.BlockSpec((tm, width), lambda i: (i, 0))
    full = lambda shape: pl.BlockSpec(shape, lambda i: (0, 0))
    return pl.pallas_call(
        functools.partial(_out_proj_kernel, alpha=alpha),
        grid=(n_tok // tm,),
        in_specs=[rows(D_MODEL), rows(D_MODEL), full((D_MODEL, D_MODEL)), full((1, D_MODEL)),
                  full((1, D_MODEL)), full((D_MODEL, 2 * ROUTER_COLS)), full((1, ROUTER_COLS))],
        out_specs=[rows(D_MODEL), rows(ROUTER_COLS)],
        out_shape=[jax.ShapeDtypeStruct((n_tok, D_MODEL), F32),
                   jax.ShapeDtypeStruct((n_tok, ROUTER_COLS), F32)],
        compiler_params=_params(1, 48),
        name="out_proj_ln_router",
    )(merged, x, w_out, ln_g.reshape(1, D_MODEL), ln_b.reshape(1, D_MODEL), w_router, b_router)


def _dispatch(expert):
    n_tok = expert.shape[0]
    flat_e = expert.reshape(-1)
    n_asg = flat_e.shape[0]
    order = jnp.argsort(flat_e)
    rank = jnp.argsort(order)
    counts = jnp.bincount(flat_e, length=N_EXPERTS)
    padded = (counts + MOE_ROWS - 1) // MOE_ROWS * MOE_ROWS
    pad_end = jnp.cumsum(padded)
    pad_start = pad_end - padded
    start = jnp.cumsum(counts) - counts
    dest = (pad_start - start)[flat_e] + rank
    n_rows = (n_asg + MOE_ROWS - 1) // MOE_ROWS * MOE_ROWS + N_EXPERTS * MOE_ROWS
    n_blk = n_rows // MOE_ROWS
    blk_expert = jnp.minimum(jnp.searchsorted(pad_end, jnp.arange(n_blk) * MOE_ROWS, side='right'),
                             N_EXPERTS - 1).astype(jnp.int32)
    row_expert = jnp.repeat(blk_expert, MOE_ROWS)
    offset = jnp.arange(n_rows) - pad_start[row_expert]
    source = order[jnp.clip(start[row_expert] + offset, 0, n_asg - 1)] // TOP_K
    row_tok = jnp.where(offset < counts[row_expert], source, 0).astype(jnp.int32)
    blk_rows = jnp.clip(counts[blk_expert] - (jnp.arange(n_blk) * MOE_ROWS - pad_start[blk_expert]),
                        0, MOE_ROWS).astype(jnp.int32)
    return dest.reshape(n_tok, TOP_K).astype(jnp.int32), row_tok, blk_expert, blk_rows


def _expert_kernel(blk_expert_ref, blk_rows_ref, row_tok_ref, x_hbm, wg_ref, wu_ref, wd_ref, o_ref, x_buf, sem):
    i = pl.program_id(0)
    slot = i % 2
    last = pl.num_programs(0) - 1

    def gather(block, into):
        def body(r, carry):
            tok = row_tok_ref[block * MOE_ROWS + r]
            pltpu.make_async_copy(x_hbm.at[pl.ds(tok, 1)], x_buf.at[into, pl.ds(r, 1)], sem.at[into]).start()
            return carry

        lax.fori_loop(0, blk_rows_ref[block], body, 0)

    def wait_rows(n, which):
        for bit in range(MOE_ROWS.bit_length()):
            size = 1 << bit

            @pl.when((n & size) != 0)
            def _():
                pltpu.make_async_copy(x_hbm.at[pl.ds(0, size)], x_buf.at[which, pl.ds(0, size)],
                                      sem.at[which]).wait()

    @pl.when(i == 0)
    def _():
        x_buf[...] = jnp.zeros_like(x_buf)
        gather(0, 0)

    @pl.when(i < last)
    def _():
        gather(i + 1, 1 - slot)

    n_rows = blk_rows_ref[i]
    wait_rows(n_rows, slot)

    @pl.when(n_rows > 0)
    def _():
        x = x_buf[slot].astype(BF16)
        gate = jnp.dot(x, wg_ref[...].astype(BF16), preferred_element_type=F32)
        up = jnp.dot(x, wu_ref[...].astype(BF16), preferred_element_type=F32)
        hid = gate * _sigmoid(gate) * up
        o_ref[...] = jnp.dot(hid.astype(BF16), wd_ref[...].astype(BF16), preferred_element_type=F32)

    @pl.when(n_rows == 0)
    def _():
        o_ref[...] = jnp.zeros_like(o_ref)


def _experts(x, row_tok, blk_expert, blk_rows, w_gate, w_up, w_down, layer):
    n_rows = row_tok.shape[0]
    grid_spec = pltpu.PrefetchScalarGridSpec(
        num_scalar_prefetch=3,
        grid=(n_rows // MOE_ROWS,),
        in_specs=[pl.BlockSpec(memory_space=pl.ANY),
                  pl.BlockSpec((None, None, D_MODEL, EXPERT_FF), lambda i, e, n, t: (layer, e[i], 0, 0)),
                  pl.BlockSpec((None, None, D_MODEL, EXPERT_FF), lambda i, e, n, t: (layer, e[i], 0, 0)),
                  pl.BlockSpec((None, None, EXPERT_FF, D_MODEL), lambda i, e, n, t: (layer, e[i], 0, 0))],
        out_specs=pl.BlockSpec((MOE_ROWS, D_MODEL), lambda i, e, n, t: (i, 0)),
        scratch_shapes=[pltpu.VMEM((2, MOE_ROWS, D_MODEL), F32), pltpu.SemaphoreType.DMA((2,))],
    )
    return pl.pallas_call(
        _expert_kernel,
        grid_spec=grid_spec,
        out_shape=jax.ShapeDtypeStruct((n_rows, D_MODEL), F32),
        compiler_params=_params(1, 56),
        name="expert_ffn",
    )(blk_expert, blk_rows, row_tok, x, w_gate, w_up, w_down)


def _combine_kernel(x_ref, ya_ref, yb_ref, gate_ref, g_ref, b_ref, x_o, xb_o, *, alpha):
    gate = gate_ref[...]
    moe = gate[:, 0:1] * ya_ref[...] + gate[:, 1:2] * yb_ref[...]
    x2 = _layer_norm(alpha * x_ref[...] + moe, g_ref[...], b_ref[...])
    x_o[...] = x2
    xb_o[...] = x2.astype(BF16)


def _combine(x1, y_first, y_second, route, ln_g, ln_b, alpha):
    n_tok = x1.shape[0]
    tm = min(OUT_ROWS, n_tok)
    rows = lambda width: pl.BlockSpec((tm, width), lambda i: (i, 0))
    vec = pl.BlockSpec((1, D_MODEL), lambda i: (0, 0))
    return pl.pallas_call(
        functools.partial(_combine_kernel, alpha=alpha),
        grid=(n_tok // tm,),
        in_specs=[rows(D_MODEL), rows(D_MODEL), rows(D_MODEL), rows(LANES), vec, vec],
        out_specs=[rows(D_MODEL), rows(D_MODEL)],
        out_shape=[jax.ShapeDtypeStruct((n_tok, D_MODEL), F32),
                   jax.ShapeDtypeStruct((n_tok, D_MODEL), BF16)],
        compiler_params=_params(1, 32),
        name="moe_combine_ln",
    )(x1, y_first, y_second, route, ln_g.reshape(1, D_MODEL), ln_b.reshape(1, D_MODEL))


def _moe(x1, route, w_gate, w_up, w_down, layer, ln_g, ln_b, alpha):
    dest, row_tok, blk_expert, blk_rows = _dispatch(route[:, TOP_K:2 * TOP_K].astype(jnp.int32))
    y_rows = _experts(x1, row_tok, blk_expert, blk_rows, w_gate, w_up, w_down, layer)
    return _combine(x1, y_rows[dest[:, 0]], y_rows[dest[:, 1]], route, ln_g, ln_b, alpha)


def kernel(x, w_in, w_branch, w_out, mu_prev, mu_next, rwkv_w0, rwkv_w2, rwkv_a0, rwkv_a2, rwkv_g2, rwkv_k_k, rwkv_k_a, rwkv_r_k, rwkv_gn_g, rwkv_gn_b, q_norm, k_norm, rel_bias, ln1_g, ln1_b, router_group_w, router_group_b, router_expert_w, router_expert_b, w_gate, w_up, w_down, ln2_g, ln2_b):
    bsz, seq, _ = x.shape
    depth = w_in.shape[0]
    alpha = (2.0 * depth) ** 0.25
    xf = x.reshape(bsz * seq, D_MODEL)
    xb = xf.astype(BF16)
    col_b = A_COLS
    col_c = A_COLS + B_COLS
    col_g = A_COLS + B_COLS + C_COLS
    for l in range(depth):
        w_l = w_in[l]
        p_a = _matmul(xb, w_l[:, :col_b].astype(BF16), 768)
        p_b = _matmul(xb, w_l[:, col_b:col_c].astype(BF16), 1152)
        p_c = _matmul(xb, w_l[:, col_c:col_g].astype(BF16), 1152)
        p_g = _matmul(xb, w_l[:, col_g:].astype(BF16), 1024)
        y_a = _attention_a(p_a, q_norm[l], k_norm[l], bsz, seq)
        y_b = _rwkv(p_b, mu_prev[l], mu_next[l], rwkv_w0[l], rwkv_w2[l], rwkv_a0[l], rwkv_a2[l],
                    rwkv_g2[l], rwkv_k_k[l], rwkv_k_a[l], rwkv_r_k[l], rwkv_gn_g[l], rwkv_gn_b[l],
                    bsz, seq)
        y_c = _attention_c(p_c, rel_bias, bsz, seq)
        wb = w_branch[l].astype(BF16)
        merged = _merge(y_a, y_b, y_c, p_g, wb[:A_Q], wb[A_Q:A_Q + B_WIDTH], wb[A_Q + B_WIDTH:])
        w_router = jnp.pad(jnp.concatenate([router_group_w[l], router_expert_w[l]], axis=1),
                           ((0, 0), (0, ROUTER_COLS - N_GROUPS - N_EXPERTS)))
        w_router_hi = w_router.astype(BF16)
        w_router = jnp.concatenate([w_router_hi, (w_router - w_router_hi.astype(F32)).astype(BF16)], axis=1)
        b_router = jnp.pad(jnp.concatenate([router_group_b[l], router_expert_b[l]]),
                           (0, ROUTER_COLS - N_GROUPS - N_EXPERTS)).reshape(1, ROUTER_COLS)
        x1, route = _out_proj(merged, xf, w_out[l].astype(BF16), ln1_g[l], ln1_b[l],
                              w_router, b_router, alpha)
        xf, xb = _moe(x1, route, w_gate, w_up, w_down, l, ln2_g[l], ln2_b[l], alpha)
    return xf.reshape(bsz, seq, D_MODEL)
```

```python
import functools

import numpy as np
import jax
import jax.numpy as jnp
from jax import lax
from jax.experimental import pallas as pl
from jax.experimental.pallas import tpu as pltpu

F32 = jnp.float32
BF16 = jnp.bfloat16
HIGHEST = lax.Precision.HIGHEST

D_MODEL = 2048
DEPTH = 4
GRID_W = 64
NEG_INF = -1e30
LN_EPS = 1e-5

A_HEADS = 8
A_KV_HEADS = 2
A_GROUP = A_HEADS // A_KV_HEADS
HEAD_DIM = 128
ROPE_THETA = 10000.0
QK_EPS = 1e-6

B_HEAD_DIM = 64
B_WIDTH = 1024
B_HEADS = B_WIDTH // B_HEAD_DIM
LORA_PAIR = 128
GATE_LORA = 128
GN_EPS = 64e-5

C_PATTERNS = ((128, 1), (512, 4), (2048, 16))
C_GROUPS = 3
C_HEADS_PER_GROUP = 4
C_HEADS = C_GROUPS * C_HEADS_PER_GROUP
C_HALF = 64
REL_BUCKETS = 32
REL_MAX_DISTANCE = 1024

N_GROUPS = 8
EXPERTS_PER_GROUP = 8
N_EXPERTS = N_GROUPS * EXPERTS_PER_GROUP
TOP_K = 2
EXPERT_FF = 384

A_Q = A_HEADS * HEAD_DIM
A_KV = A_KV_HEADS * HEAD_DIM
A_COLS = A_Q + 2 * A_KV
B_COLS = 3 * B_WIDTH + 2 * LORA_PAIR + GATE_LORA
C_QKV = C_HEADS * HEAD_DIM
C_COLS = 3 * C_QKV
N_BRANCHES = 3
GATE_COLS = N_BRANCHES * D_MODEL
C_OUT = C_HEADS_PER_GROUP * HEAD_DIM

LANES = 128
SUBLANES = 8
MIB = 1024 * 1024

MM_ROWS = 1024
ATTN_A_Q_ROWS = 512
ATTN_A_STREAMS = 2
ATTN_A_KV_CHUNK = 1024
ATTN_C_Q = 128
ATTN_C_K = 2 * ATTN_C_Q
ATTN_C_TOKENS = ATTN_C_Q * C_PATTERNS[-1][1]
RWKV_PREP_ROWS = 256
RWKV_CHUNK = 64
RWKV_SCAN_ROWS = 128
RWKV_SCAN_LANES = 512
MERGE_ROWS = 512
MERGE_COLS = 512
OUT_ROWS = 256
MOE_ROWS = 256
ROUTER_COLS = 128


def _params(n_grid, vmem_mib):
    return pltpu.CompilerParams(dimension_semantics=("arbitrary",) * n_grid,
                                vmem_limit_bytes=vmem_mib * MIB)


def _bdot(a, b):
    return jnp.dot(a.astype(BF16), b.astype(BF16), preferred_element_type=F32)


def _bdot_nt(a, b):
    return lax.dot_general(a.astype(BF16), b.astype(BF16), (((1,), (1,)), ((), ())),
                           preferred_element_type=F32)


def _bdot_tn(a, b):
    return lax.dot_general(a.astype(BF16), b.astype(BF16), (((0,), (0,)), ((), ())),
                           preferred_element_type=F32)


def _sigmoid(x):
    return 1.0 / (1.0 + jnp.exp(-x))


def _layer_norm(z, g, b):
    mu = jnp.mean(z, axis=-1, keepdims=True)
    zc = z - mu
    var = jnp.mean(zc * zc, axis=-1, keepdims=True)
    return zc * lax.rsqrt(var + LN_EPS) * g + b


def _mm_kernel(x_ref, w_ref, o_ref):
    o_ref[...] = jnp.dot(x_ref[...], w_ref[...], preferred_element_type=F32).astype(o_ref.dtype)


def _matmul(x, w, tn, out_dtype=F32):
    m, k = x.shape
    n = w.shape[1]
    tm = min(MM_ROWS, m)
    return pl.pallas_call(
        _mm_kernel,
        grid=(m // tm, n // tn),
        in_specs=[pl.BlockSpec((tm, k), lambda i, j: (i, 0)),
                  pl.BlockSpec((k, tn), lambda i, j: (0, j))],
        out_specs=pl.BlockSpec((tm, tn), lambda i, j: (i, j)),
        out_shape=jax.ShapeDtypeStruct((m, n), out_dtype),
        compiler_params=_params(2, 48),
        name="dense_proj",
    )(x, w)


def _rope_tables(seq):
    n = HEAD_DIM // 2
    inv = ROPE_THETA ** (-jnp.arange(0, n, 2, dtype=F32) / n)
    pos = jnp.arange(seq)
    ang_r = (pos // GRID_W).astype(F32)[:, None] * inv[None, :]
    ang_c = (pos % GRID_W).astype(F32)[:, None] * inv[None, :]
    cos = jnp.concatenate([jnp.cos(ang_r)] * 2 + [jnp.cos(ang_c)] * 2, axis=-1)
    sin = jnp.concatenate([-jnp.sin(ang_r), jnp.sin(ang_r), -jnp.sin(ang_c), jnp.sin(ang_c)], axis=-1)
    return cos, sin


def _norm_rope(t, gain, cos, sin):
    t = t * lax.rsqrt(jnp.mean(t * t, axis=-1, keepdims=True) + QK_EPS) * gain
    lane = lax.broadcasted_iota(jnp.int32, t.shape, 1)
    quarter = HEAD_DIM // 4
    partner = jnp.where(lane % (2 * quarter) < quarter,
                        pltpu.roll(t, HEAD_DIM - quarter, axis=1), pltpu.roll(t, quarter, axis=1))
    return t * cos + partner * sin


def _attn_a_kernel(q_ref, k_ref, v_ref, cq_ref, sq_ref, ck_ref, sk_ref, qg_ref, kg_ref, o_ref,
                   kp_ref, vp_ref):
    @pl.when((pl.program_id(2) == 0) & (pl.program_id(3) == 0))
    def _():
        kp_ref[...] = _norm_rope(k_ref[...], kg_ref[...], ck_ref[...], sk_ref[...]).astype(BF16)
        vp_ref[:, :HEAD_DIM] = v_ref[...].astype(BF16)
        vp_ref[:, HEAD_DIM:] = jnp.ones((v_ref.shape[0], HEAD_DIM), BF16)

    q = (_norm_rope(q_ref[...], qg_ref[...], cq_ref[...], sq_ref[...]) * HEAD_DIM ** -0.5).astype(BF16)
    seq = kp_ref.shape[0]
    tk = min(ATTN_A_KV_CHUNK, seq)
    rows = q.shape[0] // ATTN_A_STREAMS
    qs = [q[i * rows:(i + 1) * rows] for i in range(ATTN_A_STREAMS)]

    def scores(qi, c):
        return lax.dot_general(qi, kp_ref[c * tk:(c + 1) * tk, :], (((1,), (1,)), ((), ())),
                               preferred_element_type=F32)

    m = [None] * ATTN_A_STREAMS
    acc = [None] * ATTN_A_STREAMS
    s_next = [scores(qi, 0) for qi in qs]
    for c in range(seq // tk):
        s_now = s_next
        if (c + 1) * tk < seq:
            s_next = [scores(qi, c + 1) for qi in qs]
        for i, s in enumerate(s_now):
            m_c = jnp.max(s, axis=-1, keepdims=True)
            m_new = m_c if m[i] is None else jnp.maximum(m[i], m_c)
            p = jnp.exp(s - m_new)
            pv = jnp.dot(p.astype(BF16), vp_ref[c * tk:(c + 1) * tk, :], preferred_element_type=F32)
            acc[i] = pv if m[i] is None else acc[i] * jnp.exp(m[i] - m_new) + pv
            m[i] = m_new
    for i in range(ATTN_A_STREAMS):
        o_ref[i * rows:(i + 1) * rows, :] = (acc[i][:, :HEAD_DIM] / acc[i][:, HEAD_DIM:]).astype(o_ref.dtype)


def _attention_a(p_a, q_gain, k_gain, bsz, seq):
    tq = min(ATTN_A_Q_ROWS, seq)
    nq = seq // tq
    cos, sin = _rope_tables(seq)
    k_col = A_HEADS
    v_col = A_HEADS + A_KV_HEADS
    q_tab = pl.BlockSpec((tq, HEAD_DIM), lambda b, kv, g, i: (i, 0))
    k_tab = pl.BlockSpec((seq, HEAD_DIM), lambda b, kv, g, i: (0, 0))
    gain = pl.BlockSpec((1, HEAD_DIM), lambda b, kv, g, i: (0, 0))
    return pl.pallas_call(
        _attn_a_kernel,
        grid=(bsz, A_KV_HEADS, A_GROUP, nq),
        in_specs=[
            pl.BlockSpec((tq, HEAD_DIM), lambda b, kv, g, i: (b * nq + i, kv * A_GROUP + g)),
            pl.BlockSpec((seq, HEAD_DIM), lambda b, kv, g, i: (b, k_col + kv)),
            pl.BlockSpec((seq, HEAD_DIM), lambda b, kv, g, i: (b, v_col + kv)),
            q_tab, q_tab, k_tab, k_tab, gain, gain,
        ],
        out_specs=pl.BlockSpec((tq, HEAD_DIM), lambda b, kv, g, i: (b * nq + i, kv * A_GROUP + g)),
        out_shape=jax.ShapeDtypeStruct((bsz * seq, A_Q), BF16),
        scratch_shapes=[pltpu.VMEM((seq, HEAD_DIM), BF16), pltpu.VMEM((seq, 2 * HEAD_DIM), BF16)],
        compiler_params=_params(4, 48),
        name="axial_gqa",
    )(p_a, p_a, p_a, cos, sin, cos, sin, q_gain.reshape(1, HEAD_DIM), k_gain.reshape(1, HEAD_DIM))


def _t5_bucket(rel):
    nb = REL_BUCKETS // 2
    max_exact = nb // 2
    n = np.abs(rel)
    large = max_exact + (np.log(np.maximum(n, 1) / max_exact) / np.log(REL_MAX_DISTANCE / max_exact)
                         * (nb - max_exact)).astype(np.int32)
    large = np.minimum(large, nb - 1)
    return (rel > 0).astype(np.int32) * nb + np.where(n < max_exact, n, large)


_C_WINDOW_OFFSETS = (0, -C_HALF, ATTN_C_Q - ATTN_C_K)


def _dilated_bias(rel_bias):
    per_group = []
    for gi, (_, dilation) in enumerate(C_PATTERNS):
        hs = slice(gi * C_HEADS_PER_GROUP, (gi + 1) * C_HEADS_PER_GROUP)
        variants = []
        for off in _C_WINDOW_OFFSETS:
            delta = off + np.arange(ATTN_C_K)[None, :] - np.arange(ATTN_C_Q)[:, None]
            vals = jnp.moveaxis(rel_bias[_t5_bucket(delta * dilation)][..., hs], -1, 0).astype(F32)
            variants.append(jnp.where((np.abs(delta) <= C_HALF)[None], vals, NEG_INF))
        per_group.append(jnp.stack(variants, axis=1))
    return jnp.stack(per_group, axis=0)


def _strided(start, size, stride):
    return pl.ds(start, size) if stride == 1 else pl.ds(start, size, stride=stride)


def _attn_c_kernel(q0, q1, q2, k0, k1, k2, v0, v1, v2, bias_ref, o_ref,
                   og0, og1, og2, ls0, ls1, ls2, *, seq):
    tile = pl.program_id(2)
    scale = HEAD_DIM ** -0.5
    groups = ((q0, k0, v0, og0, ls0), (q1, k1, v1, og1, ls1), (q2, k2, v2, og2, ls2))
    for gi, (q_ref, k_ref, v_ref, og_ref, ls_ref) in enumerate(groups):
        dil = C_PATTERNS[gi][1]
        n = seq // dil
        per_tile = ATTN_C_TOKENS // dil
        for res in range(dil):
            for blk in range(per_tile // ATTN_C_Q):
                m_local = blk * ATTN_C_Q
                m0 = tile * per_tile + m_local
                k_start = jnp.clip(m0 + _C_WINDOW_OFFSETS[1], 0, n - ATTN_C_K)
                variant = jnp.where(m0 == 0, 0, jnp.where(m0 == n - ATTN_C_Q, 2, 1))
                q_rows = _strided(m_local * dil + res, ATTN_C_Q, dil)
                k_rows = _strided(k_start * dil + res, ATTN_C_K, dil)
                s = _bdot_nt(q_ref[q_rows, :], k_ref[k_rows, :]) * scale + bias_ref[gi, variant]
                m = jnp.max(s, axis=-1, keepdims=True)
                p = jnp.exp(s - m)
                l = jnp.sum(p, axis=-1, keepdims=True)
                o = _bdot(p, v_ref[k_rows, :])
                og_ref[q_rows, :] = o / l
                ls_ref[q_rows, :] = jnp.broadcast_to(m + jnp.log(l), (ATTN_C_Q, HEAD_DIM))
    top = jnp.maximum(jnp.maximum(ls0[...], ls1[...]), ls2[...])
    e0, e1, e2 = jnp.exp(ls0[...] - top), jnp.exp(ls1[...] - top), jnp.exp(ls2[...] - top)
    o_ref[...] = ((e0 * og0[...] + e1 * og1[...] + e2 * og2[...]) / (e0 + e1 + e2)).astype(o_ref.dtype)


def _attention_c(p_c, rel_bias, bsz, seq):
    assert seq % ATTN_C_TOKENS == 0 and seq // C_PATTERNS[-1][1] >= ATTN_C_K
    nt = seq // ATTN_C_TOKENS
    bias = _dilated_bias(rel_bias)

    def head_col(part, gi):
        return lambda b, j, t: part * C_HEADS + gi * C_HEADS_PER_GROUP + j

    def q_spec(gi):
        col = head_col(0, gi)
        return pl.BlockSpec((ATTN_C_TOKENS, HEAD_DIM), lambda b, j, t: (b * nt + t, col(b, j, t)))

    def kv_spec(part, gi):
        col = head_col(part, gi)
        return pl.BlockSpec((seq, HEAD_DIM), lambda b, j, t: (b, col(b, j, t)))

    tile_buf = pltpu.VMEM((ATTN_C_TOKENS, HEAD_DIM), F32)
    return pl.pallas_call(
        functools.partial(_attn_c_kernel, seq=seq),
        grid=(bsz, C_HEADS_PER_GROUP, nt),
        in_specs=[q_spec(0), q_spec(1), q_spec(2),
                  kv_spec(1, 0), kv_spec(1, 1), kv_spec(1, 2),
                  kv_spec(2, 0), kv_spec(2, 1), kv_spec(2, 2),
                  pl.BlockSpec((C_GROUPS, None, len(_C_WINDOW_OFFSETS), ATTN_C_Q, ATTN_C_K),
                               lambda b, j, t: (0, j, 0, 0, 0))],
        out_specs=pl.BlockSpec((ATTN_C_TOKENS, HEAD_DIM), lambda b, j, t: (b * nt + t, j)),
        out_shape=jax.ShapeDtypeStruct((bsz * seq, C_OUT), BF16),
        scratch_shapes=[tile_buf] * 6,
        compiler_params=_params(3, 56),
        name="dilated_attn",
    )(*([p_c] * 9), bias)


def _head_ones():
    r = lax.broadcasted_iota(jnp.int32, (LANES, LANES), 0) // B_HEAD_DIM
    c = lax.broadcasted_iota(jnp.int32, (LANES, LANES), 1) // B_HEAD_DIM
    return (r == c).astype(F32)


def _head_sum(x, ones):
    parts = [jnp.dot(x[:, i:i + LANES], ones, precision=HIGHEST, preferred_element_type=F32)
             for i in range(0, x.shape[1], LANES)]
    return jnp.concatenate(parts, axis=1)


def _rwkv_prep_kernel(y_ref, yp_ref, yn_ref, mup_ref, mun_ref, w0_ref, w2_ref, a0_ref, a2_ref, g2_ref,
                      kk_ref, ka_ref, rk_ref,
                      r_o, v_o, kk_o, lw0_o, lw1_o, kd0_o, kd1_o, b0_o, b1_o, g_o, bonus_o,
                      *, blocks_per_seq):
    i = pl.program_id(0)
    ts = y_ref.shape[0]
    at_start = (i % blocks_per_seq) == 0
    at_end = (i % blocks_per_seq) == blocks_per_seq - 1
    rows = lax.broadcasted_iota(jnp.int32, (ts, 1), 0)

    def mixed(c0, c1):
        y = y_ref[:, c0:c1]
        before = jnp.where(at_start, 0.0, yp_ref[SUBLANES - 1:SUBLANES, c0:c1])
        after = jnp.where(at_end, 0.0, yn_ref[0:1, c0:c1])
        prev = jnp.where(rows == 0, before, pltpu.roll(y, 1, axis=0))
        nxt = jnp.where(rows == ts - 1, after, pltpu.roll(y, ts - 1, axis=0))
        return y + mup_ref[:, c0:c1] * (prev - y) + mun_ref[:, c0:c1] * (nxt - y)

    w = B_WIDTH
    r = mixed(0, w)
    k = mixed(w, 2 * w)
    v = mixed(2 * w, 3 * w)
    hw = mixed(3 * w, 3 * w + LORA_PAIR)
    ha = mixed(3 * w + LORA_PAIR, 3 * w + 2 * LORA_PAIR)
    hg = mixed(3 * w + 2 * LORA_PAIR, 3 * w + 2 * LORA_PAIR + GATE_LORA)
    ones = _head_ones()

    r_o[...] = r
    v_o[...] = v
    g_o[...] = _bdot(_sigmoid(hg), g2_ref[...])
    kk = k * kk_ref[...]
    kk = kk * lax.rsqrt(_head_sum(kk * kk, ones) + 1e-12)
    kk_o[...] = kk
    tanh_hw = jnp.tanh(hw)
    rrk = r * rk_ref[...]
    bonus = jnp.zeros_like(r)
    for d, (lw_o, kd_o, b_o) in enumerate(((lw0_o, kd0_o, b0_o), (lw1_o, kd1_o, b1_o))):
        z = -(w0_ref[d:d + 1, :] + _bdot(tanh_hw, w2_ref[d]))
        softplus = jnp.maximum(z, 0.0) + jnp.log(1.0 + jnp.exp(-jnp.abs(z)))
        lw_o[...] = -jnp.exp(-softplus - 0.5)
        a = _sigmoid(a0_ref[d:d + 1, :] + _bdot(ha, a2_ref[d]))
        kd = k * (1.0 + (a - 1.0) * ka_ref[...])
        kd_o[...] = kd
        b_o[...] = kk * a
        bonus = bonus + rrk * kd
    bonus_o[...] = _head_sum(bonus, ones) * v


def _rwkv_prep(p_b, mu_prev, mu_next, w0, w2, a0, a2, g2, k_k, k_a, r_k, seq):
    n_tok = p_b.shape[0]
    ts = min(RWKV_PREP_ROWS, seq)
    blocks_per_seq = seq // ts
    halo = ts // SUBLANES
    n_halo = n_tok // SUBLANES
    zeros = jnp.zeros_like(w2[0])
    w2p = jnp.stack([jnp.concatenate([w2[0], zeros], 0), jnp.concatenate([zeros, w2[1]], 0)]).astype(BF16)
    a2p = jnp.stack([jnp.concatenate([a2[0], zeros], 0), jnp.concatenate([zeros, a2[1]], 0)]).astype(BF16)
    full = lambda shape: pl.BlockSpec(shape, lambda i: (0,) * len(shape))
    row_blk = pl.BlockSpec((ts, B_WIDTH), lambda i: (i, 0))
    out = jax.ShapeDtypeStruct((n_tok, B_WIDTH), F32)
    return pl.pallas_call(
        functools.partial(_rwkv_prep_kernel, blocks_per_seq=blocks_per_seq),
        grid=(n_tok // ts,),
        in_specs=[
            pl.BlockSpec((ts, B_COLS), lambda i: (i, 0)),
            pl.BlockSpec((SUBLANES, B_COLS), lambda i: (jnp.maximum(i * halo - 1, 0), 0)),
            pl.BlockSpec((SUBLANES, B_COLS), lambda i: (jnp.minimum((i + 1) * halo, n_halo - 1), 0)),
            full((1, B_COLS)), full((1, B_COLS)),
            full((2, B_WIDTH)), full((2, LORA_PAIR, B_WIDTH)),
            full((2, B_WIDTH)), full((2, LORA_PAIR, B_WIDTH)),
            full((GATE_LORA, B_WIDTH)),
            full((1, B_WIDTH)), full((1, B_WIDTH)), full((1, B_WIDTH)),
        ],
        out_specs=[row_blk] * 11,
        out_shape=[out] * 11,
        compiler_params=_params(1, 56),
        name="rwkv_prep",
    )(p_b, p_b, p_b, mu_prev.reshape(1, B_COLS), mu_next.reshape(1, B_COLS), w0, w2p, a0, a2p,
      g2.astype(BF16), k_k.reshape(1, B_WIDTH), k_a.reshape(1, B_WIDTH), r_k.reshape(1, B_WIDTH))


def _split3(x):
    hi = x.astype(BF16)
    rest = x - hi.astype(F32)
    mid = rest.astype(BF16)
    return hi, mid, (rest - mid.astype(F32)).astype(BF16)


def _rwkv_scan_kernel(r0, v0, kk0, lw0, kd0, b0, r1, v1, kk1, lw1, kd1, b1, o0_ref, o1_ref, s_ref, *, chunk):
    @pl.when(pl.program_id(2) == 0)
    def _():
        s_ref[...] = jnp.zeros_like(s_ref)

    t, w = r0.shape
    heads = w // B_HEAD_DIM
    n_chunks = t // chunk
    row = lax.broadcasted_iota(jnp.int32, (t, t), 0)
    col = lax.broadcasted_iota(jnp.int32, (t, t), 1)
    same = (row // chunk) == (col // chunk)
    eye = (row == col).astype(F32)
    head_blk = (lax.broadcasted_iota(jnp.int32, (w, w), 0) // B_HEAD_DIM
                == lax.broadcasted_iota(jnp.int32, (w, w), 1) // B_HEAD_DIM)
    chunk_of_row = lax.broadcasted_iota(jnp.int32, (t, 1), 0) // chunk

    def by_chunk(x):
        return jnp.concatenate([jnp.where(chunk_of_row == c, x, 0.0) for c in range(n_chunks)], axis=1)

    def head(x, h):
        return x[:, h * B_HEAD_DIM:(h + 1) * B_HEAD_DIM]

    dirs = ((r0, v0, kk0, lw0, kd0, b0), (r1, v1, kk1, lw1, kd1, b1))
    out_refs = (o0_ref, o1_ref)
    strict, incl, pre = [], [], []
    for d, (r_ref, v_ref, kk_ref, lw_ref, kd_ref, b_ref) in enumerate(dirs):
        earlier = (col > row) if d == 1 else (col < row)
        strict.append(same & earlier)
        incl.append(same & (earlier | (col == row)))
        lw = lw_ref[...]
        sums = jnp.dot(jnp.concatenate([incl[d], same], axis=0).astype(BF16),
                       jnp.concatenate(_split3(lw), axis=1), preferred_element_type=F32)
        sums = sums[:, :w] + sums[:, w:2 * w] + sums[:, 2 * w:]
        cum, tot = sums[:t], sums[t:]
        grow, shrink, to_end = jnp.exp(cum), jnp.exp(-cum), jnp.exp(tot - cum)
        k, b = kd_ref[...], b_ref[...]
        pre.append(dict(v=v_ref[...], rt=r_ref[...] * grow, kb=kk_ref[...] * jnp.exp(cum - lw),
                        kt=k * shrink, bt=b * shrink, kt_end=k * to_end, bt_end=b * to_end,
                        decay_end=jnp.exp(tot)))

    units = [(d, h) for d in range(2) for h in range(heads)]
    part = lambda name, u: head(pre[u[0]][name], u[1])
    cross = {u: _bdot_nt(jnp.concatenate([part("kb", u), part("rt", u)], axis=0),
                         jnp.concatenate([part("kt", u), part("bt", u)], axis=0)) for u in units}
    a_k = {u: jnp.where(strict[u[0]], cross[u][:t, :t], 0.0) for u in units}
    a_b = {u: jnp.where(strict[u[0]], cross[u][:t, t:], 0.0) for u in units}
    a_rk = {u: jnp.where(incl[u[0]], cross[u][t:, :t], 0.0) for u in units}
    a_rb = {u: jnp.where(incl[u[0]], cross[u][t:, t:], 0.0) for u in units}
    inv = {u: eye - a_b[u] for u in units}
    power = {u: _bdot(a_b[u], a_b[u]) for u in units}
    akv = {u: _bdot(a_k[u], part("v", u)) for u in units}
    span = 2
    while span < chunk:
        if 2 * span < chunk:
            both = {u: _bdot(jnp.concatenate([inv[u], power[u]], axis=0), power[u]) for u in units}
            inv = {u: inv[u] + both[u][:t] for u in units}
            power = {u: both[u][t:] for u in units}
        else:
            inv = {u: inv[u] + _bdot(inv[u], power[u]) for u in units}
        span *= 2
    sol = {u: _bdot(inv[u], jnp.concatenate([part("kb", u), akv[u]], axis=1)) for u in units}
    free = {}
    for u in units:
        v = part("v", u)
        p, q = sol[u][:, :B_HEAD_DIM], sol[u][:, B_HEAD_DIM:]
        rhs = jnp.concatenate([jnp.concatenate([v, jnp.zeros_like(v)], axis=1),
                               jnp.concatenate([-q, -p], axis=1)], axis=0)
        free[u] = _bdot(jnp.concatenate([a_rk[u], a_rb[u]], axis=1), rhs)

    def all_heads(table, d, lo):
        return jnp.concatenate([table[(d, h)][:, lo:lo + B_HEAD_DIM] for h in range(heads)], axis=1)

    read, out_free, trans, inject = [], [], [], []
    for d in range(2):
        p_all, q_all = all_heads(sol, d, 0), all_heads(sol, d, B_HEAD_DIM)
        out_free.append(all_heads(free, d, 0))
        read.append(pre[d]["rt"] + all_heads(free, d, B_HEAD_DIM))
        bt_end_c = by_chunk(pre[d]["bt_end"])
        trans.append(_bdot_tn(p_all, bt_end_c))
        inject.append(_bdot_tn(jnp.concatenate([pre[d]["v"], q_all], axis=0),
                               jnp.concatenate([by_chunk(pre[d]["kt_end"]), -bt_end_c], axis=0)))

    state = [s_ref[0], s_ref[1]]
    for i in range(n_chunks):
        for d in range(2):
            c = n_chunks - 1 - i if d == 1 else i
            rows = slice(c * chunk, (c + 1) * chunk)
            cols = slice(c * w, (c + 1) * w)
            blocks = jnp.where(head_blk, jnp.concatenate([state[d]] * heads, axis=0), 0.0)
            out_refs[d][rows, :] = _bdot_nt(read[d][rows], blocks) + out_free[d][rows]
            inj = jnp.where(head_blk, inject[d][:, cols], 0.0)
            inj = sum(inj[h * B_HEAD_DIM:(h + 1) * B_HEAD_DIM] for h in range(heads))
            state[d] = (state[d] * pre[d]["decay_end"][c * chunk:c * chunk + 1, :]
                        - _bdot(state[d], jnp.where(head_blk, trans[d][:, cols], 0.0)) + inj)
    s_ref[0] = state[0]
    s_ref[1] = state[1]


def _rwkv_scan(r, v, kk, lw0, lw1, kd0, kd1, b0, b1, bsz, seq):
    tile = min(RWKV_SCAN_ROWS, seq)
    chunk = min(RWKV_CHUNK, tile)
    nb = seq // tile
    fwd = pl.BlockSpec((tile, RWKV_SCAN_LANES), lambda b, hg, c: (b * nb + c, hg))
    bwd = pl.BlockSpec((tile, RWKV_SCAN_LANES), lambda b, hg, c: (b * nb + nb - 1 - c, hg))
    out = jax.ShapeDtypeStruct(r.shape, F32)
    return pl.pallas_call(
        functools.partial(_rwkv_scan_kernel, chunk=chunk),
        grid=(bsz, B_WIDTH // RWKV_SCAN_LANES, nb),
        in_specs=[fwd] * 6 + [bwd] * 6,
        out_specs=[fwd, bwd],
        out_shape=[out, out],
        scratch_shapes=[pltpu.VMEM((2, B_HEAD_DIM, RWKV_SCAN_LANES), F32)],
        compiler_params=_params(3, 48),
        name="rwkv_scan",
    )(r, v, kk, lw0, kd0, b0, r, v, kk, lw1, kd1, b1)


def _rwkv_post_kernel(o0_ref, o1_ref, bonus_ref, g_ref, gn_g_ref, gn_b_ref, y_ref):
    ones = _head_ones()
    o = o0_ref[...] + o1_ref[...]
    mu = _head_sum(o, ones) * (1.0 / B_HEAD_DIM)
    oc = o - mu
    var = _head_sum(oc * oc, ones) * (1.0 / B_HEAD_DIM)
    normed = oc * lax.rsqrt(var + GN_EPS) * gn_g_ref[...] + gn_b_ref[...]
    y_ref[...] = ((normed + bonus_ref[...]) * g_ref[...]).astype(y_ref.dtype)


def _rwkv_post(o0, o1, bonus, g, gn_g, gn_b):
    n_tok = o0.shape[0]
    ts = min(RWKV_PREP_ROWS, n_tok)
    row_blk = pl.BlockSpec((ts, B_WIDTH), lambda i: (i, 0))
    vec = pl.BlockSpec((1, B_WIDTH), lambda i: (0, 0))
    return pl.pallas_call(
        _rwkv_post_kernel,
        grid=(n_tok // ts,),
        in_specs=[row_blk] * 4 + [vec, vec],
        out_specs=row_blk,
        out_shape=jax.ShapeDtypeStruct((n_tok, B_WIDTH), BF16),
        compiler_params=_params(1, 32),
        name="rwkv_post",
    )(o0, o1, bonus, g, gn_g.reshape(1, B_WIDTH), gn_b.reshape(1, B_WIDTH))


def _rwkv(p_b, mu_prev, mu_next, w0, w2, a0, a2, g2, k_k, k_a, r_k, gn_g, gn_b, bsz, seq):
    r, v, kk, lw0, lw1, kd0, kd1, b0, b1, g, bonus = _rwkv_prep(
        p_b, mu_prev, mu_next, w0, w2, a0, a2, g2, k_k, k_a, r_k, seq)
    o0, o1 = _rwkv_scan(r, v, kk, lw0, lw1, kd0, kd1, b0, b1, bsz, seq)
    return _rwkv_post(o0, o1, bonus, g, gn_g, gn_b)


def _merge_kernel(ya_ref, yb_ref, yc_ref, ga_ref, gb_ref, gc_ref, wa_ref, wb_ref, wc_ref, o_ref):
    def branch(y_ref, w_ref, gate_ref):
        return _sigmoid(gate_ref[...]) * jnp.dot(y_ref[...], w_ref[...], preferred_element_type=F32)

    merged = branch(ya_ref, wa_ref, ga_ref) + branch(yb_ref, wb_ref, gb_ref) + branch(yc_ref, wc_ref, gc_ref)
    o_ref[...] = merged.astype(o_ref.dtype)


def _merge(y_a, y_b, y_c, p_g, wb_a, wb_b, wb_c):
    n_tok = y_a.shape[0]
    tm = min(MERGE_ROWS, n_tok)
    tn = MERGE_COLS
    nj = D_MODEL // tn

    def rows(width):
        return pl.BlockSpec((tm, width), lambda i, j: (i, 0))

    def gate(branch):
        return pl.BlockSpec((tm, tn), lambda i, j: (i, branch * nj + j))

    def weight(depth):
        return pl.BlockSpec((depth, tn), lambda i, j: (0, j))

    return pl.pallas_call(
        _merge_kernel,
        grid=(n_tok // tm, nj),
        in_specs=[rows(A_Q), rows(B_WIDTH), rows(C_OUT), gate(0), gate(1), gate(2),
                  weight(A_Q), weight(B_WIDTH), weight(C_OUT)],
        out_specs=pl.BlockSpec((tm, tn), lambda i, j: (i, j)),
        out_shape=jax.ShapeDtypeStruct((n_tok, D_MODEL), BF16),
        compiler_params=_params(2, 32),
        name="gated_merge",
    )(y_a, y_b, y_c, p_g, p_g, p_g, wb_a, wb_b, wb_c)


def _first_lane_of(mask, lane):
    return jnp.min(jnp.where(mask, lane, float(ROUTER_COLS)), axis=-1, keepdims=True)


def _route_rows(logits):
    lane = lax.broadcasted_iota(jnp.int32, logits.shape, 1).astype(F32)
    grp_logit = jnp.where(lane < N_GROUPS, logits, -jnp.inf)
    e = jnp.exp(grp_logit - jnp.max(grp_logit, axis=-1, keepdims=True))
    prob = e / jnp.sum(e, axis=-1, keepdims=True)
    grp_w = jnp.max(prob, axis=-1, keepdims=True)
    grp = _first_lane_of(prob == grp_w, lane)
    first = N_GROUPS + grp * EXPERTS_PER_GROUP
    cand = jnp.where((lane >= first) & (lane < first + EXPERTS_PER_GROUP), logits, -jnp.inf)
    top1 = jnp.max(cand, axis=-1, keepdims=True)
    lane1 = _first_lane_of(cand == top1, lane)
    cand = jnp.where(lane == lane1, -jnp.inf, cand)
    top2 = jnp.max(cand, axis=-1, keepdims=True)
    lane2 = _first_lane_of(cand == top2, lane)
    e2 = jnp.exp(top2 - top1)
    gate1 = grp_w / (1.0 + e2)
    gate2 = grp_w * e2 / (1.0 + e2)
    return jnp.where(lane == 0, gate1,
                     jnp.where(lane == 1, gate2,
                               jnp.where(lane == 2, lane1 - N_GROUPS,
                                         jnp.where(lane == 3, lane2 - N_GROUPS, 0.0))))


def _out_proj_kernel(m_ref, x_ref, w_ref, g_ref, b_ref, wr_ref, br_ref, x_o, route_o, *, alpha):
    z = alpha * x_ref[...] + jnp.dot(m_ref[...], w_ref[...], preferred_element_type=F32)
    x1 = _layer_norm(z, g_ref[...], b_ref[...])
    x_o[...] = x1
    x_hi = x1.astype(BF16)
    x_lo = (x1 - x_hi.astype(F32)).astype(BF16)
    by_hi = jnp.dot(x_hi, wr_ref[...], preferred_element_type=F32)
    by_lo = jnp.dot(x_lo, wr_ref[:, :ROUTER_COLS], preferred_element_type=F32)
    logits = by_hi[:, :ROUTER_COLS] + by_hi[:, ROUTER_COLS:] + by_lo + br_ref[...]
    route_o[...] = _route_rows(logits)


def _out_proj(merged, x, w_out, ln_g, ln_b, w_router, b_router, alpha):
    n_tok = x.shape[0]
    tm = min(OUT_ROWS, n_tok)
    rows = lambda width: pl.BlockSpec((tm, width), lambda i: (i, 0))
    full = lambda shape: pl.BlockSpec(shape, lambda i: (0, 0))
    return pl.pallas_call(
        functools.partial(_out_proj_kernel, alpha=alpha),
        grid=(n_tok // tm,),
        in_specs=[rows(D_MODEL), rows(D_MODEL), full((D_MODEL, D_MODEL)), full((1, D_MODEL)),
                  full((1, D_MODEL)), full((D_MODEL, 2 * ROUTER_COLS)), full((1, ROUTER_COLS))],
        out_specs=[rows(D_MODEL), rows(ROUTER_COLS)],
        out_shape=[jax.ShapeDtypeStruct((n_tok, D_MODEL), F32),
                   jax.ShapeDtypeStruct((n_tok, ROUTER_COLS), F32)],
        compiler_params=_params(1, 48),
        name="out_proj_ln_router",
    )(merged, x, w_out, ln_g.reshape(1, D_MODEL), ln_b.reshape(1, D_MODEL), w_router, b_router)


def _dispatch(expert):
    n_tok = expert.shape[0]
    flat_e = expert.reshape(-1)
    n_asg = flat_e.shape[0]
    ids = jnp.arange(n_asg, dtype=jnp.int32)
    _, order = lax.sort((flat_e, ids), num_keys=1)
    _, rank = lax.sort((order, ids), num_keys=1)
    counts = jnp.bincount(flat_e, length=N_EXPERTS)
    padded = (counts + MOE_ROWS - 1) // MOE_ROWS * MOE_ROWS
    pad_end = jnp.cumsum(padded)
    pad_start = pad_end - padded
    start = jnp.cumsum(counts) - counts
    dest = (pad_start - start)[flat_e] + rank
    n_rows = (n_asg + MOE_ROWS - 1) // MOE_ROWS * MOE_ROWS + N_EXPERTS * MOE_ROWS
    n_blk = n_rows // MOE_ROWS
    blk_first = jnp.arange(n_blk) * MOE_ROWS
    blk_expert = jnp.minimum(jnp.searchsorted(pad_end, blk_first, side='right'), N_EXPERTS - 1)
    offset = blk_first - pad_start[blk_expert]
    blk_rows = jnp.clip(counts[blk_expert] - offset, 0, MOE_ROWS)
    blk_base = jnp.minimum(start[blk_expert] + offset, n_asg - 1)
    as_i32 = lambda a: a.astype(jnp.int32)
    return (as_i32(dest).reshape(n_tok, TOP_K), as_i32(order // TOP_K), as_i32(blk_expert), as_i32(blk_base),
            as_i32(blk_rows))


def _expert_kernel(blk_expert_ref, blk_base_ref, blk_rows_ref, tok_ref, x_hbm, wg_ref, wu_ref, wd_ref, o_ref,
                   x_buf, sem):
    i = pl.program_id(0)
    slot = i % 2
    last = pl.num_programs(0) - 1

    def gather(block, into):
        def body(r, carry):
            tok = tok_ref[blk_base_ref[block] + r]
            pltpu.make_async_copy(x_hbm.at[pl.ds(tok, 1)], x_buf.at[into, pl.ds(r, 1)], sem.at[into]).start()
            return carry

        lax.fori_loop(0, blk_rows_ref[block], body, 0)

    def wait_rows(n, which):
        for bit in range(MOE_ROWS.bit_length()):
            size = 1 << bit

            @pl.when((n & size) != 0)
            def _():
                pltpu.make_async_copy(x_hbm.at[pl.ds(0, size)], x_buf.at[which, pl.ds(0, size)],
                                      sem.at[which]).wait()

    @pl.when(i == 0)
    def _():
        x_buf[...] = jnp.zeros_like(x_buf)
        gather(0, 0)

    @pl.when(i < last)
    def _():
        gather(i + 1, 1 - slot)

    n_rows = blk_rows_ref[i]
    wait_rows(n_rows, slot)

    @pl.when(n_rows > 0)
    def _():
        x = x_buf[slot].astype(BF16)
        gate = jnp.dot(x, wg_ref[...].astype(BF16), preferred_element_type=F32)
        up = jnp.dot(x, wu_ref[...].astype(BF16), preferred_element_type=F32)
        hid = gate * _sigmoid(gate) * up
        o_ref[...] = jnp.dot(hid.astype(BF16), wd_ref[...].astype(BF16), preferred_element_type=F32)

    @pl.when(n_rows == 0)
    def _():
        o_ref[...] = jnp.zeros_like(o_ref)


def _experts(x, sorted_tok, blk_expert, blk_base, blk_rows, w_gate, w_up, w_down, layer):
    n_rows = blk_expert.shape[0] * MOE_ROWS
    grid_spec = pltpu.PrefetchScalarGridSpec(
        num_scalar_prefetch=4,
        grid=(n_rows // MOE_ROWS,),
        in_specs=[pl.BlockSpec(memory_space=pl.ANY),
                  pl.BlockSpec((None, None, D_MODEL, EXPERT_FF), lambda i, e, base, n, t: (layer, e[i], 0, 0)),
                  pl.BlockSpec((None, None, D_MODEL, EXPERT_FF), lambda i, e, base, n, t: (layer, e[i], 0, 0)),
                  pl.BlockSpec((None, None, EXPERT_FF, D_MODEL), lambda i, e, base, n, t: (layer, e[i], 0, 0))],
        out_specs=pl.BlockSpec((MOE_ROWS, D_MODEL), lambda i, e, base, n, t: (i, 0)),
        scratch_shapes=[pltpu.VMEM((2, MOE_ROWS, D_MODEL), F32), pltpu.SemaphoreType.DMA((2,))],
    )
    return pl.pallas_call(
        _expert_kernel,
        grid_spec=grid_spec,
        out_shape=jax.ShapeDtypeStruct((n_rows, D_MODEL), F32),
        compiler_params=_params(1, 56),
        name="expert_ffn",
    )(blk_expert, blk_base, blk_rows, sorted_tok, x, w_gate, w_up, w_down)


def _combine_kernel(x_ref, ya_ref, yb_ref, gate_ref, g_ref, b_ref, x_o, xb_o, *, alpha):
    gate = gate_ref[...]
    moe = gate[:, 0:1] * ya_ref[...] + gate[:, 1:2] * yb_ref[...]
    x2 = _layer_norm(alpha * x_ref[...] + moe, g_ref[...], b_ref[...])
    x_o[...] = x2
    xb_o[...] = x2.astype(BF16)


def _combine(x1, y_first, y_second, route, ln_g, ln_b, alpha):
    n_tok = x1.shape[0]
    tm = min(OUT_ROWS, n_tok)
    rows = lambda width: pl.BlockSpec((tm, width), lambda i: (i, 0))
    vec = pl.BlockSpec((1, D_MODEL), lambda i: (0, 0))
    return pl.pallas_call(
        functools.partial(_combine_kernel, alpha=alpha),
        grid=(n_tok // tm,),
        in_specs=[rows(D_MODEL), rows(D_MODEL), rows(D_MODEL), rows(LANES), vec, vec],
        out_specs=[rows(D_MODEL), rows(D_MODEL)],
        out_shape=[jax.ShapeDtypeStruct((n_tok, D_MODEL), F32),
                   jax.ShapeDtypeStruct((n_tok, D_MODEL), BF16)],
        compiler_params=_params(1, 32),
        name="moe_combine_ln",
    )(x1, y_first, y_second, route, ln_g.reshape(1, D_MODEL), ln_b.reshape(1, D_MODEL))


def _moe(x1, route, w_gate, w_up, w_down, layer, ln_g, ln_b, alpha):
    dest, sorted_tok, blk_expert, blk_base, blk_rows = _dispatch(route[:, TOP_K:2 * TOP_K].astype(jnp.int32))
    y_rows = _experts(x1, sorted_tok, blk_expert, blk_base, blk_rows, w_gate, w_up, w_down, layer)
    return _combine(x1, y_rows[dest[:, 0]], y_rows[dest[:, 1]], route, ln_g, ln_b, alpha)


def kernel(x, w_in, w_branch, w_out, mu_prev, mu_next, rwkv_w0, rwkv_w2, rwkv_a0, rwkv_a2, rwkv_g2, rwkv_k_k, rwkv_k_a, rwkv_r_k, rwkv_gn_g, rwkv_gn_b, q_norm, k_norm, rel_bias, ln1_g, ln1_b, router_group_w, router_group_b, router_expert_w, router_expert_b, w_gate, w_up, w_down, ln2_g, ln2_b):
    bsz, seq, _ = x.shape
    depth = w_in.shape[0]
    alpha = (2.0 * depth) ** 0.25
    xf = x.reshape(bsz * seq, D_MODEL)
    xb = xf.astype(BF16)
    col_b = A_COLS
    col_c = A_COLS + B_COLS
    col_g = A_COLS + B_COLS + C_COLS
    for l in range(depth):
        w_l = w_in[l]
        p_a = _matmul(xb, w_l[:, :col_b].astype(BF16), 768)
        p_b = _matmul(xb, w_l[:, col_b:col_c].astype(BF16), 1152)
        p_c = _matmul(xb, w_l[:, col_c:col_g].astype(BF16), 1152)
        p_g = _matmul(xb, w_l[:, col_g:].astype(BF16), 1024)
        y_a = _attention_a(p_a, q_norm[l], k_norm[l], bsz, seq)
        y_b = _rwkv(p_b, mu_prev[l], mu_next[l], rwkv_w0[l], rwkv_w2[l], rwkv_a0[l], rwkv_a2[l],
                    rwkv_g2[l], rwkv_k_k[l], rwkv_k_a[l], rwkv_r_k[l], rwkv_gn_g[l], rwkv_gn_b[l],
                    bsz, seq)
        y_c = _attention_c(p_c, rel_bias, bsz, seq)
        wb = w_branch[l].astype(BF16)
        merged = _merge(y_a, y_b, y_c, p_g, wb[:A_Q], wb[A_Q:A_Q + B_WIDTH], wb[A_Q + B_WIDTH:])
        w_router = jnp.pad(jnp.concatenate([router_group_w[l], router_expert_w[l]], axis=1),
                           ((0, 0), (0, ROUTER_COLS - N_GROUPS - N_EXPERTS)))
        w_router_hi = w_router.astype(BF16)
        w_router = jnp.concatenate([w_router_hi, (w_router - w_router_hi.astype(F32)).astype(BF16)], axis=1)
        b_router = jnp.pad(jnp.concatenate([router_group_b[l], router_expert_b[l]]),
                           (0, ROUTER_COLS - N_GROUPS - N_EXPERTS)).reshape(1, ROUTER_COLS)
        x1, route = _out_proj(merged, xf, w_out[l].astype(BF16), ln1_g[l], ln1_b[l],
                              w_router, b_router, alpha)
        xf, xb = _moe(x1, route, w_gate, w_up, w_down, l, ln2_g[l], ln2_b[l], alpha)
    return xf.reshape(bsz, seq, D_MODEL)
```

```python
import functools

import numpy as np
import jax
import jax.numpy as jnp
from jax import lax
from jax.experimental import pallas as pl
from jax.experimental.pallas import tpu as pltpu

F32 = jnp.float32
BF16 = jnp.bfloat16
HIGHEST = lax.Precision.HIGHEST

D_MODEL = 2048
DEPTH = 4
GRID_W = 64
NEG_INF = -1e30
LN_EPS = 1e-5

A_HEADS = 8
A_KV_HEADS = 2
A_GROUP = A_HEADS // A_KV_HEADS
HEAD_DIM = 128
ROPE_THETA = 10000.0
QK_EPS = 1e-6

B_HEAD_DIM = 64
B_WIDTH = 1024
B_HEADS = B_WIDTH // B_HEAD_DIM
LORA_PAIR = 128
GATE_LORA = 128
GN_EPS = 64e-5

C_PATTERNS = ((128, 1), (512, 4), (2048, 16))
C_GROUPS = 3
C_HEADS_PER_GROUP = 4
C_HEADS = C_GROUPS * C_HEADS_PER_GROUP
C_HALF = 64
REL_BUCKETS = 32
REL_MAX_DISTANCE = 1024

N_GROUPS = 8
EXPERTS_PER_GROUP = 8
N_EXPERTS = N_GROUPS * EXPERTS_PER_GROUP
TOP_K = 2
EXPERT_FF = 384

A_Q = A_HEADS * HEAD_DIM
A_KV = A_KV_HEADS * HEAD_DIM
A_COLS = A_Q + 2 * A_KV
B_COLS = 3 * B_WIDTH + 2 * LORA_PAIR + GATE_LORA
C_QKV = C_HEADS * HEAD_DIM
C_COLS = 3 * C_QKV
N_BRANCHES = 3
GATE_COLS = N_BRANCHES * D_MODEL
C_OUT = C_HEADS_PER_GROUP * HEAD_DIM

LANES = 128
SUBLANES = 8
MIB = 1024 * 1024

MM_ROWS = 1024
ATTN_A_Q_ROWS = 512
ATTN_A_STREAMS = 2
ATTN_A_KV_CHUNK = 1024
ATTN_C_Q = 128
ATTN_C_K = 2 * ATTN_C_Q
ATTN_C_TOKENS = ATTN_C_Q * C_PATTERNS[-1][1]
RWKV_PREP_ROWS = 256
RWKV_CHUNK = 64
RWKV_SCAN_ROWS = 128
RWKV_SCAN_LANES = 512
MERGE_ROWS = 512
MERGE_COLS = 512
OUT_ROWS = 256
MOE_ROWS = 256
ROUTER_COLS = 128


def _params(n_grid, vmem_mib):
    return pltpu.CompilerParams(dimension_semantics=("arbitrary",) * n_grid,
                                vmem_limit_bytes=vmem_mib * MIB)


def _bdot(a, b):
    return jnp.dot(a.astype(BF16), b.astype(BF16), preferred_element_type=F32)


def _bdot_nt(a, b):
    return lax.dot_general(a.astype(BF16), b.astype(BF16), (((1,), (1,)), ((), ())),
                           preferred_element_type=F32)


def _bdot_tn(a, b):
    return lax.dot_general(a.astype(BF16), b.astype(BF16), (((0,), (0,)), ((), ())),
                           preferred_element_type=F32)


def _sigmoid(x):
    return 1.0 / (1.0 + jnp.exp(-x))


def _layer_norm(z, g, b):
    mu = jnp.mean(z, axis=-1, keepdims=True)
    zc = z - mu
    var = jnp.mean(zc * zc, axis=-1, keepdims=True)
    return zc * lax.rsqrt(var + LN_EPS) * g + b


def _mm_kernel(x_ref, w_ref, o_ref):
    o_ref[...] = jnp.dot(x_ref[...], w_ref[...], preferred_element_type=F32).astype(o_ref.dtype)


def _matmul(x, w_layers, layer, tn, out_dtype=F32):
    m, k = x.shape
    n = w_layers.shape[2]
    tm = min(MM_ROWS, m)
    return pl.pallas_call(
        _mm_kernel,
        grid=(m // tm, n // tn),
        in_specs=[pl.BlockSpec((tm, k), lambda i, j: (i, 0)),
                  pl.BlockSpec((None, k, tn), lambda i, j: (layer, 0, j))],
        out_specs=pl.BlockSpec((tm, tn), lambda i, j: (i, j)),
        out_shape=jax.ShapeDtypeStruct((m, n), out_dtype),
        compiler_params=_params(2, 48),
        name="dense_proj",
    )(x, w_layers)


def _rope_tables(seq):
    n = HEAD_DIM // 2
    inv = ROPE_THETA ** (-jnp.arange(0, n, 2, dtype=F32) / n)
    pos = jnp.arange(seq)
    ang_r = (pos // GRID_W).astype(F32)[:, None] * inv[None, :]
    ang_c = (pos % GRID_W).astype(F32)[:, None] * inv[None, :]
    cos = jnp.concatenate([jnp.cos(ang_r)] * 2 + [jnp.cos(ang_c)] * 2, axis=-1)
    sin = jnp.concatenate([-jnp.sin(ang_r), jnp.sin(ang_r), -jnp.sin(ang_c), jnp.sin(ang_c)], axis=-1)
    return cos, sin


def _norm_rope(t, gain, cos, sin):
    t = t * lax.rsqrt(jnp.mean(t * t, axis=-1, keepdims=True) + QK_EPS) * gain
    lane = lax.broadcasted_iota(jnp.int32, t.shape, 1)
    quarter = HEAD_DIM // 4
    partner = jnp.where(lane % (2 * quarter) < quarter,
                        pltpu.roll(t, HEAD_DIM - quarter, axis=1), pltpu.roll(t, quarter, axis=1))
    return t * cos + partner * sin


def _attn_a_kernel(q_ref, k_ref, v_ref, cq_ref, sq_ref, ck_ref, sk_ref, qg_ref, kg_ref, o_ref,
                   kp_ref, vp_ref):
    @pl.when((pl.program_id(2) == 0) & (pl.program_id(3) == 0))
    def _():
        kp_ref[...] = _norm_rope(k_ref[...], kg_ref[...], ck_ref[...], sk_ref[...]).astype(BF16)
        vp_ref[:, :HEAD_DIM] = v_ref[...].astype(BF16)
        vp_ref[:, HEAD_DIM:] = jnp.ones((v_ref.shape[0], HEAD_DIM), BF16)

    q = (_norm_rope(q_ref[...], qg_ref[...], cq_ref[...], sq_ref[...]) * HEAD_DIM ** -0.5).astype(BF16)
    seq = kp_ref.shape[0]
    tk = min(ATTN_A_KV_CHUNK, seq)
    rows = q.shape[0] // ATTN_A_STREAMS
    qs = [q[i * rows:(i + 1) * rows] for i in range(ATTN_A_STREAMS)]

    def scores(qi, c):
        return lax.dot_general(qi, kp_ref[c * tk:(c + 1) * tk, :], (((1,), (1,)), ((), ())),
                               preferred_element_type=F32)

    m = [None] * ATTN_A_STREAMS
    acc = [None] * ATTN_A_STREAMS
    s_next = [scores(qi, 0) for qi in qs]
    for c in range(seq // tk):
        s_now = s_next
        if (c + 1) * tk < seq:
            s_next = [scores(qi, c + 1) for qi in qs]
        for i, s in enumerate(s_now):
            m_c = jnp.max(s, axis=-1, keepdims=True)
            m_new = m_c if m[i] is None else jnp.maximum(m[i], m_c)
            p = jnp.exp(s - m_new)
            pv = jnp.dot(p.astype(BF16), vp_ref[c * tk:(c + 1) * tk, :], preferred_element_type=F32)
            acc[i] = pv if m[i] is None else acc[i] * jnp.exp(m[i] - m_new) + pv
            m[i] = m_new
    for i in range(ATTN_A_STREAMS):
        o_ref[i * rows:(i + 1) * rows, :] = (acc[i][:, :HEAD_DIM] / acc[i][:, HEAD_DIM:]).astype(o_ref.dtype)


def _attention_a(p_a, q_gain, k_gain, bsz, seq):
    tq = min(ATTN_A_Q_ROWS, seq)
    nq = seq // tq
    cos, sin = _rope_tables(seq)
    k_col = A_HEADS
    v_col = A_HEADS + A_KV_HEADS
    q_tab = pl.BlockSpec((tq, HEAD_DIM), lambda b, kv, g, i: (i, 0))
    k_tab = pl.BlockSpec((seq, HEAD_DIM), lambda b, kv, g, i: (0, 0))
    gain = pl.BlockSpec((1, HEAD_DIM), lambda b, kv, g, i: (0, 0))
    return pl.pallas_call(
        _attn_a_kernel,
        grid=(bsz, A_KV_HEADS, A_GROUP, nq),
        in_specs=[
            pl.BlockSpec((tq, HEAD_DIM), lambda b, kv, g, i: (b * nq + i, kv * A_GROUP + g)),
            pl.BlockSpec((seq, HEAD_DIM), lambda b, kv, g, i: (b, k_col + kv)),
            pl.BlockSpec((seq, HEAD_DIM), lambda b, kv, g, i: (b, v_col + kv)),
            q_tab, q_tab, k_tab, k_tab, gain, gain,
        ],
        out_specs=pl.BlockSpec((tq, HEAD_DIM), lambda b, kv, g, i: (b * nq + i, kv * A_GROUP + g)),
        out_shape=jax.ShapeDtypeStruct((bsz * seq, A_Q), BF16),
        scratch_shapes=[pltpu.VMEM((seq, HEAD_DIM), BF16), pltpu.VMEM((seq, 2 * HEAD_DIM), BF16)],
        compiler_params=_params(4, 48),
        name="axial_gqa",
    )(p_a, p_a, p_a, cos, sin, cos, sin, q_gain.reshape(1, HEAD_DIM), k_gain.reshape(1, HEAD_DIM))


def _t5_bucket(rel):
    nb = REL_BUCKETS // 2
    max_exact = nb // 2
    n = np.abs(rel)
    large = max_exact + (np.log(np.maximum(n, 1) / max_exact) / np.log(REL_MAX_DISTANCE / max_exact)
                         * (nb - max_exact)).astype(np.int32)
    large = np.minimum(large, nb - 1)
    return (rel > 0).astype(np.int32) * nb + np.where(n < max_exact, n, large)


_C_WINDOW_OFFSETS = (0, -C_HALF, ATTN_C_Q - ATTN_C_K)


def _dilated_bias(rel_bias):
    per_group = []
    for gi, (_, dilation) in enumerate(C_PATTERNS):
        hs = slice(gi * C_HEADS_PER_GROUP, (gi + 1) * C_HEADS_PER_GROUP)
        variants = []
        for off in _C_WINDOW_OFFSETS:
            delta = off + np.arange(ATTN_C_K)[None, :] - np.arange(ATTN_C_Q)[:, None]
            vals = jnp.moveaxis(rel_bias[_t5_bucket(delta * dilation)][..., hs], -1, 0).astype(F32)
            variants.append(jnp.where((np.abs(delta) <= C_HALF)[None], vals, NEG_INF))
        per_group.append(jnp.stack(variants, axis=1))
    return jnp.stack(per_group, axis=0)


def _strided(start, size, stride):
    return pl.ds(start, size) if stride == 1 else pl.ds(start, size, stride=stride)


def _attn_c_kernel(q0, q1, q2, k0, k1, k2, v0, v1, v2, bias_ref, o_ref,
                   og0, og1, og2, ls0, ls1, ls2, *, seq):
    tile = pl.program_id(2)
    scale = HEAD_DIM ** -0.5
    groups = ((q0, k0, v0, og0, ls0), (q1, k1, v1, og1, ls1), (q2, k2, v2, og2, ls2))
    for gi, (q_ref, k_ref, v_ref, og_ref, ls_ref) in enumerate(groups):
        dil = C_PATTERNS[gi][1]
        n = seq // dil
        per_tile = ATTN_C_TOKENS // dil
        for res in range(dil):
            for blk in range(per_tile // ATTN_C_Q):
                m_local = blk * ATTN_C_Q
                m0 = tile * per_tile + m_local
                k_start = jnp.clip(m0 + _C_WINDOW_OFFSETS[1], 0, n - ATTN_C_K)
                variant = jnp.where(m0 == 0, 0, jnp.where(m0 == n - ATTN_C_Q, 2, 1))
                q_rows = _strided(m_local * dil + res, ATTN_C_Q, dil)
                k_rows = _strided(k_start * dil + res, ATTN_C_K, dil)
                s = _bdot_nt(q_ref[q_rows, :], k_ref[k_rows, :]) * scale + bias_ref[gi, variant]
                m = jnp.max(s, axis=-1, keepdims=True)
                p = jnp.exp(s - m)
                l = jnp.sum(p, axis=-1, keepdims=True)
                o = _bdot(p, v_ref[k_rows, :])
                og_ref[q_rows, :] = o / l
                ls_ref[q_rows, :] = jnp.broadcast_to(m + jnp.log(l), (ATTN_C_Q, HEAD_DIM))
    top = jnp.maximum(jnp.maximum(ls0[...], ls1[...]), ls2[...])
    e0, e1, e2 = jnp.exp(ls0[...] - top), jnp.exp(ls1[...] - top), jnp.exp(ls2[...] - top)
    o_ref[...] = ((e0 * og0[...] + e1 * og1[...] + e2 * og2[...]) / (e0 + e1 + e2)).astype(o_ref.dtype)


def _attention_c(p_c, rel_bias, bsz, seq):
    assert seq % ATTN_C_TOKENS == 0 and seq // C_PATTERNS[-1][1] >= ATTN_C_K
    nt = seq // ATTN_C_TOKENS
    bias = _dilated_bias(rel_bias)

    def head_col(part, gi):
        return lambda b, j, t: part * C_HEADS + gi * C_HEADS_PER_GROUP + j

    def q_spec(gi):
        col = head_col(0, gi)
        return pl.BlockSpec((ATTN_C_TOKENS, HEAD_DIM), lambda b, j, t: (b * nt + t, col(b, j, t)))

    def kv_spec(part, gi):
        col = head_col(part, gi)
        return pl.BlockSpec((seq, HEAD_DIM), lambda b, j, t: (b, col(b, j, t)))

    tile_buf = pltpu.VMEM((ATTN_C_TOKENS, HEAD_DIM), F32)
    return pl.pallas_call(
        functools.partial(_attn_c_kernel, seq=seq),
        grid=(bsz, C_HEADS_PER_GROUP, nt),
        in_specs=[q_spec(0), q_spec(1), q_spec(2),
                  kv_spec(1, 0), kv_spec(1, 1), kv_spec(1, 2),
                  kv_spec(2, 0), kv_spec(2, 1), kv_spec(2, 2),
                  pl.BlockSpec((C_GROUPS, None, len(_C_WINDOW_OFFSETS), ATTN_C_Q, ATTN_C_K),
                               lambda b, j, t: (0, j, 0, 0, 0))],
        out_specs=pl.BlockSpec((ATTN_C_TOKENS, HEAD_DIM), lambda b, j, t: (b * nt + t, j)),
        out_shape=jax.ShapeDtypeStruct((bsz * seq, C_OUT), BF16),
        scratch_shapes=[tile_buf] * 6,
        compiler_params=_params(3, 56),
        name="dilated_attn",
    )(*([p_c] * 9), bias)


def _head_ones():
    r = lax.broadcasted_iota(jnp.int32, (LANES, LANES), 0) // B_HEAD_DIM
    c = lax.broadcasted_iota(jnp.int32, (LANES, LANES), 1) // B_HEAD_DIM
    return (r == c).astype(F32)


def _head_sum(x, ones):
    parts = [jnp.dot(x[:, i:i + LANES], ones, precision=HIGHEST, preferred_element_type=F32)
             for i in range(0, x.shape[1], LANES)]
    return jnp.concatenate(parts, axis=1)


def _rwkv_prep_kernel(y_ref, yp_ref, yn_ref, mup_ref, mun_ref, w0_ref, w2_ref, a0_ref, a2_ref, g2_ref,
                      kk_ref, ka_ref, rk_ref,
                      r_o, v_o, kk_o, lw0_o, lw1_o, kd0_o, kd1_o, b0_o, b1_o, g_o, bonus_o,
                      *, blocks_per_seq):
    i = pl.program_id(0)
    ts = y_ref.shape[0]
    at_start = (i % blocks_per_seq) == 0
    at_end = (i % blocks_per_seq) == blocks_per_seq - 1
    rows = lax.broadcasted_iota(jnp.int32, (ts, 1), 0)

    def mixed(c0, c1):
        y = y_ref[:, c0:c1]
        before = jnp.where(at_start, 0.0, yp_ref[SUBLANES - 1:SUBLANES, c0:c1])
        after = jnp.where(at_end, 0.0, yn_ref[0:1, c0:c1])
        prev = jnp.where(rows == 0, before, pltpu.roll(y, 1, axis=0))
        nxt = jnp.where(rows == ts - 1, after, pltpu.roll(y, ts - 1, axis=0))
        return y + mup_ref[:, c0:c1] * (prev - y) + mun_ref[:, c0:c1] * (nxt - y)

    w = B_WIDTH
    r = mixed(0, w)
    k = mixed(w, 2 * w)
    v = mixed(2 * w, 3 * w)
    hw = mixed(3 * w, 3 * w + LORA_PAIR)
    ha = mixed(3 * w + LORA_PAIR, 3 * w + 2 * LORA_PAIR)
    hg = mixed(3 * w + 2 * LORA_PAIR, 3 * w + 2 * LORA_PAIR + GATE_LORA)
    ones = _head_ones()

    r_o[...] = r
    v_o[...] = v
    g_o[...] = _bdot(_sigmoid(hg), g2_ref[...])
    kk = k * kk_ref[...]
    kk = kk * lax.rsqrt(_head_sum(kk * kk, ones) + 1e-12)
    kk_o[...] = kk
    tanh_hw = jnp.tanh(hw)
    rrk = r * rk_ref[...]
    bonus = jnp.zeros_like(r)
    for d, (lw_o, kd_o, b_o) in enumerate(((lw0_o, kd0_o, b0_o), (lw1_o, kd1_o, b1_o))):
        z = -(w0_ref[d:d + 1, :] + _bdot(tanh_hw, w2_ref[d]))
        softplus = jnp.maximum(z, 0.0) + jnp.log(1.0 + jnp.exp(-jnp.abs(z)))
        lw_o[...] = -jnp.exp(-softplus - 0.5)
        a = _sigmoid(a0_ref[d:d + 1, :] + _bdot(ha, a2_ref[d]))
        kd = k * (1.0 + (a - 1.0) * ka_ref[...])
        kd_o[...] = kd
        b_o[...] = kk * a
        bonus = bonus + rrk * kd
    bonus_o[...] = _head_sum(bonus, ones) * v


def _rwkv_prep(p_b, mu_prev, mu_next, w0, w2, a0, a2, g2, k_k, k_a, r_k, seq):
    n_tok = p_b.shape[0]
    ts = min(RWKV_PREP_ROWS, seq)
    blocks_per_seq = seq // ts
    halo = ts // SUBLANES
    n_halo = n_tok // SUBLANES
    zeros = jnp.zeros_like(w2[0])
    w2p = jnp.stack([jnp.concatenate([w2[0], zeros], 0), jnp.concatenate([zeros, w2[1]], 0)]).astype(BF16)
    a2p = jnp.stack([jnp.concatenate([a2[0], zeros], 0), jnp.concatenate([zeros, a2[1]], 0)]).astype(BF16)
    full = lambda shape: pl.BlockSpec(shape, lambda i: (0,) * len(shape))
    row_blk = pl.BlockSpec((ts, B_WIDTH), lambda i: (i, 0))
    out = jax.ShapeDtypeStruct((n_tok, B_WIDTH), F32)
    return pl.pallas_call(
        functools.partial(_rwkv_prep_kernel, blocks_per_seq=blocks_per_seq),
        grid=(n_tok // ts,),
        in_specs=[
            pl.BlockSpec((ts, B_COLS), lambda i: (i, 0)),
            pl.BlockSpec((SUBLANES, B_COLS), lambda i: (jnp.maximum(i * halo - 1, 0), 0)),
            pl.BlockSpec((SUBLANES, B_COLS), lambda i: (jnp.minimum((i + 1) * halo, n_halo - 1), 0)),
            full((1, B_COLS)), full((1, B_COLS)),
            full((2, B_WIDTH)), full((2, LORA_PAIR, B_WIDTH)),
            full((2, B_WIDTH)), full((2, LORA_PAIR, B_WIDTH)),
            full((GATE_LORA, B_WIDTH)),
            full((1, B_WIDTH)), full((1, B_WIDTH)), full((1, B_WIDTH)),
        ],
        out_specs=[row_blk] * 11,
        out_shape=[out] * 11,
        compiler_params=_params(1, 56),
        name="rwkv_prep",
    )(p_b, p_b, p_b, mu_prev.reshape(1, B_COLS), mu_next.reshape(1, B_COLS), w0, w2p, a0, a2p,
      g2.astype(BF16), k_k.reshape(1, B_WIDTH), k_a.reshape(1, B_WIDTH), r_k.reshape(1, B_WIDTH))


def _split3(x):
    hi = x.astype(BF16)
    rest = x - hi.astype(F32)
    mid = rest.astype(BF16)
    return hi, mid, (rest - mid.astype(F32)).astype(BF16)


def _rwkv_scan_kernel(r0, v0, kk0, lw0, kd0, b0, r1, v1, kk1, lw1, kd1, b1, o0_ref, o1_ref, s_ref, *, chunk):
    @pl.when(pl.program_id(2) == 0)
    def _():
        s_ref[...] = jnp.zeros_like(s_ref)

    t, w = r0.shape
    heads = w // B_HEAD_DIM
    n_chunks = t // chunk
    row = lax.broadcasted_iota(jnp.int32, (t, t), 0)
    col = lax.broadcasted_iota(jnp.int32, (t, t), 1)
    same = (row // chunk) == (col // chunk)
    eye = (row == col).astype(F32)
    head_blk = (lax.broadcasted_iota(jnp.int32, (w, w), 0) // B_HEAD_DIM
                == lax.broadcasted_iota(jnp.int32, (w, w), 1) // B_HEAD_DIM)
    chunk_of_row = lax.broadcasted_iota(jnp.int32, (t, 1), 0) // chunk

    def by_chunk(x):
        return jnp.concatenate([jnp.where(chunk_of_row == c, x, 0.0) for c in range(n_chunks)], axis=1)

    def head(x, h):
        return x[:, h * B_HEAD_DIM:(h + 1) * B_HEAD_DIM]

    dirs = ((r0, v0, kk0, lw0, kd0, b0), (r1, v1, kk1, lw1, kd1, b1))
    out_refs = (o0_ref, o1_ref)
    strict, incl, pre = [], [], []
    for d, (r_ref, v_ref, kk_ref, lw_ref, kd_ref, b_ref) in enumerate(dirs):
        earlier = (col > row) if d == 1 else (col < row)
        strict.append(same & earlier)
        incl.append(same & (earlier | (col == row)))
        lw = lw_ref[...]
        sums = jnp.dot(jnp.concatenate([incl[d], same], axis=0).astype(BF16),
                       jnp.concatenate(_split3(lw), axis=1), preferred_element_type=F32)
        sums = sums[:, :w] + sums[:, w:2 * w] + sums[:, 2 * w:]
        cum, tot = sums[:t], sums[t:]
        grow, shrink, to_end = jnp.exp(cum), jnp.exp(-cum), jnp.exp(tot - cum)
        k, b = kd_ref[...], b_ref[...]
        pre.append(dict(v=v_ref[...], rt=r_ref[...] * grow, kb=kk_ref[...] * jnp.exp(cum - lw),
                        kt=k * shrink, bt=b * shrink, kt_end=k * to_end, bt_end=b * to_end,
                        decay_end=jnp.exp(tot)))

    units = [(d, h) for d in range(2) for h in range(heads)]
    part = lambda name, u: head(pre[u[0]][name], u[1])
    cross = {u: _bdot_nt(jnp.concatenate([part("kb", u), part("rt", u)], axis=0),
                         jnp.concatenate([part("kt", u), part("bt", u)], axis=0)) for u in units}
    a_k = {u: jnp.where(strict[u[0]], cross[u][:t, :t], 0.0) for u in units}
    a_b = {u: jnp.where(strict[u[0]], cross[u][:t, t:], 0.0) for u in units}
    a_rk = {u: jnp.where(incl[u[0]], cross[u][t:, :t], 0.0) for u in units}
    a_rb = {u: jnp.where(incl[u[0]], cross[u][t:, t:], 0.0) for u in units}
    inv = {u: eye - a_b[u] for u in units}
    power = {u: _bdot(a_b[u], a_b[u]) for u in units}
    akv = {u: _bdot(a_k[u], part("v", u)) for u in units}
    span = 2
    while span < chunk:
        if 2 * span < chunk:
            both = {u: _bdot(jnp.concatenate([inv[u], power[u]], axis=0), power[u]) for u in units}
            inv = {u: inv[u] + both[u][:t] for u in units}
            power = {u: both[u][t:] for u in units}
        else:
            inv = {u: inv[u] + _bdot(inv[u], power[u]) for u in units}
        span *= 2
    sol = {u: _bdot(inv[u], jnp.concatenate([part("kb", u), akv[u]], axis=1)) for u in units}
    free = {}
    for u in units:
        v = part("v", u)
        p, q = sol[u][:, :B_HEAD_DIM], sol[u][:, B_HEAD_DIM:]
        rhs = jnp.concatenate([jnp.concatenate([v, jnp.zeros_like(v)], axis=1),
                               jnp.concatenate([-q, -p], axis=1)], axis=0)
        free[u] = _bdot(jnp.concatenate([a_rk[u], a_rb[u]], axis=1), rhs)

    def all_heads(table, d, lo):
        return jnp.concatenate([table[(d, h)][:, lo:lo + B_HEAD_DIM] for h in range(heads)], axis=1)

    read, out_free, trans, inject = [], [], [], []
    for d in range(2):
        p_all, q_all = all_heads(sol, d, 0), all_heads(sol, d, B_HEAD_DIM)
        out_free.append(all_heads(free, d, 0))
        read.append(pre[d]["rt"] + all_heads(free, d, B_HEAD_DIM))
        bt_end_c = by_chunk(pre[d]["bt_end"])
        trans.append(_bdot_tn(p_all, bt_end_c))
        inject.append(_bdot_tn(jnp.concatenate([pre[d]["v"], q_all], axis=0),
                               jnp.concatenate([by_chunk(pre[d]["kt_end"]), -bt_end_c], axis=0)))

    state = [s_ref[0], s_ref[1]]
    for i in range(n_chunks):
        for d in range(2):
            c = n_chunks - 1 - i if d == 1 else i
            rows = slice(c * chunk, (c + 1) * chunk)
            cols = slice(c * w, (c + 1) * w)
            blocks = jnp.where(head_blk, jnp.concatenate([state[d]] * heads, axis=0), 0.0)
            out_refs[d][rows, :] = _bdot_nt(read[d][rows], blocks) + out_free[d][rows]
            inj = jnp.where(head_blk, inject[d][:, cols], 0.0)
            inj = sum(inj[h * B_HEAD_DIM:(h + 1) * B_HEAD_DIM] for h in range(heads))
            state[d] = (state[d] * pre[d]["decay_end"][c * chunk:c * chunk + 1, :]
                        - _bdot(state[d], jnp.where(head_blk, trans[d][:, cols], 0.0)) + inj)
    s_ref[0] = state[0]
    s_ref[1] = state[1]


def _rwkv_scan(r, v, kk, lw0, lw1, kd0, kd1, b0, b1, bsz, seq):
    tile = min(RWKV_SCAN_ROWS, seq)
    chunk = min(RWKV_CHUNK, tile)
    nb = seq // tile
    fwd = pl.BlockSpec((tile, RWKV_SCAN_LANES), lambda b, hg, c: (b * nb + c, hg))
    bwd = pl.BlockSpec((tile, RWKV_SCAN_LANES), lambda b, hg, c: (b * nb + nb - 1 - c, hg))
    out = jax.ShapeDtypeStruct(r.shape, F32)
    return pl.pallas_call(
        functools.partial(_rwkv_scan_kernel, chunk=chunk),
        grid=(bsz, B_WIDTH // RWKV_SCAN_LANES, nb),
        in_specs=[fwd] * 6 + [bwd] * 6,
        out_specs=[fwd, bwd],
        out_shape=[out, out],
        scratch_shapes=[pltpu.VMEM((2, B_HEAD_DIM, RWKV_SCAN_LANES), F32)],
        compiler_params=_params(3, 48),
        name="rwkv_scan",
    )(r, v, kk, lw0, kd0, b0, r, v, kk, lw1, kd1, b1)


def _rwkv_post_kernel(o0_ref, o1_ref, bonus_ref, g_ref, gn_g_ref, gn_b_ref, y_ref):
    ones = _head_ones()
    o = o0_ref[...] + o1_ref[...]
    mu = _head_sum(o, ones) * (1.0 / B_HEAD_DIM)
    oc = o - mu
    var = _head_sum(oc * oc, ones) * (1.0 / B_HEAD_DIM)
    normed = oc * lax.rsqrt(var + GN_EPS) * gn_g_ref[...] + gn_b_ref[...]
    y_ref[...] = ((normed + bonus_ref[...]) * g_ref[...]).astype(y_ref.dtype)


def _rwkv_post(o0, o1, bonus, g, gn_g, gn_b):
    n_tok = o0.shape[0]
    ts = min(RWKV_PREP_ROWS, n_tok)
    row_blk = pl.BlockSpec((ts, B_WIDTH), lambda i: (i, 0))
    vec = pl.BlockSpec((1, B_WIDTH), lambda i: (0, 0))
    return pl.pallas_call(
        _rwkv_post_kernel,
        grid=(n_tok // ts,),
        in_specs=[row_blk] * 4 + [vec, vec],
        out_specs=row_blk,
        out_shape=jax.ShapeDtypeStruct((n_tok, B_WIDTH), BF16),
        compiler_params=_params(1, 32),
        name="rwkv_post",
    )(o0, o1, bonus, g, gn_g.reshape(1, B_WIDTH), gn_b.reshape(1, B_WIDTH))


def _rwkv(p_b, mu_prev, mu_next, w0, w2, a0, a2, g2, k_k, k_a, r_k, gn_g, gn_b, bsz, seq):
    r, v, kk, lw0, lw1, kd0, kd1, b0, b1, g, bonus = _rwkv_prep(
        p_b, mu_prev, mu_next, w0, w2, a0, a2, g2, k_k, k_a, r_k, seq)
    o0, o1 = _rwkv_scan(r, v, kk, lw0, lw1, kd0, kd1, b0, b1, bsz, seq)
    return _rwkv_post(o0, o1, bonus, g, gn_g, gn_b)


def _merge_kernel(ya_ref, yb_ref, yc_ref, ga_ref, gb_ref, gc_ref, wa_ref, wb_ref, wc_ref, o_ref):
    def branch(y_ref, w_ref, gate_ref):
        return _sigmoid(gate_ref[...]) * jnp.dot(y_ref[...], w_ref[...], preferred_element_type=F32)

    merged = branch(ya_ref, wa_ref, ga_ref) + branch(yb_ref, wb_ref, gb_ref) + branch(yc_ref, wc_ref, gc_ref)
    o_ref[...] = merged.astype(o_ref.dtype)


def _merge(y_a, y_b, y_c, p_g, wb_a, wb_b, wb_c):
    n_tok = y_a.shape[0]
    tm = min(MERGE_ROWS, n_tok)
    tn = MERGE_COLS
    nj = D_MODEL // tn

    def rows(width):
        return pl.BlockSpec((tm, width), lambda i, j: (i, 0))

    def gate(branch):
        return pl.BlockSpec((tm, tn), lambda i, j: (i, branch * nj + j))

    def weight(depth):
        return pl.BlockSpec((depth, tn), lambda i, j: (0, j))

    return pl.pallas_call(
        _merge_kernel,
        grid=(n_tok // tm, nj),
        in_specs=[rows(A_Q), rows(B_WIDTH), rows(C_OUT), gate(0), gate(1), gate(2),
                  weight(A_Q), weight(B_WIDTH), weight(C_OUT)],
        out_specs=pl.BlockSpec((tm, tn), lambda i, j: (i, j)),
        out_shape=jax.ShapeDtypeStruct((n_tok, D_MODEL), BF16),
        compiler_params=_params(2, 32),
        name="gated_merge",
    )(y_a, y_b, y_c, p_g, p_g, p_g, wb_a, wb_b, wb_c)


def _first_lane_of(mask, lane):
    return jnp.min(jnp.where(mask, lane, float(ROUTER_COLS)), axis=-1, keepdims=True)


def _route_rows(logits):
    lane = lax.broadcasted_iota(jnp.int32, logits.shape, 1).astype(F32)
    grp_logit = jnp.where(lane < N_GROUPS, logits, -jnp.inf)
    e = jnp.exp(grp_logit - jnp.max(grp_logit, axis=-1, keepdims=True))
    prob = e / jnp.sum(e, axis=-1, keepdims=True)
    grp_w = jnp.max(prob, axis=-1, keepdims=True)
    grp = _first_lane_of(prob == grp_w, lane)
    first = N_GROUPS + grp * EXPERTS_PER_GROUP
    cand = jnp.where((lane >= first) & (lane < first + EXPERTS_PER_GROUP), logits, -jnp.inf)
    top1 = jnp.max(cand, axis=-1, keepdims=True)
    lane1 = _first_lane_of(cand == top1, lane)
    cand = jnp.where(lane == lane1, -jnp.inf, cand)
    top2 = jnp.max(cand, axis=-1, keepdims=True)
    lane2 = _first_lane_of(cand == top2, lane)
    e2 = jnp.exp(top2 - top1)
    gate1 = grp_w / (1.0 + e2)
    gate2 = grp_w * e2 / (1.0 + e2)
    return jnp.where(lane == 0, gate1,
                     jnp.where(lane == 1, gate2,
                               jnp.where(lane == 2, lane1 - N_GROUPS,
                                         jnp.where(lane == 3, lane2 - N_GROUPS, 0.0))))


def _out_proj_kernel(m_ref, x_ref, w_ref, g_ref, b_ref, wr_ref, br_ref, x_o, route_o, *, alpha):
    z = alpha * x_ref[...] + jnp.dot(m_ref[...], w_ref[...], preferred_element_type=F32)
    x1 = _layer_norm(z, g_ref[...], b_ref[...])
    x_o[...] = x1
    x_hi = x1.astype(BF16)
    x_lo = (x1 - x_hi.astype(F32)).astype(BF16)
    by_hi = jnp.dot(x_hi, wr_ref[...], preferred_element_type=F32)
    by_lo = jnp.dot(x_lo, wr_ref[:, :ROUTER_COLS], preferred_element_type=F32)
    logits = by_hi[:, :ROUTER_COLS] + by_hi[:, ROUTER_COLS:] + by_lo + br_ref[...]
    route_o[...] = _route_rows(logits)


def _out_proj(merged, x, w_out, ln_g, ln_b, w_router, b_router, alpha):
    n_tok = x.shape[0]
    tm = min(OUT_ROWS, n_tok)
    rows = lambda width: pl.BlockSpec((tm, width), lambda i: (i, 0))
    full = lambda shape: pl.BlockSpec(shape, lambda i: (0, 0))
    return pl.pallas_call(
        functools.partial(_out_proj_kernel, alpha=alpha),
        grid=(n_tok // tm,),
        in_specs=[rows(D_MODEL), rows(D_MODEL), full((D_MODEL, D_MODEL)), full((1, D_MODEL)),
                  full((1, D_MODEL)), full((D_MODEL, 2 * ROUTER_COLS)), full((1, ROUTER_COLS))],
        out_specs=[rows(D_MODEL), rows(ROUTER_COLS)],
        out_shape=[jax.ShapeDtypeStruct((n_tok, D_MODEL), F32),
                   jax.ShapeDtypeStruct((n_tok, ROUTER_COLS), F32)],
        compiler_params=_params(1, 48),
        name="out_proj_ln_router",
    )(merged, x, w_out, ln_g.reshape(1, D_MODEL), ln_b.reshape(1, D_MODEL), w_router, b_router)


def _dispatch(expert):
    n_tok = expert.shape[0]
    flat_e = expert.reshape(-1)
    n_asg = flat_e.shape[0]
    ids = jnp.arange(n_asg, dtype=jnp.int32)
    _, order = lax.sort((flat_e, ids), num_keys=1)
    _, rank = lax.sort((order, ids), num_keys=1)
    counts = jnp.bincount(flat_e, length=N_EXPERTS)
    padded = (counts + MOE_ROWS - 1) // MOE_ROWS * MOE_ROWS
    pad_end = jnp.cumsum(padded)
    pad_start = pad_end - padded
    start = jnp.cumsum(counts) - counts
    dest = (pad_start - start)[flat_e] + rank
    n_rows = (n_asg + MOE_ROWS - 1) // MOE_ROWS * MOE_ROWS + N_EXPERTS * MOE_ROWS
    n_blk = n_rows // MOE_ROWS
    blk_first = jnp.arange(n_blk) * MOE_ROWS
    blk_expert = jnp.minimum(jnp.searchsorted(pad_end, blk_first, side='right'), N_EXPERTS - 1)
    offset = blk_first - pad_start[blk_expert]
    blk_rows = jnp.clip(counts[blk_expert] - offset, 0, MOE_ROWS)
    blk_base = jnp.minimum(start[blk_expert] + offset, n_asg - 1)
    as_i32 = lambda a: a.astype(jnp.int32)
    return (as_i32(dest).reshape(n_tok, TOP_K), as_i32(order // TOP_K), as_i32(blk_expert), as_i32(blk_base),
            as_i32(blk_rows))


def _expert_kernel(blk_expert_ref, blk_base_ref, blk_rows_ref, tok_ref, x_hbm, wg_ref, wu_ref, wd_ref, o_ref,
                   x_buf, sem):
    i = pl.program_id(0)
    slot = i % 2
    last = pl.num_programs(0) - 1

    def gather(block, into):
        def body(r, carry):
            tok = tok_ref[blk_base_ref[block] + r]
            pltpu.make_async_copy(x_hbm.at[pl.ds(tok, 1)], x_buf.at[into, pl.ds(r, 1)], sem.at[into]).start()
            return carry

        lax.fori_loop(0, blk_rows_ref[block], body, 0)

    def wait_rows(n, which):
        for bit in range(MOE_ROWS.bit_length()):
            size = 1 << bit

            @pl.when((n & size) != 0)
            def _():
                pltpu.make_async_copy(x_hbm.at[pl.ds(0, size)], x_buf.at[which, pl.ds(0, size)],
                                      sem.at[which]).wait()

    @pl.when(i == 0)
    def _():
        x_buf[...] = jnp.zeros_like(x_buf)
        gather(0, 0)

    @pl.when(i < last)
    def _():
        gather(i + 1, 1 - slot)

    n_rows = blk_rows_ref[i]
    wait_rows(n_rows, slot)

    @pl.when(n_rows > 0)
    def _():
        x = x_buf[slot].astype(BF16)
        gate = jnp.dot(x, wg_ref[...].astype(BF16), preferred_element_type=F32)
        up = jnp.dot(x, wu_ref[...].astype(BF16), preferred_element_type=F32)
        hid = gate * _sigmoid(gate) * up
        o_ref[...] = jnp.dot(hid.astype(BF16), wd_ref[...].astype(BF16), preferred_element_type=F32)

    @pl.when(n_rows == 0)
    def _():
        o_ref[...] = jnp.zeros_like(o_ref)


def _experts(x, sorted_tok, blk_expert, blk_base, blk_rows, w_gate, w_up, w_down, layer):
    n_rows = blk_expert.shape[0] * MOE_ROWS
    grid_spec = pltpu.PrefetchScalarGridSpec(
        num_scalar_prefetch=4,
        grid=(n_rows // MOE_ROWS,),
        in_specs=[pl.BlockSpec(memory_space=pl.ANY),
                  pl.BlockSpec((None, None, D_MODEL, EXPERT_FF), lambda i, e, base, n, t: (layer, e[i], 0, 0)),
                  pl.BlockSpec((None, None, D_MODEL, EXPERT_FF), lambda i, e, base, n, t: (layer, e[i], 0, 0)),
                  pl.BlockSpec((None, None, EXPERT_FF, D_MODEL), lambda i, e, base, n, t: (layer, e[i], 0, 0))],
        out_specs=pl.BlockSpec((MOE_ROWS, D_MODEL), lambda i, e, base, n, t: (i, 0)),
        scratch_shapes=[pltpu.VMEM((2, MOE_ROWS, D_MODEL), F32), pltpu.SemaphoreType.DMA((2,))],
    )
    return pl.pallas_call(
        _expert_kernel,
        grid_spec=grid_spec,
        out_shape=jax.ShapeDtypeStruct((n_rows, D_MODEL), F32),
        compiler_params=_params(1, 56),
        name="expert_ffn",
    )(blk_expert, blk_base, blk_rows, sorted_tok, x, w_gate, w_up, w_down)


def _combine_kernel(x_ref, ya_ref, yb_ref, gate_ref, g_ref, b_ref, x_o, xb_o, *, alpha):
    gate = gate_ref[...]
    moe = gate[:, 0:1] * ya_ref[...] + gate[:, 1:2] * yb_ref[...]
    x2 = _layer_norm(alpha * x_ref[...] + moe, g_ref[...], b_ref[...])
    x_o[...] = x2
    xb_o[...] = x2.astype(BF16)


def _combine(x1, y_first, y_second, route, ln_g, ln_b, alpha):
    n_tok = x1.shape[0]
    tm = min(OUT_ROWS, n_tok)
    rows = lambda width: pl.BlockSpec((tm, width), lambda i: (i, 0))
    vec = pl.BlockSpec((1, D_MODEL), lambda i: (0, 0))
    return pl.pallas_call(
        functools.partial(_combine_kernel, alpha=alpha),
        grid=(n_tok // tm,),
        in_specs=[rows(D_MODEL), rows(D_MODEL), rows(D_MODEL), rows(LANES), vec, vec],
        out_specs=[rows(D_MODEL), rows(D_MODEL)],
        out_shape=[jax.ShapeDtypeStruct((n_tok, D_MODEL), F32),
                   jax.ShapeDtypeStruct((n_tok, D_MODEL), BF16)],
        compiler_params=_params(1, 32),
        name="moe_combine_ln",
    )(x1, y_first, y_second, route, ln_g.reshape(1, D_MODEL), ln_b.reshape(1, D_MODEL))


def _moe(x1, route, w_gate, w_up, w_down, layer, ln_g, ln_b, alpha):
    dest, sorted_tok, blk_expert, blk_base, blk_rows = _dispatch(route[:, TOP_K:2 * TOP_K].astype(jnp.int32))
    y_rows = _experts(x1, sorted_tok, blk_expert, blk_base, blk_rows, w_gate, w_up, w_down, layer)
    return _combine(x1, y_rows[dest[:, 0]], y_rows[dest[:, 1]], route, ln_g, ln_b, alpha)


def kernel(x, w_in, w_branch, w_out, mu_prev, mu_next, rwkv_w0, rwkv_w2, rwkv_a0, rwkv_a2, rwkv_g2, rwkv_k_k, rwkv_k_a, rwkv_r_k, rwkv_gn_g, rwkv_gn_b, q_norm, k_norm, rel_bias, ln1_g, ln1_b, router_group_w, router_group_b, router_expert_w, router_expert_b, w_gate, w_up, w_down, ln2_g, ln2_b):
    bsz, seq, _ = x.shape
    depth = w_in.shape[0]
    alpha = (2.0 * depth) ** 0.25
    xf = x.reshape(bsz * seq, D_MODEL)
    xb = xf.astype(BF16)
    col_b = A_COLS
    col_c = A_COLS + B_COLS
    col_g = A_COLS + B_COLS + C_COLS
    w_a = w_in[:, :, :col_b].astype(BF16)
    w_b = w_in[:, :, col_b:col_c].astype(BF16)
    w_c = w_in[:, :, col_c:col_g].astype(BF16)
    w_g = w_in[:, :, col_g:].astype(BF16)
    for l in range(depth):
        p_a = _matmul(xb, w_a, l, 768)
        p_b = _matmul(xb, w_b, l, 1152)
        p_c = _matmul(xb, w_c, l, 1152)
        p_g = _matmul(xb, w_g, l, 1024)
        y_a = _attention_a(p_a, q_norm[l], k_norm[l], bsz, seq)
        y_b = _rwkv(p_b, mu_prev[l], mu_next[l], rwkv_w0[l], rwkv_w2[l], rwkv_a0[l], rwkv_a2[l],
                    rwkv_g2[l], rwkv_k_k[l], rwkv_k_a[l], rwkv_r_k[l], rwkv_gn_g[l], rwkv_gn_b[l],
                    bsz, seq)
        y_c = _attention_c(p_c, rel_bias, bsz, seq)
        wb = w_branch[l].astype(BF16)
        merged = _merge(y_a, y_b, y_c, p_g, wb[:A_Q], wb[A_Q:A_Q + B_WIDTH], wb[A_Q + B_WIDTH:])
        w_router = jnp.pad(jnp.concatenate([router_group_w[l], router_expert_w[l]], axis=1),
                           ((0, 0), (0, ROUTER_COLS - N_GROUPS - N_EXPERTS)))
        w_router_hi = w_router.astype(BF16)
        w_router = jnp.concatenate([w_router_hi, (w_router - w_router_hi.astype(F32)).astype(BF16)], axis=1)
        b_router = jnp.pad(jnp.concatenate([router_group_b[l], router_expert_b[l]]),
                           (0, ROUTER_COLS - N_GROUPS - N_EXPERTS)).reshape(1, ROUTER_COLS)
        x1, route = _out_proj(merged, xf, w_out[l].astype(BF16), ln1_g[l], ln1_b[l],
                              w_router, b_router, alpha)
        xf, xb = _moe(x1, route, w_gate, w_up, w_down, l, ln2_g[l], ln2_b[l], alpha)
    return xf.reshape(bsz, seq, D_MODEL)
```

```python
import functools

import numpy as np
import jax
import jax.numpy as jnp
from jax import lax
from jax.experimental import pallas as pl
from jax.experimental.pallas import tpu as pltpu

F32 = jnp.float32
BF16 = jnp.bfloat16
HIGHEST = lax.Precision.HIGHEST

D_MODEL = 2048
DEPTH = 4
GRID_W = 64
NEG_INF = -1e30
LN_EPS = 1e-5

A_HEADS = 8
A_KV_HEADS = 2
A_GROUP = A_HEADS // A_KV_HEADS
HEAD_DIM = 128
ROPE_THETA = 10000.0
QK_EPS = 1e-6

B_HEAD_DIM = 64
B_WIDTH = 1024
B_HEADS = B_WIDTH // B_HEAD_DIM
LORA_PAIR = 128
GATE_LORA = 128
GN_EPS = 64e-5

C_PATTERNS = ((128, 1), (512, 4), (2048, 16))
C_GROUPS = 3
C_HEADS_PER_GROUP = 4
C_HEADS = C_GROUPS * C_HEADS_PER_GROUP
C_HALF = 64
REL_BUCKETS = 32
REL_MAX_DISTANCE = 1024

N_GROUPS = 8
EXPERTS_PER_GROUP = 8
N_EXPERTS = N_GROUPS * EXPERTS_PER_GROUP
TOP_K = 2
EXPERT_FF = 384

A_Q = A_HEADS * HEAD_DIM
A_KV = A_KV_HEADS * HEAD_DIM
A_COLS = A_Q + 2 * A_KV
B_COLS = 3 * B_WIDTH + 2 * LORA_PAIR + GATE_LORA
C_QKV = C_HEADS * HEAD_DIM
C_COLS = 3 * C_QKV
N_BRANCHES = 3
GATE_COLS = N_BRANCHES * D_MODEL
C_OUT = C_HEADS_PER_GROUP * HEAD_DIM

LANES = 128
SUBLANES = 8
MIB = 1024 * 1024

MM_ROWS = 1024
ATTN_A_Q_ROWS = 512
ATTN_A_STREAMS = 2
ATTN_A_KV_CHUNK = 1024
ATTN_C_Q = 128
ATTN_C_K = 2 * ATTN_C_Q
ATTN_C_TOKENS = ATTN_C_Q * C_PATTERNS[-1][1]
RWKV_PREP_ROWS = 256
RWKV_CHUNK = 64
RWKV_SCAN_ROWS = 128
RWKV_SCAN_LANES = 512
MERGE_ROWS = 512
MERGE_COLS = 512
OUT_ROWS = 256
MOE_ROWS = 256
ROUTER_COLS = 128


def _params(n_grid, vmem_mib):
    return pltpu.CompilerParams(dimension_semantics=("arbitrary",) * n_grid,
                                vmem_limit_bytes=vmem_mib * MIB)


def _bdot(a, b):
    return jnp.dot(a.astype(BF16), b.astype(BF16), preferred_element_type=F32)


def _bdot_nt(a, b):
    return lax.dot_general(a.astype(BF16), b.astype(BF16), (((1,), (1,)), ((), ())),
                           preferred_element_type=F32)


def _bdot_tn(a, b):
    return lax.dot_general(a.astype(BF16), b.astype(BF16), (((0,), (0,)), ((), ())),
                           preferred_element_type=F32)


def _sigmoid(x):
    return 1.0 / (1.0 + jnp.exp(-x))


def _layer_norm(z, g, b):
    mu = jnp.mean(z, axis=-1, keepdims=True)
    zc = z - mu
    var = jnp.mean(zc * zc, axis=-1, keepdims=True)
    return zc * lax.rsqrt(var + LN_EPS) * g + b


def _mm_kernel(x_ref, w_ref, o_ref):
    o_ref[...] = jnp.dot(x_ref[...], w_ref[...], preferred_element_type=F32).astype(o_ref.dtype)


def _matmul(x, w, tn, out_dtype=F32):
    m, k = x.shape
    n = w.shape[1]
    tm = min(MM_ROWS, m)
    return pl.pallas_call(
        _mm_kernel,
        grid=(m // tm, n // tn),
        in_specs=[pl.BlockSpec((tm, k), lambda i, j: (i, 0)),
                  pl.BlockSpec((k, tn), lambda i, j: (0, j))],
        out_specs=pl.BlockSpec((tm, tn), lambda i, j: (i, j)),
        out_shape=jax.ShapeDtypeStruct((m, n), out_dtype),
        compiler_params=_params(2, 48),
        name="dense_proj",
    )(x, w)


def _rope_tables(seq):
    n = HEAD_DIM // 2
    inv = ROPE_THETA ** (-jnp.arange(0, n, 2, dtype=F32) / n)
    pos = jnp.arange(seq)
    ang_r = (pos // GRID_W).astype(F32)[:, None] * inv[None, :]
    ang_c = (pos % GRID_W).astype(F32)[:, None] * inv[None, :]
    cos = jnp.concatenate([jnp.cos(ang_r)] * 2 + [jnp.cos(ang_c)] * 2, axis=-1)
    sin = jnp.concatenate([-jnp.sin(ang_r), jnp.sin(ang_r), -jnp.sin(ang_c), jnp.sin(ang_c)], axis=-1)
    return cos, sin


def _norm_rope(t, gain, cos, sin):
    t = t * lax.rsqrt(jnp.mean(t * t, axis=-1, keepdims=True) + QK_EPS) * gain
    lane = lax.broadcasted_iota(jnp.int32, t.shape, 1)
    quarter = HEAD_DIM // 4
    partner = jnp.where(lane % (2 * quarter) < quarter,
                        pltpu.roll(t, HEAD_DIM - quarter, axis=1), pltpu.roll(t, quarter, axis=1))
    return t * cos + partner * sin


def _attn_a_kernel(q_ref, k_ref, v_ref, cq_ref, sq_ref, ck_ref, sk_ref, qg_ref, kg_ref, o_ref,
                   kp_ref, vp_ref):
    @pl.when((pl.program_id(2) == 0) & (pl.program_id(3) == 0))
    def _():
        kp_ref[...] = _norm_rope(k_ref[...], kg_ref[...], ck_ref[...], sk_ref[...]).astype(BF16)
        vp_ref[:, :HEAD_DIM] = v_ref[...].astype(BF16)
        vp_ref[:, HEAD_DIM:] = jnp.ones((v_ref.shape[0], HEAD_DIM), BF16)

    q = (_norm_rope(q_ref[...], qg_ref[...], cq_ref[...], sq_ref[...]) * HEAD_DIM ** -0.5).astype(BF16)
    seq = kp_ref.shape[0]
    tk = min(ATTN_A_KV_CHUNK, seq)
    rows = q.shape[0] // ATTN_A_STREAMS
    qs = [q[i * rows:(i + 1) * rows] for i in range(ATTN_A_STREAMS)]

    def scores(qi, c):
        return lax.dot_general(qi, kp_ref[c * tk:(c + 1) * tk, :], (((1,), (1,)), ((), ())),
                               preferred_element_type=F32)

    m = [None] * ATTN_A_STREAMS
    acc = [None] * ATTN_A_STREAMS
    s_next = [scores(qi, 0) for qi in qs]
    for c in range(seq // tk):
        s_now = s_next
        if (c + 1) * tk < seq:
            s_next = [scores(qi, c + 1) for qi in qs]
        for i, s in enumerate(s_now):
            m_c = jnp.max(s, axis=-1, keepdims=True)
            m_new = m_c if m[i] is None else jnp.maximum(m[i], m_c)
            p = jnp.exp(s - m_new)
            pv = jnp.dot(p.astype(BF16), vp_ref[c * tk:(c + 1) * tk, :], preferred_element_type=F32)
            acc[i] = pv if m[i] is None else acc[i] * jnp.exp(m[i] - m_new) + pv
            m[i] = m_new
    for i in range(ATTN_A_STREAMS):
        o_ref[i * rows:(i + 1) * rows, :] = (acc[i][:, :HEAD_DIM] / acc[i][:, HEAD_DIM:]).astype(o_ref.dtype)


def _attention_a(p_a, q_gain, k_gain, bsz, seq):
    tq = min(ATTN_A_Q_ROWS, seq)
    nq = seq // tq
    cos, sin = _rope_tables(seq)
    k_col = A_HEADS
    v_col = A_HEADS + A_KV_HEADS
    q_tab = pl.BlockSpec((tq, HEAD_DIM), lambda b, kv, g, i: (i, 0))
    k_tab = pl.BlockSpec((seq, HEAD_DIM), lambda b, kv, g, i: (0, 0))
    gain = pl.BlockSpec((1, HEAD_DIM), lambda b, kv, g, i: (0, 0))
    return pl.pallas_call(
        _attn_a_kernel,
        grid=(bsz, A_KV_HEADS, A_GROUP, nq),
        in_specs=[
            pl.BlockSpec((tq, HEAD_DIM), lambda b, kv, g, i: (b * nq + i, kv * A_GROUP + g)),
            pl.BlockSpec((seq, HEAD_DIM), lambda b, kv, g, i: (b, k_col + kv)),
            pl.BlockSpec((seq, HEAD_DIM), lambda b, kv, g, i: (b, v_col + kv)),
            q_tab, q_tab, k_tab, k_tab, gain, gain,
        ],
        out_specs=pl.BlockSpec((tq, HEAD_DIM), lambda b, kv, g, i: (b * nq + i, kv * A_GROUP + g)),
        out_shape=jax.ShapeDtypeStruct((bsz * seq, A_Q), BF16),
        scratch_shapes=[pltpu.VMEM((seq, HEAD_DIM), BF16), pltpu.VMEM((seq, 2 * HEAD_DIM), BF16)],
        compiler_params=_params(4, 48),
        name="axial_gqa",
    )(p_a, p_a, p_a, cos, sin, cos, sin, q_gain.reshape(1, HEAD_DIM), k_gain.reshape(1, HEAD_DIM))


def _t5_bucket(rel):
    nb = REL_BUCKETS // 2
    max_exact = nb // 2
    n = np.abs(rel)
    large = max_exact + (np.log(np.maximum(n, 1) / max_exact) / np.log(REL_MAX_DISTANCE / max_exact)
                         * (nb - max_exact)).astype(np.int32)
    large = np.minimum(large, nb - 1)
    return (rel > 0).astype(np.int32) * nb + np.where(n < max_exact, n, large)


_C_WINDOW_OFFSETS = (0, -C_HALF, ATTN_C_Q - ATTN_C_K)


def _dilated_bias(rel_bias):
    per_group = []
    for gi, (_, dilation) in enumerate(C_PATTERNS):
        hs = slice(gi * C_HEADS_PER_GROUP, (gi + 1) * C_HEADS_PER_GROUP)
        variants = []
        for off in _C_WINDOW_OFFSETS:
            delta = off + np.arange(ATTN_C_K)[None, :] - np.arange(ATTN_C_Q)[:, None]
            vals = jnp.moveaxis(rel_bias[_t5_bucket(delta * dilation)][..., hs], -1, 0).astype(F32)
            variants.append(jnp.where((np.abs(delta) <= C_HALF)[None], vals, NEG_INF))
        per_group.append(jnp.stack(variants, axis=1))
    return jnp.stack(per_group, axis=0)


def _strided(start, size, stride):
    return pl.ds(start, size) if stride == 1 else pl.ds(start, size, stride=stride)


def _attn_c_kernel(q0, q1, q2, k0, k1, k2, v0, v1, v2, bias_ref, o_ref,
                   og0, og1, og2, ls0, ls1, ls2, *, seq):
    tile = pl.program_id(2)
    scale = HEAD_DIM ** -0.5
    groups = ((q0, k0, v0, og0, ls0), (q1, k1, v1, og1, ls1), (q2, k2, v2, og2, ls2))
    for gi, (q_ref, k_ref, v_ref, og_ref, ls_ref) in enumerate(groups):
        dil = C_PATTERNS[gi][1]
        n = seq // dil
        per_tile = ATTN_C_TOKENS // dil
        for res in range(dil):
            for blk in range(per_tile // ATTN_C_Q):
                m_local = blk * ATTN_C_Q
                m0 = tile * per_tile + m_local
                k_start = jnp.clip(m0 + _C_WINDOW_OFFSETS[1], 0, n - ATTN_C_K)
                variant = jnp.where(m0 == 0, 0, jnp.where(m0 == n - ATTN_C_Q, 2, 1))
                q_rows = _strided(m_local * dil + res, ATTN_C_Q, dil)
                k_rows = _strided(k_start * dil + res, ATTN_C_K, dil)
                s = _bdot_nt(q_ref[q_rows, :], k_ref[k_rows, :]) * scale + bias_ref[gi, variant]
                m = jnp.max(s, axis=-1, keepdims=True)
                p = jnp.exp(s - m)
                l = jnp.sum(p, axis=-1, keepdims=True)
                o = _bdot(p, v_ref[k_rows, :])
                og_ref[q_rows, :] = o / l
                ls_ref[q_rows, :] = jnp.broadcast_to(m + jnp.log(l), (ATTN_C_Q, HEAD_DIM))
    top = jnp.maximum(jnp.maximum(ls0[...], ls1[...]), ls2[...])
    e0, e1, e2 = jnp.exp(ls0[...] - top), jnp.exp(ls1[...] - top), jnp.exp(ls2[...] - top)
    o_ref[...] = ((e0 * og0[...] + e1 * og1[...] + e2 * og2[...]) / (e0 + e1 + e2)).astype(o_ref.dtype)


def _attention_c(p_c, rel_bias, bsz, seq):
    assert seq % ATTN_C_TOKENS == 0 and seq // C_PATTERNS[-1][1] >= ATTN_C_K
    nt = seq // ATTN_C_TOKENS
    bias = _dilated_bias(rel_bias)

    def head_col(part, gi):
        return lambda b, j, t: part * C_HEADS + gi * C_HEADS_PER_GROUP + j

    def q_spec(gi):
        col = head_col(0, gi)
        return pl.BlockSpec((ATTN_C_TOKENS, HEAD_DIM), lambda b, j, t: (b * nt + t, col(b, j, t)))

    def kv_spec(part, gi):
        col = head_col(part, gi)
        return pl.BlockSpec((seq, HEAD_DIM), lambda b, j, t: (b, col(b, j, t)))

    tile_buf = pltpu.VMEM((ATTN_C_TOKENS, HEAD_DIM), F32)
    return pl.pallas_call(
        functools.partial(_attn_c_kernel, seq=seq),
        grid=(bsz, C_HEADS_PER_GROUP, nt),
        in_specs=[q_spec(0), q_spec(1), q_spec(2),
                  kv_spec(1, 0), kv_spec(1, 1), kv_spec(1, 2),
                  kv_spec(2, 0), kv_spec(2, 1), kv_spec(2, 2),
                  pl.BlockSpec((C_GROUPS, None, len(_C_WINDOW_OFFSETS), ATTN_C_Q, ATTN_C_K),
                               lambda b, j, t: (0, j, 0, 0, 0))],
        out_specs=pl.BlockSpec((ATTN_C_TOKENS, HEAD_DIM), lambda b, j, t: (b * nt + t, j)),
        out_shape=jax.ShapeDtypeStruct((bsz * seq, C_OUT), BF16),
        scratch_shapes=[tile_buf] * 6,
        compiler_params=_params(3, 56),
        name="dilated_attn",
    )(*([p_c] * 9), bias)


def _head_ones():
    r = lax.broadcasted_iota(jnp.int32, (LANES, LANES), 0) // B_HEAD_DIM
    c = lax.broadcasted_iota(jnp.int32, (LANES, LANES), 1) // B_HEAD_DIM
    return (r == c).astype(F32)


def _head_sum(x, ones):
    parts = [jnp.dot(x[:, i:i + LANES], ones, precision=HIGHEST, preferred_element_type=F32)
             for i in range(0, x.shape[1], LANES)]
    return jnp.concatenate(parts, axis=1)


def _rwkv_prep_kernel(y_ref, yp_ref, yn_ref, mup_ref, mun_ref, w0_ref, w2_ref, a0_ref, a2_ref, g2_ref,
                      kk_ref, ka_ref, rk_ref,
                      r_o, v_o, kk_o, lw0_o, lw1_o, kd0_o, kd1_o, b0_o, b1_o, g_o, bonus_o,
                      *, blocks_per_seq):
    i = pl.program_id(0)
    ts = y_ref.shape[0]
    at_start = (i % blocks_per_seq) == 0
    at_end = (i % blocks_per_seq) == blocks_per_seq - 1
    rows = lax.broadcasted_iota(jnp.int32, (ts, 1), 0)

    def mixed(c0, c1):
        y = y_ref[:, c0:c1]
        before = jnp.where(at_start, 0.0, yp_ref[SUBLANES - 1:SUBLANES, c0:c1])
        after = jnp.where(at_end, 0.0, yn_ref[0:1, c0:c1])
        prev = jnp.where(rows == 0, before, pltpu.roll(y, 1, axis=0))
        nxt = jnp.where(rows == ts - 1, after, pltpu.roll(y, ts - 1, axis=0))
        return y + mup_ref[:, c0:c1] * (prev - y) + mun_ref[:, c0:c1] * (nxt - y)

    w = B_WIDTH
    r = mixed(0, w)
    k = mixed(w, 2 * w)
    v = mixed(2 * w, 3 * w)
    hw = mixed(3 * w, 3 * w + LORA_PAIR)
    ha = mixed(3 * w + LORA_PAIR, 3 * w + 2 * LORA_PAIR)
    hg = mixed(3 * w + 2 * LORA_PAIR, 3 * w + 2 * LORA_PAIR + GATE_LORA)
    ones = _head_ones()

    r_o[...] = r
    v_o[...] = v
    g_o[...] = _bdot(_sigmoid(hg), g2_ref[...])
    kk = k * kk_ref[...]
    kk = kk * lax.rsqrt(_head_sum(kk * kk, ones) + 1e-12)
    kk_o[...] = kk
    tanh_hw = jnp.tanh(hw)
    rrk = r * rk_ref[...]
    bonus = jnp.zeros_like(r)
    for d, (lw_o, kd_o, b_o) in enumerate(((lw0_o, kd0_o, b0_o), (lw1_o, kd1_o, b1_o))):
        z = -(w0_ref[d:d + 1, :] + _bdot(tanh_hw, w2_ref[d]))
        softplus = jnp.maximum(z, 0.0) + jnp.log(1.0 + jnp.exp(-jnp.abs(z)))
        lw_o[...] = -jnp.exp(-softplus - 0.5)
        a = _sigmoid(a0_ref[d:d + 1, :] + _bdot(ha, a2_ref[d]))
        kd = k * (1.0 + (a - 1.0) * ka_ref[...])
        kd_o[...] = kd
        b_o[...] = kk * a
        bonus = bonus + rrk * kd
    bonus_o[...] = _head_sum(bonus, ones) * v


def _rwkv_prep(p_b, mu_prev, mu_next, w0, w2, a0, a2, g2, k_k, k_a, r_k, seq):
    n_tok = p_b.shape[0]
    ts = min(RWKV_PREP_ROWS, seq)
    blocks_per_seq = seq // ts
    halo = ts // SUBLANES
    n_halo = n_tok // SUBLANES
    zeros = jnp.zeros_like(w2[0])
    w2p = jnp.stack([jnp.concatenate([w2[0], zeros], 0), jnp.concatenate([zeros, w2[1]], 0)]).astype(BF16)
    a2p = jnp.stack([jnp.concatenate([a2[0], zeros], 0), jnp.concatenate([zeros, a2[1]], 0)]).astype(BF16)
    full = lambda shape: pl.BlockSpec(shape, lambda i: (0,) * len(shape))
    row_blk = pl.BlockSpec((ts, B_WIDTH), lambda i: (i, 0))
    out = jax.ShapeDtypeStruct((n_tok, B_WIDTH), F32)
    return pl.pallas_call(
        functools.partial(_rwkv_prep_kernel, blocks_per_seq=blocks_per_seq),
        grid=(n_tok // ts,),
        in_specs=[
            pl.BlockSpec((ts, B_COLS), lambda i: (i, 0)),
            pl.BlockSpec((SUBLANES, B_COLS), lambda i: (jnp.maximum(i * halo - 1, 0), 0)),
            pl.BlockSpec((SUBLANES, B_COLS), lambda i: (jnp.minimum((i + 1) * halo, n_halo - 1), 0)),
            full((1, B_COLS)), full((1, B_COLS)),
            full((2, B_WIDTH)), full((2, LORA_PAIR, B_WIDTH)),
            full((2, B_WIDTH)), full((2, LORA_PAIR, B_WIDTH)),
            full((GATE_LORA, B_WIDTH)),
            full((1, B_WIDTH)), full((1, B_WIDTH)), full((1, B_WIDTH)),
        ],
        out_specs=[row_blk] * 11,
        out_shape=[out] * 11,
        compiler_params=_params(1, 56),
        name="rwkv_prep",
    )(p_b, p_b, p_b, mu_prev.reshape(1, B_COLS), mu_next.reshape(1, B_COLS), w0, w2p, a0, a2p,
      g2.astype(BF16), k_k.reshape(1, B_WIDTH), k_a.reshape(1, B_WIDTH), r_k.reshape(1, B_WIDTH))


def _split3(x):
    hi = x.astype(BF16)
    rest = x - hi.astype(F32)
    mid = rest.astype(BF16)
    return hi, mid, (rest - mid.astype(F32)).astype(BF16)


def _rwkv_scan_kernel(r0, v0, kk0, lw0, kd0, b0, r1, v1, kk1, lw1, kd1, b1, o0_ref, o1_ref, s_ref, *, chunk):
    @pl.when(pl.program_id(2) == 0)
    def _():
        s_ref[...] = jnp.zeros_like(s_ref)

    t, w = r0.shape
    heads = w // B_HEAD_DIM
    n_chunks = t // chunk
    row = lax.broadcasted_iota(jnp.int32, (t, t), 0)
    col = lax.broadcasted_iota(jnp.int32, (t, t), 1)
    same = (row // chunk) == (col // chunk)
    eye = (row == col).astype(F32)
    head_blk = (lax.broadcasted_iota(jnp.int32, (w, w), 0) // B_HEAD_DIM
                == lax.broadcasted_iota(jnp.int32, (w, w), 1) // B_HEAD_DIM)
    chunk_of_row = lax.broadcasted_iota(jnp.int32, (t, 1), 0) // chunk

    def by_chunk(x):
        return jnp.concatenate([jnp.where(chunk_of_row == c, x, 0.0) for c in range(n_chunks)], axis=1)

    def head(x, h):
        return x[:, h * B_HEAD_DIM:(h + 1) * B_HEAD_DIM]

    dirs = ((r0, v0, kk0, lw0, kd0, b0), (r1, v1, kk1, lw1, kd1, b1))
    out_refs = (o0_ref, o1_ref)
    strict, incl, pre = [], [], []
    for d, (r_ref, v_ref, kk_ref, lw_ref, kd_ref, b_ref) in enumerate(dirs):
        earlier = (col > row) if d == 1 else (col < row)
        strict.append(same & earlier)
        incl.append(same & (earlier | (col == row)))
        lw = lw_ref[...]
        sums = jnp.dot(jnp.concatenate([incl[d], same], axis=0).astype(BF16),
                       jnp.concatenate(_split3(lw), axis=1), preferred_element_type=F32)
        sums = sums[:, :w] + sums[:, w:2 * w] + sums[:, 2 * w:]
        cum, tot = sums[:t], sums[t:]
        grow, shrink, to_end = jnp.exp(cum), jnp.exp(-cum), jnp.exp(tot - cum)
        k, b = kd_ref[...], b_ref[...]
        pre.append(dict(v=v_ref[...], rt=r_ref[...] * grow, kb=kk_ref[...] * jnp.exp(cum - lw),
                        kt=k * shrink, bt=b * shrink, kt_end=k * to_end, bt_end=b * to_end,
                        decay_end=jnp.exp(tot)))

    units = [(d, h) for d in range(2) for h in range(heads)]
    part = lambda name, u: head(pre[u[0]][name], u[1])
    cross = {u: _bdot_nt(jnp.concatenate([part("kb", u), part("rt", u)], axis=0),
                         jnp.concatenate([part("kt", u), part("bt", u)], axis=0)) for u in units}
    a_k = {u: jnp.where(strict[u[0]], cross[u][:t, :t], 0.0) for u in units}
    a_b = {u: jnp.where(strict[u[0]], cross[u][:t, t:], 0.0) for u in units}
    a_rk = {u: jnp.where(incl[u[0]], cross[u][t:, :t], 0.0) for u in units}
    a_rb = {u: jnp.where(incl[u[0]], cross[u][t:, t:], 0.0) for u in units}
    inv = {u: eye - a_b[u] for u in units}
    power = {u: _bdot(a_b[u], a_b[u]) for u in units}
    akv = {u: _bdot(a_k[u], part("v", u)) for u in units}
    span = 2
    while span < chunk:
        if 2 * span < chunk:
            both = {u: _bdot(jnp.concatenate([inv[u], power[u]], axis=0), power[u]) for u in units}
            inv = {u: inv[u] + both[u][:t] for u in units}
            power = {u: both[u][t:] for u in units}
        else:
            inv = {u: inv[u] + _bdot(inv[u], power[u]) for u in units}
        span *= 2
    sol = {u: _bdot(inv[u], jnp.concatenate([part("kb", u), akv[u]], axis=1)) for u in units}
    free = {}
    for u in units:
        v = part("v", u)
        p, q = sol[u][:, :B_HEAD_DIM], sol[u][:, B_HEAD_DIM:]
        rhs = jnp.concatenate([jnp.concatenate([v, jnp.zeros_like(v)], axis=1),
                               jnp.concatenate([-q, -p], axis=1)], axis=0)
        free[u] = _bdot(jnp.concatenate([a_rk[u], a_rb[u]], axis=1), rhs)

    def all_heads(table, d, lo):
        return jnp.concatenate([table[(d, h)][:, lo:lo + B_HEAD_DIM] for h in range(heads)], axis=1)

    read, out_free, trans, inject = [], [], [], []
    for d in range(2):
        p_all, q_all = all_heads(sol, d, 0), all_heads(sol, d, B_HEAD_DIM)
        out_free.append(all_heads(free, d, 0))
        read.append(pre[d]["rt"] + all_heads(free, d, B_HEAD_DIM))
        bt_end_c = by_chunk(pre[d]["bt_end"])
        trans.append(_bdot_tn(p_all, bt_end_c))
        inject.append(_bdot_tn(jnp.concatenate([pre[d]["v"], q_all], axis=0),
                               jnp.concatenate([by_chunk(pre[d]["kt_end"]), -bt_end_c], axis=0)))

    state = [s_ref[0], s_ref[1]]
    for i in range(n_chunks):
        for d in range(2):
            c = n_chunks - 1 - i if d == 1 else i
            rows = slice(c * chunk, (c + 1) * chunk)
            cols = slice(c * w, (c + 1) * w)
            blocks = jnp.where(head_blk, jnp.concatenate([state[d]] * heads, axis=0), 0.0)
            out_refs[d][rows, :] = _bdot_nt(read[d][rows], blocks) + out_free[d][rows]
            inj = jnp.where(head_blk, inject[d][:, cols], 0.0)
            inj = sum(inj[h * B_HEAD_DIM:(h + 1) * B_HEAD_DIM] for h in range(heads))
            state[d] = (state[d] * pre[d]["decay_end"][c * chunk:c * chunk + 1, :]
                        - _bdot(state[d], jnp.where(head_blk, trans[d][:, cols], 0.0)) + inj)
    s_ref[0] = state[0]
    s_ref[1] = state[1]


def _rwkv_scan(r, v, kk, lw0, lw1, kd0, kd1, b0, b1, bsz, seq):
    tile = min(RWKV_SCAN_ROWS, seq)
    chunk = min(RWKV_CHUNK, tile)
    nb = seq // tile
    fwd = pl.BlockSpec((tile, RWKV_SCAN_LANES), lambda b, hg, c: (b * nb + c, hg))
    bwd = pl.BlockSpec((tile, RWKV_SCAN_LANES), lambda b, hg, c: (b * nb + nb - 1 - c, hg))
    out = jax.ShapeDtypeStruct(r.shape, F32)
    return pl.pallas_call(
        functools.partial(_rwkv_scan_kernel, chunk=chunk),
        grid=(bsz, B_WIDTH // RWKV_SCAN_LANES, nb),
        in_specs=[fwd] * 6 + [bwd] * 6,
        out_specs=[fwd, bwd],
        out_shape=[out, out],
        scratch_shapes=[pltpu.VMEM((2, B_HEAD_DIM, RWKV_SCAN_LANES), F32)],
        compiler_params=_params(3, 48),
        name="rwkv_scan",
    )(r, v, kk, lw0, kd0, b0, r, v, kk, lw1, kd1, b1)


def _rwkv_post_kernel(o0_ref, o1_ref, bonus_ref, g_ref, gn_g_ref, gn_b_ref, y_ref):
    ones = _head_ones()
    o = o0_ref[...] + o1_ref[...]
    mu = _head_sum(o, ones) * (1.0 / B_HEAD_DIM)
    oc = o - mu
    var = _head_sum(oc * oc, ones) * (1.0 / B_HEAD_DIM)
    normed = oc * lax.rsqrt(var + GN_EPS) * gn_g_ref[...] + gn_b_ref[...]
    y_ref[...] = ((normed + bonus_ref[...]) * g_ref[...]).astype(y_ref.dtype)


def _rwkv_post(o0, o1, bonus, g, gn_g, gn_b):
    n_tok = o0.shape[0]
    ts = min(RWKV_PREP_ROWS, n_tok)
    row_blk = pl.BlockSpec((ts, B_WIDTH), lambda i: (i, 0))
    vec = pl.BlockSpec((1, B_WIDTH), lambda i: (0, 0))
    return pl.pallas_call(
        _rwkv_post_kernel,
        grid=(n_tok // ts,),
        in_specs=[row_blk] * 4 + [vec, vec],
        out_specs=row_blk,
        out_shape=jax.ShapeDtypeStruct((n_tok, B_WIDTH), BF16),
        compiler_params=_params(1, 32),
        name="rwkv_post",
    )(o0, o1, bonus, g, gn_g.reshape(1, B_WIDTH), gn_b.reshape(1, B_WIDTH))


def _rwkv(p_b, mu_prev, mu_next, w0, w2, a0, a2, g2, k_k, k_a, r_k, gn_g, gn_b, bsz, seq):
    r, v, kk, lw0, lw1, kd0, kd1, b0, b1, g, bonus = _rwkv_prep(
        p_b, mu_prev, mu_next, w0, w2, a0, a2, g2, k_k, k_a, r_k, seq)
    o0, o1 = _rwkv_scan(r, v, kk, lw0, lw1, kd0, kd1, b0, b1, bsz, seq)
    return _rwkv_post(o0, o1, bonus, g, gn_g, gn_b)


def _merge_kernel(ya_ref, yb_ref, yc_ref, ga_ref, gb_ref, gc_ref, wa_ref, wb_ref, wc_ref, o_ref):
    def branch(y_ref, w_ref, gate_ref):
        return _sigmoid(gate_ref[...].astype(F32)) * jnp.dot(y_ref[...], w_ref[...], preferred_element_type=F32)

    merged = branch(ya_ref, wa_ref, ga_ref) + branch(yb_ref, wb_ref, gb_ref) + branch(yc_ref, wc_ref, gc_ref)
    o_ref[...] = merged.astype(o_ref.dtype)


def _merge(y_a, y_b, y_c, p_g, wb_a, wb_b, wb_c):
    n_tok = y_a.shape[0]
    tm = min(MERGE_ROWS, n_tok)
    tn = MERGE_COLS
    nj = D_MODEL // tn

    def rows(width):
        return pl.BlockSpec((tm, width), lambda i, j: (i, 0))

    def gate(branch):
        return pl.BlockSpec((tm, tn), lambda i, j: (i, branch * nj + j))

    def weight(depth):
        return pl.BlockSpec((depth, tn), lambda i, j: (0, j))

    return pl.pallas_call(
        _merge_kernel,
        grid=(n_tok // tm, nj),
        in_specs=[rows(A_Q), rows(B_WIDTH), rows(C_OUT), gate(0), gate(1), gate(2),
                  weight(A_Q), weight(B_WIDTH), weight(C_OUT)],
        out_specs=pl.BlockSpec((tm, tn), lambda i, j: (i, j)),
        out_shape=jax.ShapeDtypeStruct((n_tok, D_MODEL), BF16),
        compiler_params=_params(2, 32),
        name="gated_merge",
    )(y_a, y_b, y_c, p_g, p_g, p_g, wb_a, wb_b, wb_c)


def _first_lane_of(mask, lane):
    return jnp.min(jnp.where(mask, lane, float(ROUTER_COLS)), axis=-1, keepdims=True)


def _route_rows(logits):
    lane = lax.broadcasted_iota(jnp.int32, logits.shape, 1).astype(F32)
    grp_logit = jnp.where(lane < N_GROUPS, logits, -jnp.inf)
    e = jnp.exp(grp_logit - jnp.max(grp_logit, axis=-1, keepdims=True))
    prob = e / jnp.sum(e, axis=-1, keepdims=True)
    grp_w = jnp.max(prob, axis=-1, keepdims=True)
    grp = _first_lane_of(prob == grp_w, lane)
    first = N_GROUPS + grp * EXPERTS_PER_GROUP
    cand = jnp.where((lane >= first) & (lane < first + EXPERTS_PER_GROUP), logits, -jnp.inf)
    top1 = jnp.max(cand, axis=-1, keepdims=True)
    lane1 = _first_lane_of(cand == top1, lane)
    cand = jnp.where(lane == lane1, -jnp.inf, cand)
    top2 = jnp.max(cand, axis=-1, keepdims=True)
    lane2 = _first_lane_of(cand == top2, lane)
    e2 = jnp.exp(top2 - top1)
    gate1 = grp_w / (1.0 + e2)
    gate2 = grp_w * e2 / (1.0 + e2)
    return jnp.where(lane == 0, gate1,
                     jnp.where(lane == 1, gate2,
                               jnp.where(lane == 2, lane1 - N_GROUPS,
                                         jnp.where(lane == 3, lane2 - N_GROUPS, 0.0))))


def _out_proj_kernel(m_ref, x_ref, w_ref, g_ref, b_ref, wr_ref, br_ref, x_o, route_o, *, alpha):
    z = alpha * x_ref[...] + jnp.dot(m_ref[...], w_ref[...], preferred_element_type=F32)
    x1 = _layer_norm(z, g_ref[...], b_ref[...])
    x_o[...] = x1
    x_hi = x1.astype(BF16)
    x_lo = (x1 - x_hi.astype(F32)).astype(BF16)
    by_hi = jnp.dot(x_hi, wr_ref[...], preferred_element_type=F32)
    by_lo = jnp.dot(x_lo, wr_ref[:, :ROUTER_COLS], preferred_element_type=F32)
    logits = by_hi[:, :ROUTER_COLS] + by_hi[:, ROUTER_COLS:] + by_lo + br_ref[...]
    route_o[...] = _route_rows(logits)


def _out_proj(merged, x, w_out, ln_g, ln_b, w_router, b_router, alpha):
    n_tok = x.shape[0]
    tm = min(OUT_ROWS, n_tok)
    rows = lambda width: pl.BlockSpec((tm, width), lambda i: (i, 0))
    full = lambda shape: pl.BlockSpec(shape, lambda i: (0, 0))
    return pl.pallas_call(
        functools.partial(_out_proj_kernel, alpha=alpha),
        grid=(n_tok // tm,),
        in_specs=[rows(D_MODEL), rows(D_MODEL), full((D_MODEL, D_MODEL)), full((1, D_MODEL)),
                  full((1, D_MODEL)), full((D_MODEL, 2 * ROUTER_COLS)), full((1, ROUTER_COLS))],
        out_specs=[rows(D_MODEL), rows(ROUTER_COLS)],
        out_shape=[jax.ShapeDtypeStruct((n_tok, D_MODEL), F32),
                   jax.ShapeDtypeStruct((n_tok, ROUTER_COLS), F32)],
        compiler_params=_params(1, 48),
        name="out_proj_ln_router",
    )(merged, x, w_out, ln_g.reshape(1, D_MODEL), ln_b.reshape(1, D_MODEL), w_router, b_router)


def _dispatch(expert):
    n_tok = expert.shape[0]
    flat_e = expert.reshape(-1)
    n_asg = flat_e.shape[0]
    ids = jnp.arange(n_asg, dtype=jnp.int32)
    _, order = lax.sort((flat_e, ids), num_keys=1)
    _, rank = lax.sort((order, ids), num_keys=1)
    counts = jnp.bincount(flat_e, length=N_EXPERTS)
    padded = (counts + MOE_ROWS - 1) // MOE_ROWS * MOE_ROWS
    pad_end = jnp.cumsum(padded)
    pad_start = pad_end - padded
    start = jnp.cumsum(counts) - counts
    dest = (pad_start - start)[flat_e] + rank
    n_rows = (n_asg + MOE_ROWS - 1) // MOE_ROWS * MOE_ROWS + N_EXPERTS * MOE_ROWS
    n_blk = n_rows // MOE_ROWS
    blk_first = jnp.arange(n_blk) * MOE_ROWS
    blk_expert = jnp.minimum(jnp.searchsorted(pad_end, blk_first, side='right'), N_EXPERTS - 1)
    offset = blk_first - pad_start[blk_expert]
    blk_rows = jnp.clip(counts[blk_expert] - offset, 0, MOE_ROWS)
    blk_base = jnp.minimum(start[blk_expert] + offset, n_asg - 1)
    as_i32 = lambda a: a.astype(jnp.int32)
    return (as_i32(dest).reshape(n_tok, TOP_K), as_i32(order // TOP_K), as_i32(blk_expert), as_i32(blk_base),
            as_i32(blk_rows))


def _expert_kernel(blk_expert_ref, blk_base_ref, blk_rows_ref, tok_ref, x_hbm, wg_ref, wu_ref, wd_ref, o_ref,
                   x_buf, sem):
    i = pl.program_id(0)
    slot = i % 2
    last = pl.num_programs(0) - 1

    def gather(block, into):
        def body(r, carry):
            tok = tok_ref[blk_base_ref[block] + r]
            pltpu.make_async_copy(x_hbm.at[pl.ds(tok, 1)], x_buf.at[into, pl.ds(r, 1)], sem.at[into]).start()
            return carry

        lax.fori_loop(0, blk_rows_ref[block], body, 0)

    def wait_rows(n, which):
        for bit in range(MOE_ROWS.bit_length()):
            size = 1 << bit

            @pl.when((n & size) != 0)
            def _():
                pltpu.make_async_copy(x_hbm.at[pl.ds(0, size)], x_buf.at[which, pl.ds(0, size)],
                                      sem.at[which]).wait()

    @pl.when(i == 0)
    def _():
        x_buf[...] = jnp.zeros_like(x_buf)
        gather(0, 0)

    @pl.when(i < last)
    def _():
        gather(i + 1, 1 - slot)

    n_rows = blk_rows_ref[i]
    wait_rows(n_rows, slot)

    @pl.when(n_rows > 0)
    def _():
        x = x_buf[slot].astype(BF16)
        gate = jnp.dot(x, wg_ref[...].astype(BF16), preferred_element_type=F32)
        up = jnp.dot(x, wu_ref[...].astype(BF16), preferred_element_type=F32)
        hid = gate * _sigmoid(gate) * up
        o_ref[...] = jnp.dot(hid.astype(BF16), wd_ref[...].astype(BF16), preferred_element_type=F32)

    @pl.when(n_rows == 0)
    def _():
        o_ref[...] = jnp.zeros_like(o_ref)


def _experts(x, sorted_tok, blk_expert, blk_base, blk_rows, w_gate, w_up, w_down, layer):
    n_rows = blk_expert.shape[0] * MOE_ROWS
    grid_spec = pltpu.PrefetchScalarGridSpec(
        num_scalar_prefetch=4,
        grid=(n_rows // MOE_ROWS,),
        in_specs=[pl.BlockSpec(memory_space=pl.ANY),
                  pl.BlockSpec((None, None, D_MODEL, EXPERT_FF), lambda i, e, base, n, t: (layer, e[i], 0, 0)),
                  pl.BlockSpec((None, None, D_MODEL, EXPERT_FF), lambda i, e, base, n, t: (layer, e[i], 0, 0)),
                  pl.BlockSpec((None, None, EXPERT_FF, D_MODEL), lambda i, e, base, n, t: (layer, e[i], 0, 0))],
        out_specs=pl.BlockSpec((MOE_ROWS, D_MODEL), lambda i, e, base, n, t: (i, 0)),
        scratch_shapes=[pltpu.VMEM((2, MOE_ROWS, D_MODEL), F32), pltpu.SemaphoreType.DMA((2,))],
    )
    return pl.pallas_call(
        _expert_kernel,
        grid_spec=grid_spec,
        out_shape=jax.ShapeDtypeStruct((n_rows, D_MODEL), F32),
        compiler_params=_params(1, 56),
        name="expert_ffn",
    )(blk_expert, blk_base, blk_rows, sorted_tok, x, w_gate, w_up, w_down)


def _combine_kernel(x_ref, ya_ref, yb_ref, gate_ref, g_ref, b_ref, x_o, xb_o, *, alpha):
    gate = gate_ref[...]
    moe = gate[:, 0:1] * ya_ref[...] + gate[:, 1:2] * yb_ref[...]
    x2 = _layer_norm(alpha * x_ref[...] + moe, g_ref[...], b_ref[...])
    x_o[...] = x2
    xb_o[...] = x2.astype(BF16)


def _combine(x1, y_first, y_second, route, ln_g, ln_b, alpha):
    n_tok = x1.shape[0]
    tm = min(OUT_ROWS, n_tok)
    rows = lambda width: pl.BlockSpec((tm, width), lambda i: (i, 0))
    vec = pl.BlockSpec((1, D_MODEL), lambda i: (0, 0))
    return pl.pallas_call(
        functools.partial(_combine_kernel, alpha=alpha),
        grid=(n_tok // tm,),
        in_specs=[rows(D_MODEL), rows(D_MODEL), rows(D_MODEL), rows(LANES), vec, vec],
        out_specs=[rows(D_MODEL), rows(D_MODEL)],
        out_shape=[jax.ShapeDtypeStruct((n_tok, D_MODEL), F32),
                   jax.ShapeDtypeStruct((n_tok, D_MODEL), BF16)],
        compiler_params=_params(1, 32),
        name="moe_combine_ln",
    )(x1, y_first, y_second, route, ln_g.reshape(1, D_MODEL), ln_b.reshape(1, D_MODEL))


def _moe(x1, route, w_gate, w_up, w_down, layer, ln_g, ln_b, alpha):
    dest, sorted_tok, blk_expert, blk_base, blk_rows = _dispatch(route[:, TOP_K:2 * TOP_K].astype(jnp.int32))
    y_rows = _experts(x1, sorted_tok, blk_expert, blk_base, blk_rows, w_gate, w_up, w_down, layer)
    return _combine(x1, y_rows[dest[:, 0]], y_rows[dest[:, 1]], route, ln_g, ln_b, alpha)


def kernel(x, w_in, w_branch, w_out, mu_prev, mu_next, rwkv_w0, rwkv_w2, rwkv_a0, rwkv_a2, rwkv_g2, rwkv_k_k, rwkv_k_a, rwkv_r_k, rwkv_gn_g, rwkv_gn_b, q_norm, k_norm, rel_bias, ln1_g, ln1_b, router_group_w, router_group_b, router_expert_w, router_expert_b, w_gate, w_up, w_down, ln2_g, ln2_b):
    bsz, seq, _ = x.shape
    depth = w_in.shape[0]
    alpha = (2.0 * depth) ** 0.25
    xf = x.reshape(bsz * seq, D_MODEL)
    xb = xf.astype(BF16)
    col_b = A_COLS
    col_c = A_COLS + B_COLS
    col_g = A_COLS + B_COLS + C_COLS
    for l in range(depth):
        w_l = w_in[l]
        p_a = _matmul(xb, w_l[:, :col_b].astype(BF16), 768)
        p_b = _matmul(xb, w_l[:, col_b:col_c].astype(BF16), 1152)
        p_c = _matmul(xb, w_l[:, col_c:col_g].astype(BF16), 1152)
        p_g = _matmul(xb, w_l[:, col_g:].astype(BF16), 1024, out_dtype=BF16)
        y_a = _attention_a(p_a, q_norm[l], k_norm[l], bsz, seq)
        y_b = _rwkv(p_b, mu_prev[l], mu_next[l], rwkv_w0[l], rwkv_w2[l], rwkv_a0[l], rwkv_a2[l],
                    rwkv_g2[l], rwkv_k_k[l], rwkv_k_a[l], rwkv_r_k[l], rwkv_gn_g[l], rwkv_gn_b[l],
                    bsz, seq)
        y_c = _attention_c(p_c, rel_bias, bsz, seq)
        wb = w_branch[l].astype(BF16)
        merged = _merge(y_a, y_b, y_c, p_g, wb[:A_Q], wb[A_Q:A_Q + B_WIDTH], wb[A_Q + B_WIDTH:])
        w_router = jnp.pad(jnp.concatenate([router_group_w[l], router_expert_w[l]], axis=1),
                           ((0, 0), (0, ROUTER_COLS - N_GROUPS - N_EXPERTS)))
        w_router_hi = w_router.astype(BF16)
        w_router = jnp.concatenate([w_router_hi, (w_router - w_router_hi.astype(F32)).astype(BF16)], axis=1)
        b_router = jnp.pad(jnp.concatenate([router_group_b[l], router_expert_b[l]]),
                           (0, ROUTER_COLS - N_GROUPS - N_EXPERTS)).reshape(1, ROUTER_COLS)
        x1, route = _out_proj(merged, xf, w_out[l].astype(BF16), ln1_g[l], ln1_b[l],
                              w_router, b_router, alpha)
        xf, xb = _moe(x1, route, w_gate, w_up, w_down, l, ln2_g[l], ln2_b[l], alpha)
    return xf.reshape(bsz, seq, D_MODEL)
```

```python
import functools

import numpy as np
import jax
import jax.numpy as jnp
from jax import lax
from jax.experimental import pallas as pl
from jax.experimental.pallas import tpu as pltpu

F32 = jnp.float32
BF16 = jnp.bfloat16
HIGHEST = lax.Precision.HIGHEST

D_MODEL = 2048
DEPTH = 4
GRID_W = 64
NEG_INF = -1e30
LN_EPS = 1e-5

A_HEADS = 8
A_KV_HEADS = 2
A_GROUP = A_HEADS // A_KV_HEADS
HEAD_DIM = 128
ROPE_THETA = 10000.0
QK_EPS = 1e-6

B_HEAD_DIM = 64
B_WIDTH = 1024
B_HEADS = B_WIDTH // B_HEAD_DIM
LORA_PAIR = 128
GATE_LORA = 128
GN_EPS = 64e-5

C_PATTERNS = ((128, 1), (512, 4), (2048, 16))
C_GROUPS = 3
C_HEADS_PER_GROUP = 4
C_HEADS = C_GROUPS * C_HEADS_PER_GROUP
C_HALF = 64
REL_BUCKETS = 32
REL_MAX_DISTANCE = 1024

N_GROUPS = 8
EXPERTS_PER_GROUP = 8
N_EXPERTS = N_GROUPS * EXPERTS_PER_GROUP
TOP_K = 2
EXPERT_FF = 384

A_Q = A_HEADS * HEAD_DIM
A_KV = A_KV_HEADS * HEAD_DIM
A_COLS = A_Q + 2 * A_KV
B_COLS = 3 * B_WIDTH + 2 * LORA_PAIR + GATE_LORA
C_QKV = C_HEADS * HEAD_DIM
C_COLS = 3 * C_QKV
N_BRANCHES = 3
GATE_COLS = N_BRANCHES * D_MODEL
C_OUT = C_HEADS_PER_GROUP * HEAD_DIM

LANES = 128
SUBLANES = 8
MIB = 1024 * 1024

MM_ROWS = 1024
ATTN_A_Q_ROWS = 512
ATTN_A_STREAMS = 2
ATTN_A_KV_CHUNK = 1024
ATTN_C_Q = 128
ATTN_C_K = 2 * ATTN_C_Q
ATTN_C_TOKENS = ATTN_C_Q * C_PATTERNS[-1][1]
RWKV_PREP_ROWS = 256
RWKV_CHUNK = 64
RWKV_SCAN_ROWS = 128
RWKV_SCAN_LANES = 512
MERGE_ROWS = 512
MERGE_COLS = 512
OUT_ROWS = 256
MOE_ROWS = 256
ROUTER_COLS = 128


def _params(n_grid, vmem_mib):
    return pltpu.CompilerParams(dimension_semantics=("arbitrary",) * n_grid,
                                vmem_limit_bytes=vmem_mib * MIB)


def _bdot(a, b):
    return jnp.dot(a.astype(BF16), b.astype(BF16), preferred_element_type=F32)


def _bdot_nt(a, b):
    return lax.dot_general(a.astype(BF16), b.astype(BF16), (((1,), (1,)), ((), ())),
                           preferred_element_type=F32)


def _bdot_tn(a, b):
    return lax.dot_general(a.astype(BF16), b.astype(BF16), (((0,), (0,)), ((), ())),
                           preferred_element_type=F32)


def _sigmoid(x):
    return 1.0 / (1.0 + jnp.exp(-x))


def _layer_norm(z, g, b):
    mu = jnp.mean(z, axis=-1, keepdims=True)
    zc = z - mu
    var = jnp.mean(zc * zc, axis=-1, keepdims=True)
    return zc * lax.rsqrt(var + LN_EPS) * g + b


def _mm_kernel(x_ref, w_ref, o_ref):
    o_ref[...] = jnp.dot(x_ref[...], w_ref[...], preferred_element_type=F32).astype(o_ref.dtype)


def _matmul(x, w, tn, out_dtype=F32):
    m, k = x.shape
    n = w.shape[1]
    tm = min(MM_ROWS, m)
    return pl.pallas_call(
        _mm_kernel,
        grid=(m // tm, n // tn),
        in_specs=[pl.BlockSpec((tm, k), lambda i, j: (i, 0)),
                  pl.BlockSpec((k, tn), lambda i, j: (0, j))],
        out_specs=pl.BlockSpec((tm, tn), lambda i, j: (i, j)),
        out_shape=jax.ShapeDtypeStruct((m, n), out_dtype),
        compiler_params=_params(2, 48),
        name="dense_proj",
    )(x, w)


def _rope_tables(seq):
    n = HEAD_DIM // 2
    inv = ROPE_THETA ** (-jnp.arange(0, n, 2, dtype=F32) / n)
    pos = jnp.arange(seq)
    ang_r = (pos // GRID_W).astype(F32)[:, None] * inv[None, :]
    ang_c = (pos % GRID_W).astype(F32)[:, None] * inv[None, :]
    cos = jnp.concatenate([jnp.cos(ang_r)] * 2 + [jnp.cos(ang_c)] * 2, axis=-1)
    sin = jnp.concatenate([-jnp.sin(ang_r), jnp.sin(ang_r), -jnp.sin(ang_c), jnp.sin(ang_c)], axis=-1)
    return cos, sin


def _norm_rope(t, gain, cos, sin):
    t = t * lax.rsqrt(jnp.mean(t * t, axis=-1, keepdims=True) + QK_EPS) * gain
    lane = lax.broadcasted_iota(jnp.int32, t.shape, 1)
    quarter = HEAD_DIM // 4
    partner = jnp.where(lane % (2 * quarter) < quarter,
                        pltpu.roll(t, HEAD_DIM - quarter, axis=1), pltpu.roll(t, quarter, axis=1))
    return t * cos + partner * sin


def _attn_a_kernel(q_ref, k_ref, v_ref, cq_ref, sq_ref, ck_ref, sk_ref, qg_ref, kg_ref, o_ref,
                   kp_ref, vp_ref):
    @pl.when((pl.program_id(2) == 0) & (pl.program_id(3) == 0))
    def _():
        kp_ref[...] = _norm_rope(k_ref[...], kg_ref[...], ck_ref[...], sk_ref[...]).astype(BF16)
        vp_ref[:, :HEAD_DIM] = v_ref[...].astype(BF16)
        vp_ref[:, HEAD_DIM:] = jnp.ones((v_ref.shape[0], HEAD_DIM), BF16)

    q = (_norm_rope(q_ref[...], qg_ref[...], cq_ref[...], sq_ref[...]) * HEAD_DIM ** -0.5).astype(BF16)
    seq = kp_ref.shape[0]
    tk = min(ATTN_A_KV_CHUNK, seq)
    rows = q.shape[0] // ATTN_A_STREAMS
    qs = [q[i * rows:(i + 1) * rows] for i in range(ATTN_A_STREAMS)]

    def scores(qi, c):
        return lax.dot_general(qi, kp_ref[c * tk:(c + 1) * tk, :], (((1,), (1,)), ((), ())),
                               preferred_element_type=F32)

    m = [None] * ATTN_A_STREAMS
    acc = [None] * ATTN_A_STREAMS
    s_next = [scores(qi, 0) for qi in qs]
    for c in range(seq // tk):
        s_now = s_next
        if (c + 1) * tk < seq:
            s_next = [scores(qi, c + 1) for qi in qs]
        for i, s in enumerate(s_now):
            m_c = jnp.max(s, axis=-1, keepdims=True)
            m_new = m_c if m[i] is None else jnp.maximum(m[i], m_c)
            p = jnp.exp(s - m_new)
            pv = jnp.dot(p.astype(BF16), vp_ref[c * tk:(c + 1) * tk, :], preferred_element_type=F32)
            acc[i] = pv if m[i] is None else acc[i] * jnp.exp(m[i] - m_new) + pv
            m[i] = m_new
    for i in range(ATTN_A_STREAMS):
        o_ref[i * rows:(i + 1) * rows, :] = (acc[i][:, :HEAD_DIM] / acc[i][:, HEAD_DIM:]).astype(o_ref.dtype)


def _attention_a(p_a, q_gain, k_gain, bsz, seq):
    tq = min(ATTN_A_Q_ROWS, seq)
    nq = seq // tq
    cos, sin = _rope_tables(seq)
    k_col = A_HEADS
    v_col = A_HEADS + A_KV_HEADS
    q_tab = pl.BlockSpec((tq, HEAD_DIM), lambda b, kv, g, i: (i, 0))
    k_tab = pl.BlockSpec((seq, HEAD_DIM), lambda b, kv, g, i: (0, 0))
    gain = pl.BlockSpec((1, HEAD_DIM), lambda b, kv, g, i: (0, 0))
    return pl.pallas_call(
        _attn_a_kernel,
        grid=(bsz, A_KV_HEADS, A_GROUP, nq),
        in_specs=[
            pl.BlockSpec((tq, HEAD_DIM), lambda b, kv, g, i: (b * nq + i, kv * A_GROUP + g)),
            pl.BlockSpec((seq, HEAD_DIM), lambda b, kv, g, i: (b, k_col + kv)),
            pl.BlockSpec((seq, HEAD_DIM), lambda b, kv, g, i: (b, v_col + kv)),
            q_tab, q_tab, k_tab, k_tab, gain, gain,
        ],
        out_specs=pl.BlockSpec((tq, HEAD_DIM), lambda b, kv, g, i: (b * nq + i, kv * A_GROUP + g)),
        out_shape=jax.ShapeDtypeStruct((bsz * seq, A_Q), BF16),
        scratch_shapes=[pltpu.VMEM((seq, HEAD_DIM), BF16), pltpu.VMEM((seq, 2 * HEAD_DIM), BF16)],
        compiler_params=_params(4, 48),
        name="axial_gqa",
    )(p_a, p_a, p_a, cos, sin, cos, sin, q_gain.reshape(1, HEAD_DIM), k_gain.reshape(1, HEAD_DIM))


def _t5_bucket(rel):
    nb = REL_BUCKETS // 2
    max_exact = nb // 2
    n = np.abs(rel)
    large = max_exact + (np.log(np.maximum(n, 1) / max_exact) / np.log(REL_MAX_DISTANCE / max_exact)
                         * (nb - max_exact)).astype(np.int32)
    large = np.minimum(large, nb - 1)
    return (rel > 0).astype(np.int32) * nb + np.where(n < max_exact, n, large)


_C_WINDOW_OFFSETS = (0, -C_HALF, ATTN_C_Q - ATTN_C_K)


def _dilated_bias(rel_bias):
    per_group = []
    for gi, (_, dilation) in enumerate(C_PATTERNS):
        hs = slice(gi * C_HEADS_PER_GROUP, (gi + 1) * C_HEADS_PER_GROUP)
        variants = []
        for off in _C_WINDOW_OFFSETS:
            delta = off + np.arange(ATTN_C_K)[None, :] - np.arange(ATTN_C_Q)[:, None]
            vals = jnp.moveaxis(rel_bias[_t5_bucket(delta * dilation)][..., hs], -1, 0).astype(F32)
            variants.append(jnp.where((np.abs(delta) <= C_HALF)[None], vals, NEG_INF))
        per_group.append(jnp.stack(variants, axis=1))
    return jnp.stack(per_group, axis=0)


def _strided(start, size, stride):
    return pl.ds(start, size) if stride == 1 else pl.ds(start, size, stride=stride)


def _attn_c_kernel(q0, q1, q2, k0, k1, k2, v0, v1, v2, bias_ref, o_ref,
                   og0, og1, og2, ls0, ls1, ls2, *, seq):
    tile = pl.program_id(2)
    scale = HEAD_DIM ** -0.5
    groups = ((q0, k0, v0, og0, ls0), (q1, k1, v1, og1, ls1), (q2, k2, v2, og2, ls2))
    for gi, (q_ref, k_ref, v_ref, og_ref, ls_ref) in enumerate(groups):
        dil = C_PATTERNS[gi][1]
        n = seq // dil
        per_tile = ATTN_C_TOKENS // dil
        for res in range(dil):
            for blk in range(per_tile // ATTN_C_Q):
                m_local = blk * ATTN_C_Q
                m0 = tile * per_tile + m_local
                k_start = jnp.clip(m0 + _C_WINDOW_OFFSETS[1], 0, n - ATTN_C_K)
                variant = jnp.where(m0 == 0, 0, jnp.where(m0 == n - ATTN_C_Q, 2, 1))
                q_rows = _strided(m_local * dil + res, ATTN_C_Q, dil)
                k_rows = _strided(k_start * dil + res, ATTN_C_K, dil)
                s = _bdot_nt(q_ref[q_rows, :], k_ref[k_rows, :]) * scale + bias_ref[gi, variant]
                m = jnp.max(s, axis=-1, keepdims=True)
                p = jnp.exp(s - m)
                l = jnp.sum(p, axis=-1, keepdims=True)
                o = _bdot(p, v_ref[k_rows, :])
                og_ref[q_rows, :] = o / l
                ls_ref[q_rows, :] = jnp.broadcast_to(m + jnp.log(l), (ATTN_C_Q, HEAD_DIM))
    top = jnp.maximum(jnp.maximum(ls0[...], ls1[...]), ls2[...])
    e0, e1, e2 = jnp.exp(ls0[...] - top), jnp.exp(ls1[...] - top), jnp.exp(ls2[...] - top)
    o_ref[...] = ((e0 * og0[...] + e1 * og1[...] + e2 * og2[...]) / (e0 + e1 + e2)).astype(o_ref.dtype)


def _attention_c(p_c, rel_bias, bsz, seq):
    assert seq % ATTN_C_TOKENS == 0 and seq // C_PATTERNS[-1][1] >= ATTN_C_K
    nt = seq // ATTN_C_TOKENS
    bias = _dilated_bias(rel_bias)

    def head_col(part, gi):
        return lambda b, j, t: part * C_HEADS + gi * C_HEADS_PER_GROUP + j

    def q_spec(gi):
        col = head_col(0, gi)
        return pl.BlockSpec((ATTN_C_TOKENS, HEAD_DIM), lambda b, j, t: (b * nt + t, col(b, j, t)))

    def kv_spec(part, gi):
        col = head_col(part, gi)
        return pl.BlockSpec((seq, HEAD_DIM), lambda b, j, t: (b, col(b, j, t)))

    tile_buf = pltpu.VMEM((ATTN_C_TOKENS, HEAD_DIM), F32)
    return pl.pallas_call(
        functools.partial(_attn_c_kernel, seq=seq),
        grid=(bsz, C_HEADS_PER_GROUP, nt),
        in_specs=[q_spec(0), q_spec(1), q_spec(2),
                  kv_spec(1, 0), kv_spec(1, 1), kv_spec(1, 2),
                  kv_spec(2, 0), kv_spec(2, 1), kv_spec(2, 2),
                  pl.BlockSpec((C_GROUPS, None, len(_C_WINDOW_OFFSETS), ATTN_C_Q, ATTN_C_K),
                               lambda b, j, t: (0, j, 0, 0, 0))],
        out_specs=pl.BlockSpec((ATTN_C_TOKENS, HEAD_DIM), lambda b, j, t: (b * nt + t, j)),
        out_shape=jax.ShapeDtypeStruct((bsz * seq, C_OUT), BF16),
        scratch_shapes=[tile_buf] * 6,
        compiler_params=_params(3, 56),
        name="dilated_attn",
    )(*([p_c] * 9), bias)


def _head_ones():
    r = lax.broadcasted_iota(jnp.int32, (LANES, LANES), 0) // B_HEAD_DIM
    c = lax.broadcasted_iota(jnp.int32, (LANES, LANES), 1) // B_HEAD_DIM
    return (r == c).astype(F32)


def _head_sum(x, ones):
    parts = [jnp.dot(x[:, i:i + LANES], ones, precision=HIGHEST, preferred_element_type=F32)
             for i in range(0, x.shape[1], LANES)]
    return jnp.concatenate(parts, axis=1)


def _rwkv_prep_kernel(y_ref, yp_ref, yn_ref, mup_ref, mun_ref, w0_ref, w2_ref, a0_ref, a2_ref, g2_ref,
                      kk_ref, ka_ref, rk_ref,
                      r_o, v_o, kk_o, lw0_o, lw1_o, kd0_o, kd1_o, b0_o, b1_o, g_o, bonus_o,
                      *, blocks_per_seq):
    i = pl.program_id(0)
    ts = y_ref.shape[0]
    at_start = (i % blocks_per_seq) == 0
    at_end = (i % blocks_per_seq) == blocks_per_seq - 1
    rows = lax.broadcasted_iota(jnp.int32, (ts, 1), 0)

    def mixed(c0, c1):
        y = y_ref[:, c0:c1]
        before = jnp.where(at_start, 0.0, yp_ref[SUBLANES - 1:SUBLANES, c0:c1])
        after = jnp.where(at_end, 0.0, yn_ref[0:1, c0:c1])
        prev = jnp.where(rows == 0, before, pltpu.roll(y, 1, axis=0))
        nxt = jnp.where(rows == ts - 1, after, pltpu.roll(y, ts - 1, axis=0))
        return y + mup_ref[:, c0:c1] * (prev - y) + mun_ref[:, c0:c1] * (nxt - y)

    w = B_WIDTH
    r = mixed(0, w)
    k = mixed(w, 2 * w)
    v = mixed(2 * w, 3 * w)
    hw = mixed(3 * w, 3 * w + LORA_PAIR)
    ha = mixed(3 * w + LORA_PAIR, 3 * w + 2 * LORA_PAIR)
    hg = mixed(3 * w + 2 * LORA_PAIR, 3 * w + 2 * LORA_PAIR + GATE_LORA)
    ones = _head_ones()

    r_o[...] = r
    v_o[...] = v
    g_o[...] = _bdot(_sigmoid(hg), g2_ref[...])
    kk = k * kk_ref[...]
    kk = kk * lax.rsqrt(_head_sum(kk * kk, ones) + 1e-12)
    kk_o[...] = kk
    tanh_hw = jnp.tanh(hw)
    rrk = r * rk_ref[...]
    bonus = jnp.zeros_like(r)
    for d, (lw_o, kd_o, b_o) in enumerate(((lw0_o, kd0_o, b0_o), (lw1_o, kd1_o, b1_o))):
        z = -(w0_ref[d:d + 1, :] + _bdot(tanh_hw, w2_ref[d]))
        softplus = jnp.maximum(z, 0.0) + jnp.log(1.0 + jnp.exp(-jnp.abs(z)))
        lw_o[...] = -jnp.exp(-softplus - 0.5)
        a = _sigmoid(a0_ref[d:d + 1, :] + _bdot(ha, a2_ref[d]))
        kd = k * (1.0 + (a - 1.0) * ka_ref[...])
        kd_o[...] = kd
        b_o[...] = kk * a
        bonus = bonus + rrk * kd
    bonus_o[...] = _head_sum(bonus, ones) * v


def _rwkv_prep(p_b, mu_prev, mu_next, w0, w2, a0, a2, g2, k_k, k_a, r_k, seq):
    n_tok = p_b.shape[0]
    ts = min(RWKV_PREP_ROWS, seq)
    blocks_per_seq = seq // ts
    halo = ts // SUBLANES
    n_halo = n_tok // SUBLANES
    zeros = jnp.zeros_like(w2[0])
    w2p = jnp.stack([jnp.concatenate([w2[0], zeros], 0), jnp.concatenate([zeros, w2[1]], 0)]).astype(BF16)
    a2p = jnp.stack([jnp.concatenate([a2[0], zeros], 0), jnp.concatenate([zeros, a2[1]], 0)]).astype(BF16)
    full = lambda shape: pl.BlockSpec(shape, lambda i: (0,) * len(shape))
    row_blk = pl.BlockSpec((ts, B_WIDTH), lambda i: (i, 0))
    out = jax.ShapeDtypeStruct((n_tok, B_WIDTH), F32)
    return pl.pallas_call(
        functools.partial(_rwkv_prep_kernel, blocks_per_seq=blocks_per_seq),
        grid=(n_tok // ts,),
        in_specs=[
            pl.BlockSpec((ts, B_COLS), lambda i: (i, 0)),
            pl.BlockSpec((SUBLANES, B_COLS), lambda i: (jnp.maximum(i * halo - 1, 0), 0)),
            pl.BlockSpec((SUBLANES, B_COLS), lambda i: (jnp.minimum((i + 1) * halo, n_halo - 1), 0)),
            full((1, B_COLS)), full((1, B_COLS)),
            full((2, B_WIDTH)), full((2, LORA_PAIR, B_WIDTH)),
            full((2, B_WIDTH)), full((2, LORA_PAIR, B_WIDTH)),
            full((GATE_LORA, B_WIDTH)),
            full((1, B_WIDTH)), full((1, B_WIDTH)), full((1, B_WIDTH)),
        ],
        out_specs=[row_blk] * 11,
        out_shape=[out] * 11,
        compiler_params=_params(1, 56),
        name="rwkv_prep",
    )(p_b, p_b, p_b, mu_prev.reshape(1, B_COLS), mu_next.reshape(1, B_COLS), w0, w2p, a0, a2p,
      g2.astype(BF16), k_k.reshape(1, B_WIDTH), k_a.reshape(1, B_WIDTH), r_k.reshape(1, B_WIDTH))


def _split3(x):
    hi = x.astype(BF16)
    rest = x - hi.astype(F32)
    mid = rest.astype(BF16)
    return hi, mid, (rest - mid.astype(F32)).astype(BF16)


def _rwkv_scan_kernel(r0, v0, kk0, lw0, kd0, b0, r1, v1, kk1, lw1, kd1, b1, o0_ref, o1_ref, s_ref, *, chunk):
    @pl.when(pl.program_id(2) == 0)
    def _():
        s_ref[...] = jnp.zeros_like(s_ref)

    t, w = r0.shape
    heads = w // B_HEAD_DIM
    n_chunks = t // chunk
    row = lax.broadcasted_iota(jnp.int32, (t, t), 0)
    col = lax.broadcasted_iota(jnp.int32, (t, t), 1)
    same = (row // chunk) == (col // chunk)
    eye = (row == col).astype(F32)
    head_blk = (lax.broadcasted_iota(jnp.int32, (w, w), 0) // B_HEAD_DIM
                == lax.broadcasted_iota(jnp.int32, (w, w), 1) // B_HEAD_DIM)
    chunk_of_row = lax.broadcasted_iota(jnp.int32, (t, 1), 0) // chunk

    def by_chunk(x):
        return jnp.concatenate([jnp.where(chunk_of_row == c, x, 0.0) for c in range(n_chunks)], axis=1)

    def head(x, h):
        return x[:, h * B_HEAD_DIM:(h + 1) * B_HEAD_DIM]

    dirs = ((r0, v0, kk0, lw0, kd0, b0), (r1, v1, kk1, lw1, kd1, b1))
    out_refs = (o0_ref, o1_ref)
    strict, incl, pre = [], [], []
    for d, (r_ref, v_ref, kk_ref, lw_ref, kd_ref, b_ref) in enumerate(dirs):
        earlier = (col > row) if d == 1 else (col < row)
        strict.append(same & earlier)
        incl.append(same & (earlier | (col == row)))
        lw = lw_ref[...]
        sums = jnp.dot(jnp.concatenate([incl[d], same], axis=0).astype(BF16),
                       jnp.concatenate(_split3(lw), axis=1), preferred_element_type=F32)
        sums = sums[:, :w] + sums[:, w:2 * w] + sums[:, 2 * w:]
        cum, tot = sums[:t], sums[t:]
        grow, shrink, to_end = jnp.exp(cum), jnp.exp(-cum), jnp.exp(tot - cum)
        k, b = kd_ref[...], b_ref[...]
        pre.append(dict(v=v_ref[...], rt=r_ref[...] * grow, kb=kk_ref[...] * jnp.exp(cum - lw),
                        kt=k * shrink, bt=b * shrink, kt_end=k * to_end, bt_end=b * to_end,
                        decay_end=jnp.exp(tot)))

    units = [(d, h) for d in range(2) for h in range(heads)]
    part = lambda name, u: head(pre[u[0]][name], u[1])
    cross = {u: _bdot_nt(jnp.concatenate([part("kb", u), part("rt", u)], axis=0),
                         jnp.concatenate([part("kt", u), part("bt", u)], axis=0)) for u in units}
    a_k = {u: jnp.where(strict[u[0]], cross[u][:t, :t], 0.0) for u in units}
    a_b = {u: jnp.where(strict[u[0]], cross[u][:t, t:], 0.0) for u in units}
    a_rk = {u: jnp.where(incl[u[0]], cross[u][t:, :t], 0.0) for u in units}
    a_rb = {u: jnp.where(incl[u[0]], cross[u][t:, t:], 0.0) for u in units}
    inv = {u: eye - a_b[u] for u in units}
    power = {u: _bdot(a_b[u], a_b[u]) for u in units}
    akv = {u: _bdot(a_k[u], part("v", u)) for u in units}
    span = 2
    while span < chunk:
        if 2 * span < chunk:
            both = {u: _bdot(jnp.concatenate([inv[u], power[u]], axis=0), power[u]) for u in units}
            inv = {u: inv[u] + both[u][:t] for u in units}
            power = {u: both[u][t:] for u in units}
        else:
            inv = {u: inv[u] + _bdot(inv[u], power[u]) for u in units}
        span *= 2
    sol = {u: _bdot(inv[u], jnp.concatenate([part("kb", u), akv[u]], axis=1)) for u in units}
    free = {}
    for u in units:
        v = part("v", u)
        p, q = sol[u][:, :B_HEAD_DIM], sol[u][:, B_HEAD_DIM:]
        rhs = jnp.concatenate([jnp.concatenate([v, jnp.zeros_like(v)], axis=1),
                               jnp.concatenate([-q, -p], axis=1)], axis=0)
        free[u] = _bdot(jnp.concatenate([a_rk[u], a_rb[u]], axis=1), rhs)

    def all_heads(table, d, lo):
        return jnp.concatenate([table[(d, h)][:, lo:lo + B_HEAD_DIM] for h in range(heads)], axis=1)

    read, out_free, trans, inject = [], [], [], []
    for d in range(2):
        p_all, q_all = all_heads(sol, d, 0), all_heads(sol, d, B_HEAD_DIM)
        out_free.append(all_heads(free, d, 0))
        read.append(pre[d]["rt"] + all_heads(free, d, B_HEAD_DIM))
        bt_end_c = by_chunk(pre[d]["bt_end"])
        trans.append(_bdot_tn(p_all, bt_end_c))
        inject.append(_bdot_tn(jnp.concatenate([pre[d]["v"], q_all], axis=0),
                               jnp.concatenate([by_chunk(pre[d]["kt_end"]), -bt_end_c], axis=0)))

    state = [s_ref[0], s_ref[1]]
    for i in range(n_chunks):
        for d in range(2):
            c = n_chunks - 1 - i if d == 1 else i
            rows = slice(c * chunk, (c + 1) * chunk)
            cols = slice(c * w, (c + 1) * w)
            blocks = jnp.where(head_blk, jnp.concatenate([state[d]] * heads, axis=0), 0.0)
            out_refs[d][rows, :] = _bdot_nt(read[d][rows], blocks) + out_free[d][rows]
            inj = jnp.where(head_blk, inject[d][:, cols], 0.0)
            inj = sum(inj[h * B_HEAD_DIM:(h + 1) * B_HEAD_DIM] for h in range(heads))
            state[d] = (state[d] * pre[d]["decay_end"][c * chunk:c * chunk + 1, :]
                        - _bdot(state[d], jnp.where(head_blk, trans[d][:, cols], 0.0)) + inj)
    s_ref[0] = state[0]
    s_ref[1] = state[1]


def _rwkv_scan(r, v, kk, lw0, lw1, kd0, kd1, b0, b1, bsz, seq):
    tile = min(RWKV_SCAN_ROWS, seq)
    chunk = min(RWKV_CHUNK, tile)
    nb = seq // tile
    fwd = pl.BlockSpec((tile, RWKV_SCAN_LANES), lambda b, hg, c: (b * nb + c, hg))
    bwd = pl.BlockSpec((tile, RWKV_SCAN_LANES), lambda b, hg, c: (b * nb + nb - 1 - c, hg))
    out = jax.ShapeDtypeStruct(r.shape, F32)
    return pl.pallas_call(
        functools.partial(_rwkv_scan_kernel, chunk=chunk),
        grid=(bsz, B_WIDTH // RWKV_SCAN_LANES, nb),
        in_specs=[fwd] * 6 + [bwd] * 6,
        out_specs=[fwd, bwd],
        out_shape=[out, out],
        scratch_shapes=[pltpu.VMEM((2, B_HEAD_DIM, RWKV_SCAN_LANES), F32)],
        compiler_params=_params(3, 48),
        name="rwkv_scan",
    )(r, v, kk, lw0, kd0, b0, r, v, kk, lw1, kd1, b1)


def _rwkv_post_kernel(o0_ref, o1_ref, bonus_ref, g_ref, gn_g_ref, gn_b_ref, y_ref):
    ones = _head_ones()
    o = o0_ref[...] + o1_ref[...]
    mu = _head_sum(o, ones) * (1.0 / B_HEAD_DIM)
    oc = o - mu
    var = _head_sum(oc * oc, ones) * (1.0 / B_HEAD_DIM)
    normed = oc * lax.rsqrt(var + GN_EPS) * gn_g_ref[...] + gn_b_ref[...]
    y_ref[...] = ((normed + bonus_ref[...]) * g_ref[...]).astype(y_ref.dtype)


def _rwkv_post(o0, o1, bonus, g, gn_g, gn_b):
    n_tok = o0.shape[0]
    ts = min(RWKV_PREP_ROWS, n_tok)
    row_blk = pl.BlockSpec((ts, B_WIDTH), lambda i: (i, 0))
    vec = pl.BlockSpec((1, B_WIDTH), lambda i: (0, 0))
    return pl.pallas_call(
        _rwkv_post_kernel,
        grid=(n_tok // ts,),
        in_specs=[row_blk] * 4 + [vec, vec],
        out_specs=row_blk,
        out_shape=jax.ShapeDtypeStruct((n_tok, B_WIDTH), BF16),
        compiler_params=_params(1, 32),
        name="rwkv_post",
    )(o0, o1, bonus, g, gn_g.reshape(1, B_WIDTH), gn_b.reshape(1, B_WIDTH))


def _rwkv(p_b, mu_prev, mu_next, w0, w2, a0, a2, g2, k_k, k_a, r_k, gn_g, gn_b, bsz, seq):
    r, v, kk, lw0, lw1, kd0, kd1, b0, b1, g, bonus = _rwkv_prep(
        p_b, mu_prev, mu_next, w0, w2, a0, a2, g2, k_k, k_a, r_k, seq)
    o0, o1 = _rwkv_scan(r, v, kk, lw0, lw1, kd0, kd1, b0, b1, bsz, seq)
    return _rwkv_post(o0, o1, bonus, g, gn_g, gn_b)


def _merge_kernel(ya_ref, yb_ref, yc_ref, ga_ref, gb_ref, gc_ref, wa_ref, wb_ref, wc_ref, o_ref):
    def branch(y_ref, w_ref, gate_ref):
        return _sigmoid(gate_ref[...].astype(F32)) * jnp.dot(y_ref[...], w_ref[...], preferred_element_type=F32)

    merged = branch(ya_ref, wa_ref, ga_ref) + branch(yb_ref, wb_ref, gb_ref) + branch(yc_ref, wc_ref, gc_ref)
    o_ref[...] = merged.astype(o_ref.dtype)


def _merge(y_a, y_b, y_c, p_g, wb_a, wb_b, wb_c):
    n_tok = y_a.shape[0]
    tm = min(MERGE_ROWS, n_tok)
    tn = MERGE_COLS
    nj = D_MODEL // tn

    def rows(width):
        return pl.BlockSpec((tm, width), lambda i, j: (i, 0))

    def gate(branch):
        return pl.BlockSpec((tm, tn), lambda i, j: (i, branch * nj + j))

    def weight(depth):
        return pl.BlockSpec((depth, tn), lambda i, j: (0, j))

    return pl.pallas_call(
        _merge_kernel,
        grid=(n_tok // tm, nj),
        in_specs=[rows(A_Q), rows(B_WIDTH), rows(C_OUT), gate(0), gate(1), gate(2),
                  weight(A_Q), weight(B_WIDTH), weight(C_OUT)],
        out_specs=pl.BlockSpec((tm, tn), lambda i, j: (i, j)),
        out_shape=jax.ShapeDtypeStruct((n_tok, D_MODEL), BF16),
        compiler_params=_params(2, 32),
        name="gated_merge",
    )(y_a, y_b, y_c, p_g, p_g, p_g, wb_a, wb_b, wb_c)


def _first_lane_of(mask, lane):
    return jnp.min(jnp.where(mask, lane, float(ROUTER_COLS)), axis=-1, keepdims=True)


def _route_rows(logits):
    lane = lax.broadcasted_iota(jnp.int32, logits.shape, 1).astype(F32)
    grp_logit = jnp.where(lane < N_GROUPS, logits, -jnp.inf)
    e = jnp.exp(grp_logit - jnp.max(grp_logit, axis=-1, keepdims=True))
    prob = e / jnp.sum(e, axis=-1, keepdims=True)
    grp_w = jnp.max(prob, axis=-1, keepdims=True)
    grp = _first_lane_of(prob == grp_w, lane)
    first = N_GROUPS + grp * EXPERTS_PER_GROUP
    cand = jnp.where((lane >= first) & (lane < first + EXPERTS_PER_GROUP), logits, -jnp.inf)
    top1 = jnp.max(cand, axis=-1, keepdims=True)
    lane1 = _first_lane_of(cand == top1, lane)
    cand = jnp.where(lane == lane1, -jnp.inf, cand)
    top2 = jnp.max(cand, axis=-1, keepdims=True)
    lane2 = _first_lane_of(cand == top2, lane)
    e2 = jnp.exp(top2 - top1)
    gate1 = grp_w / (1.0 + e2)
    gate2 = grp_w * e2 / (1.0 + e2)
    return jnp.where(lane == 0, gate1,
                     jnp.where(lane == 1, gate2,
                               jnp.where(lane == 2, lane1 - N_GROUPS,
                                         jnp.where(lane == 3, lane2 - N_GROUPS, 0.0))))


def _out_proj_kernel(m_ref, x_ref, w_ref, g_ref, b_ref, wr_ref, br_ref, x_o, route_o, *, alpha):
    z = alpha * x_ref[...] + jnp.dot(m_ref[...], w_ref[...], preferred_element_type=F32)
    x1 = _layer_norm(z, g_ref[...], b_ref[...])
    x_o[...] = x1
    x_hi = x1.astype(BF16)
    x_lo = (x1 - x_hi.astype(F32)).astype(BF16)
    by_hi = jnp.dot(x_hi, wr_ref[...], preferred_element_type=F32)
    by_lo = jnp.dot(x_lo, wr_ref[:, :ROUTER_COLS], preferred_element_type=F32)
    logits = by_hi[:, :ROUTER_COLS] + by_hi[:, ROUTER_COLS:] + by_lo + br_ref[...]
    route_o[...] = _route_rows(logits)


def _out_proj(merged, x, w_out, ln_g, ln_b, w_router, b_router, alpha):
    n_tok = x.shape[0]
    tm = min(OUT_ROWS, n_tok)
    rows = lambda width: pl.BlockSpec((tm, width), lambda i: (i, 0))
    full = lambda shape: pl.BlockSpec(shape, lambda i: (0, 0))
    return pl.pallas_call(
        functools.partial(_out_proj_kernel, alpha=alpha),
        grid=(n_tok // tm,),
        in_specs=[rows(D_MODEL), rows(D_MODEL), full((D_MODEL, D_MODEL)), full((1, D_MODEL)),
                  full((1, D_MODEL)), full((D_MODEL, 2 * ROUTER_COLS)), full((1, ROUTER_COLS))],
        out_specs=[rows(D_MODEL), rows(ROUTER_COLS)],
        out_shape=[jax.ShapeDtypeStruct((n_tok, D_MODEL), F32),
                   jax.ShapeDtypeStruct((n_tok, ROUTER_COLS), F32)],
        compiler_params=_params(1, 48),
        name="out_proj_ln_router",
    )(merged, x, w_out, ln_g.reshape(1, D_MODEL), ln_b.reshape(1, D_MODEL), w_router, b_router)


def _dispatch(expert):
    n_tok = expert.shape[0]
    flat_e = expert.reshape(-1)
    n_asg = flat_e.shape[0]
    ids = jnp.arange(n_asg, dtype=jnp.int32)
    _, order = lax.sort((flat_e, ids), num_keys=1)
    _, rank = lax.sort((order, ids), num_keys=1)
    counts = jnp.bincount(flat_e, length=N_EXPERTS)
    padded = (counts + MOE_ROWS - 1) // MOE_ROWS * MOE_ROWS
    pad_end = jnp.cumsum(padded)
    pad_start = pad_end - padded
    start = jnp.cumsum(counts) - counts
    dest = (pad_start - start)[flat_e] + rank
    n_rows = (n_asg + MOE_ROWS - 1) // MOE_ROWS * MOE_ROWS + N_EXPERTS * MOE_ROWS
    n_blk = n_rows // MOE_ROWS
    blk_first = jnp.arange(n_blk) * MOE_ROWS
    blk_expert = jnp.minimum(jnp.searchsorted(pad_end, blk_first, side='right'), N_EXPERTS - 1)
    offset = blk_first - pad_start[blk_expert]
    blk_rows = jnp.clip(counts[blk_expert] - offset, 0, MOE_ROWS)
    blk_base = jnp.minimum(start[blk_expert] + offset, n_asg - 1)
    as_i32 = lambda a: a.astype(jnp.int32)
    return (as_i32(dest).reshape(n_tok, TOP_K), as_i32(order // TOP_K), as_i32(blk_expert), as_i32(blk_base),
            as_i32(blk_rows))


def _expert_kernel(blk_expert_ref, blk_base_ref, blk_rows_ref, tok_ref, x_hbm, wg_ref, wu_ref, wd_ref, o_ref,
                   x_buf, sem):
    i = pl.program_id(0)
    slot = i % 2
    last = pl.num_programs(0) - 1

    def gather(block, into):
        base = blk_base_ref[block]
        n = blk_rows_ref[block]

        def start_row(r, priority):
            tok = tok_ref[base + r]
            pltpu.make_async_copy(x_hbm.at[pl.ds(tok, 1)], x_buf.at[into, pl.ds(r, 1)],
                                  sem.at[into]).start(priority=priority)

        def body(pair, carry):
            start_row(2 * pair, 0)

            @pl.when(2 * pair + 1 < n)
            def _():
                start_row(2 * pair + 1, 1)

            return carry

        lax.fori_loop(0, (n + 1) // 2, body, 0)

    def wait_rows(n, which):
        for bit in range(MOE_ROWS.bit_length()):
            size = 1 << bit

            @pl.when((n & size) != 0)
            def _():
                pltpu.make_async_copy(x_hbm.at[pl.ds(0, size)], x_buf.at[which, pl.ds(0, size)],
                                      sem.at[which]).wait()

    @pl.when(i == 0)
    def _():
        x_buf[...] = jnp.zeros_like(x_buf)
        gather(0, 0)

    @pl.when(i < last)
    def _():
        gather(i + 1, 1 - slot)

    n_rows = blk_rows_ref[i]
    wait_rows(n_rows, slot)

    @pl.when(n_rows > 0)
    def _():
        x = x_buf[slot].astype(BF16)
        gate = jnp.dot(x, wg_ref[...].astype(BF16), preferred_element_type=F32)
        up = jnp.dot(x, wu_ref[...].astype(BF16), preferred_element_type=F32)
        hid = gate * _sigmoid(gate) * up
        o_ref[...] = jnp.dot(hid.astype(BF16), wd_ref[...].astype(BF16), preferred_element_type=F32)

    @pl.when(n_rows == 0)
    def _():
        o_ref[...] = jnp.zeros_like(o_ref)


def _experts(x, sorted_tok, blk_expert, blk_base, blk_rows, w_gate, w_up, w_down, layer):
    n_rows = blk_expert.shape[0] * MOE_ROWS
    grid_spec = pltpu.PrefetchScalarGridSpec(
        num_scalar_prefetch=4,
        grid=(n_rows // MOE_ROWS,),
        in_specs=[pl.BlockSpec(memory_space=pl.ANY),
                  pl.BlockSpec((None, None, D_MODEL, EXPERT_FF), lambda i, e, base, n, t: (layer, e[i], 0, 0)),
                  pl.BlockSpec((None, None, D_MODEL, EXPERT_FF), lambda i, e, base, n, t: (layer, e[i], 0, 0)),
                  pl.BlockSpec((None, None, EXPERT_FF, D_MODEL), lambda i, e, base, n, t: (layer, e[i], 0, 0))],
        out_specs=pl.BlockSpec((MOE_ROWS, D_MODEL), lambda i, e, base, n, t: (i, 0)),
        scratch_shapes=[pltpu.VMEM((2, MOE_ROWS, D_MODEL), F32), pltpu.SemaphoreType.DMA((2,))],
    )
    return pl.pallas_call(
        _expert_kernel,
        grid_spec=grid_spec,
        out_shape=jax.ShapeDtypeStruct((n_rows, D_MODEL), F32),
        compiler_params=_params(1, 56),
        name="expert_ffn",
    )(blk_expert, blk_base, blk_rows, sorted_tok, x, w_gate, w_up, w_down)


def _combine_kernel(x_ref, ya_ref, yb_ref, gate_ref, g_ref, b_ref, x_o, xb_o, *, alpha):
    gate = gate_ref[...]
    moe = gate[:, 0:1] * ya_ref[...] + gate[:, 1:2] * yb_ref[...]
    x2 = _layer_norm(alpha * x_ref[...] + moe, g_ref[...], b_ref[...])
    x_o[...] = x2
    xb_o[...] = x2.astype(BF16)


def _combine(x1, y_first, y_second, route, ln_g, ln_b, alpha):
    n_tok = x1.shape[0]
    tm = min(OUT_ROWS, n_tok)
    rows = lambda width: pl.BlockSpec((tm, width), lambda i: (i, 0))
    vec = pl.BlockSpec((1, D_MODEL), lambda i: (0, 0))
    return pl.pallas_call(
        functools.partial(_combine_kernel, alpha=alpha),
        grid=(n_tok // tm,),
        in_specs=[rows(D_MODEL), rows(D_MODEL), rows(D_MODEL), rows(LANES), vec, vec],
        out_specs=[rows(D_MODEL), rows(D_MODEL)],
        out_shape=[jax.ShapeDtypeStruct((n_tok, D_MODEL), F32),
                   jax.ShapeDtypeStruct((n_tok, D_MODEL), BF16)],
        compiler_params=_params(1, 32),
        name="moe_combine_ln",
    )(x1, y_first, y_second, route, ln_g.reshape(1, D_MODEL), ln_b.reshape(1, D_MODEL))


def _moe(x1, route, w_gate, w_up, w_down, layer, ln_g, ln_b, alpha):
    dest, sorted_tok, blk_expert, blk_base, blk_rows = _dispatch(route[:, TOP_K:2 * TOP_K].astype(jnp.int32))
    y_rows = _experts(x1, sorted_tok, blk_expert, blk_base, blk_rows, w_gate, w_up, w_down, layer)
    return _combine(x1, y_rows[dest[:, 0]], y_rows[dest[:, 1]], route, ln_g, ln_b, alpha)


def kernel(x, w_in, w_branch, w_out, mu_prev, mu_next, rwkv_w0, rwkv_w2, rwkv_a0, rwkv_a2, rwkv_g2, rwkv_k_k, rwkv_k_a, rwkv_r_k, rwkv_gn_g, rwkv_gn_b, q_norm, k_norm, rel_bias, ln1_g, ln1_b, router_group_w, router_group_b, router_expert_w, router_expert_b, w_gate, w_up, w_down, ln2_g, ln2_b):
    bsz, seq, _ = x.shape
    depth = w_in.shape[0]
    alpha = (2.0 * depth) ** 0.25
    xf = x.reshape(bsz * seq, D_MODEL)
    xb = xf.astype(BF16)
    col_b = A_COLS
    col_c = A_COLS + B_COLS
    col_g = A_COLS + B_COLS + C_COLS
    for l in range(depth):
        w_l = w_in[l]
        p_a = _matmul(xb, w_l[:, :col_b].astype(BF16), 768)
        p_b = _matmul(xb, w_l[:, col_b:col_c].astype(BF16), 1152)
        p_c = _matmul(xb, w_l[:, col_c:col_g].astype(BF16), 1152)
        p_g = _matmul(xb, w_l[:, col_g:].astype(BF16), 1024, out_dtype=BF16)
        y_a = _attention_a(p_a, q_norm[l], k_norm[l], bsz, seq)
        y_b = _rwkv(p_b, mu_prev[l], mu_next[l], rwkv_w0[l], rwkv_w2[l], rwkv_a0[l], rwkv_a2[l],
                    rwkv_g2[l], rwkv_k_k[l], rwkv_k_a[l], rwkv_r_k[l], rwkv_gn_g[l], rwkv_gn_b[l],
                    bsz, seq)
        y_c = _attention_c(p_c, rel_bias, bsz, seq)
        wb = w_branch[l].astype(BF16)
        merged = _merge(y_a, y_b, y_c, p_g, wb[:A_Q], wb[A_Q:A_Q + B_WIDTH], wb[A_Q + B_WIDTH:])
        w_router = jnp.pad(jnp.concatenate([router_group_w[l], router_expert_w[l]], axis=1),
                           ((0, 0), (0, ROUTER_COLS - N_GROUPS - N_EXPERTS)))
        w_router_hi = w_router.astype(BF16)
        w_router = jnp.concatenate([w_router_hi, (w_router - w_router_hi.astype(F32)).astype(BF16)], axis=1)
        b_router = jnp.pad(jnp.concatenate([router_group_b[l], router_expert_b[l]]),
                           (0, ROUTER_COLS - N_GROUPS - N_EXPERTS)).reshape(1, ROUTER_COLS)
        x1, route = _out_proj(merged, xf, w_out[l].astype(BF16), ln1_g[l], ln1_b[l],
                              w_router, b_router, alpha)
        xf, xb = _moe(x1, route, w_gate, w_up, w_down, l, ln2_g[l], ln2_b[l], alpha)
    return xf.reshape(bsz, seq, D_MODEL)
```
